```python
import math
import jax, jax.numpy as jnp
from jax import lax
import numpy as np

D_MODEL = 1024
BATCH = 32
SEQ = 256
DEPTH = 4
DEC_BATCH = 2
DEC_SEQ = 2048
PAST_LEN = 512

GRID_W = 64
N_EVEN = (DEPTH + 1) // 2
N_ODD = DEPTH // 2
A_WIDTH = D_MODEL // 2
A_HEADS = 4
A_DV = A_WIDTH // A_HEADS
A_DK = A_DV // 2
A_QK = A_HEADS * A_DK
A_RANK = 16
A_GATE_NORM = 16.0
GLA_CHUNK = 16
B_WIDTH = D_MODEL - A_WIDTH
POOL_WINDOWS = (2, 4, 8, 16)
B_GROUPS = len(POOL_WINDOWS)
B_GW = B_WIDTH // B_GROUPS
A_SPLITS = (A_QK, A_QK, A_WIDTH, A_WIDTH, 2 * A_RANK, B_WIDTH)
C_HEADS = 8
C_DK = D_MODEL // C_HEADS
C_DV = C_DK
C_WIDTH = C_HEADS * C_DV
SHORT_CONV = 3
DN_CHUNK = 64
C_SPLITS = (3 * C_WIDTH, C_WIDTH, 2 * C_HEADS, 2 * C_HEADS)
D_FF = 2816
FFN_CONV = 3
N_MOD = 6
ALPHA = (2 * DEPTH) ** 0.25
BETA_INIT = (8 * DEPTH) ** -0.25
EPS = 1e-6

kernel_name = "bidir_gla_pool_deltanet_convffn_diffusion_step"

F32 = jnp.float32


def _split(h, sizes):
    idx, acc = [], 0
    for s in sizes[:-1]:
        acc += s
        idx.append(acc)
    return jnp.split(h, idx, axis=-1)


def _layernorm(x, g, b):
    xf = x.astype(F32)
    mu = xf.mean(-1, keepdims=True)
    var = jnp.mean(jnp.square(xf - mu), -1, keepdims=True)
    return ((xf - mu) * lax.rsqrt(var + EPS) * g.astype(F32) + b.astype(F32)).astype(x.dtype)


def _rmsnorm_f32(x, g):
    xf = x.astype(F32)
    return xf * lax.rsqrt(jnp.mean(xf * xf, -1, keepdims=True) + EPS) * g.astype(F32)


def _l2norm(x):
    return x * lax.rsqrt(jnp.sum(x * x, -1, keepdims=True) + EPS)


def _dwconv(x, w):
    k = w.shape[0]
    p = k // 2
    L = x.shape[1]
    xp = jnp.pad(x, ((0, 0), (p, p), (0, 0)))
    return sum(xp[:, i:i + L] * w[i] for i in range(k))


def _grid_pos_embed(n_tokens):
    rows = n_tokens // GRID_W
    r = jnp.broadcast_to(jnp.arange(rows, dtype=F32)[:, None], (rows, GRID_W)).reshape(-1)
    col = jnp.broadcast_to(jnp.arange(GRID_W, dtype=F32)[None, :], (rows, GRID_W)).reshape(-1)
    quarter = D_MODEL // 4
    freq = jnp.exp(-math.log(10000.0) * jnp.arange(quarter, dtype=F32) / quarter)
    ra, ca = r[:, None] * freq, col[:, None] * freq
    return jnp.concatenate([jnp.sin(ra), jnp.cos(ra), jnp.sin(ca), jnp.cos(ca)], -1)


def _gla_chunk(q, k, v, g, s0):
    B_, H, L, dk = q.shape
    dv = v.shape[-1]
    n = L // GLA_CHUNK
    sp = lambda t: t.reshape(B_, H, n, GLA_CHUNK, t.shape[-1])
    q, k, v, g = sp(q), sp(k), sp(v), sp(g)
    b = jnp.cumsum(g, axis=3)
    b_last = b[:, :, :, -1]
    mask = jnp.tril(jnp.ones((GLA_CHUNK, GLA_CHUNK), bool))[:, :, None]
    dec = jnp.exp(jnp.where(mask, b[:, :, :, :, None, :] - b[:, :, :, None, :, :], -jnp.inf))
    att = jnp.einsum('bhnid,bhnjd,bhnijd->bhnij', q, k, dec)
    o_intra = jnp.einsum('bhnij,bhnjv->bhniv', att, v)
    qg = q * jnp.exp(b)
    kd = k * jnp.exp(b_last[:, :, :, None, :] - b)

    def step(s, xs):
        qg_c, kd_c, v_c, d_c = xs
        o = jnp.einsum('bhid,bhdv->bhiv', qg_c, s)
        s = s * d_c[..., None] + jnp.einsum('bhid,bhiv->bhdv', kd_c, v_c)
        return s, o

    xs = (jnp.moveaxis(qg, 2, 0), jnp.moveaxis(kd, 2, 0), jnp.moveaxis(v, 2, 0),
          jnp.moveaxis(jnp.exp(b_last), 2, 0))
    s_fin, o_inter = lax.scan(step, s0, xs)
    o = o_intra + jnp.moveaxis(o_inter, 0, 2)
    return o.reshape(B_, H, L, dv), s_fin


def _delta_chunk(q, k, v, g, beta, s0):
    B_, H, L, dk = q.shape
    dv = v.shape[-1]
    C = DN_CHUNK
    n = L // C
    q = q.reshape(B_, H, n, C, dk)
    k = k.reshape(B_, H, n, C, dk)
    v = v.reshape(B_, H, n, C, dv)
    g = g.reshape(B_, H, n, C)
    beta = beta.reshape(B_, H, n, C)
    b = jnp.cumsum(g, axis=-1)
    incl = jnp.tril(jnp.ones((C, C), bool))
    strict = jnp.tril(jnp.ones((C, C), bool), k=-1)
    ld = jnp.exp(jnp.where(incl, b[..., :, None] - b[..., None, :], -jnp.inf))
    kb = k * beta[..., None]
    kk = jnp.where(strict, jnp.einsum('bhnid,bhnjd->bhnij', kb, k) * ld, 0.0)
    eye = jnp.eye(C, dtype=F32)
    t = lax.linalg.triangular_solve(eye + kk, jnp.broadcast_to(eye, kk.shape),
                                    left_side=True, lower=True, unit_diagonal=True)
    u = jnp.einsum('bhnij,bhnjv->bhniv', t, v * beta[..., None])
    w = jnp.einsum('bhnij,bhnjd->bhnid', t, kb * jnp.exp(b)[..., None])
    aqk = jnp.einsum('bhnid,bhnjd->bhnij', q, k) * ld
    qd = q * jnp.exp(b)[..., None]
    kd = k * jnp.exp(b[..., -1:] - b)[..., None]
    dec = jnp.exp(b[..., -1])

    def step(s, xs):
        u_c, w_c, q_c, a_c, k_c, d_c = xs
        v_new = u_c - jnp.einsum('bhid,bhdv->bhiv', w_c, s)
        o = jnp.einsum('bhid,bhdv->bhiv', q_c, s) + jnp.einsum('bhij,bhjv->bhiv', a_c, v_new)
        s = s * d_c[..., None, None] + jnp.einsum('bhid,bhiv->bhdv', k_c, v_new)
        return s, o

    xs = tuple(jnp.moveaxis(a, 2, 0) for a in (u, w, qd, aqk, kd, dec))
    s_fin, o = lax.scan(step, s0, xs)
    return jnp.moveaxis(o, 0, 2).reshape(B_, H, L, dv), s_fin


def _pool_mix(u, w_grp, scale):
    B_, L, _ = u.shape
    uf = u.astype(F32).reshape(B_, L, B_GROUPS, B_GW)
    cs = jnp.pad(jnp.cumsum(uf, axis=1), ((0, 0), (1, 0), (0, 0), (0, 0)))
    pos = jnp.arange(L)
    outs = []
    for gi, win in enumerate(POOL_WINDOWS):
        lo = win // 2
        hi = win - 1 - lo
        start = jnp.clip(pos - lo, 0, L)
        end = jnp.clip(pos + hi + 1, 0, L)
        csg = cs[:, :, gi]
        cnt = (end - start).astype(F32)[None, :, None]
        outs.append((csg[:, end] - csg[:, start]) / cnt - uf[:, :, gi])
    pooled = jnp.stack(outs, axis=2).astype(u.dtype)
    mixed = jnp.einsum('blgc,gcd->blgd', pooled, w_grp)
    return mixed.reshape(B_, L, B_WIDTH) * scale


def _heads(t, d):
    B_, L = t.shape[0], t.shape[1]
    return t.reshape(B_, L, -1, d).transpose(0, 2, 1, 3).astype(F32)


def _gla_pool_mixer(u, s0, w_in, w_gate, b_gate, norm_g, pool_w, pool_s, w_out):
    B_, L, _ = u.shape
    q, k, v, r, lr, pz = _split(u @ w_in, A_SPLITS)
    qh = _heads(q, A_DK) * (A_DK ** -0.5)
    kh = _heads(k, A_DK)
    vh = _heads(v, A_DV)
    lr = lr.astype(F32).reshape(B_, L, 2, A_RANK)
    glog = jax.nn.log_sigmoid(jnp.einsum('blzr,zrk->blzk', lr, w_gate.astype(F32)) + b_gate.astype(F32)) / A_GATE_NORM
    g_f = _heads(glog[:, :, 0], A_DK)
    g_b = _heads(glog[:, :, 1], A_DK)
    fl = lambda t: jnp.flip(t, axis=2)
    o_f, s_f = _gla_chunk(qh, kh, vh, g_f, s0[:, 0])
    o_b, s_b = _gla_chunk(fl(qh), fl(kh), fl(vh), fl(g_b), s0[:, 1])
    o = (o_f + fl(o_b)).transpose(0, 2, 1, 3)
    o = _rmsnorm_f32(o, norm_g).reshape(B_, L, A_WIDTH).astype(u.dtype) * jax.nn.silu(r)
    p = _pool_mix(pz, pool_w, pool_s)
    y = jnp.concatenate([o, p], axis=-1) @ w_out
    return y, jnp.stack([s_f, s_b], axis=1)


def _deltanet_mixer(u, s0, w_in, conv_w, a_log, dt_bias, norm_g, w_out):
    B_, L, _ = u.shape
    qkv, z, a, bt = _split(u @ w_in, C_SPLITS)
    qkv = jax.nn.silu(_dwconv(qkv, conv_w))
    q, k, v = jnp.split(qkv, 3, axis=-1)
    qh = _l2norm(_heads(q, C_DK)) * (C_DK ** -0.5)
    kh = _l2norm(_heads(k, C_DK))
    vh = _heads(v, C_DV)
    a = a.astype(F32).reshape(B_, L, 2, C_HEADS)
    g = -jnp.exp(a_log.astype(F32)) * jax.nn.softplus(a + dt_bias.astype(F32))
    beta = jax.nn.sigmoid(bt.astype(F32).reshape(B_, L, 2, C_HEADS))
    g = g.transpose(2, 0, 3, 1)
    beta = beta.transpose(2, 0, 3, 1)
    fl = lambda t: jnp.flip(t, axis=2)
    o_f, s_f = _delta_chunk(qh, kh, vh, g[0], beta[0], s0[:, 0])
    o_b, s_b = _delta_chunk(fl(qh), fl(kh), fl(vh), fl(g[1]), fl(beta[1]), s0[:, 1])
    o = (o_f + fl(o_b)).transpose(0, 2, 1, 3)
    o = _rmsnorm_f32(o, norm_g).reshape(B_, L, C_WIDTH).astype(u.dtype) * jax.nn.silu(z)
    return o @ w_out, jnp.stack([s_f, s_b], axis=1)


def _conv_ffn(u, w_up, conv_w, w_down):
    h = _dwconv(u @ w_up, conv_w)
    a, gt = jnp.split(h, 2, axis=-1)
    return (jax.nn.silu(gt) * a) @ w_down


def _trunk(x, cond, gla_s0, dn_s0, w):
    sc = jax.nn.silu(cond.astype(F32)).astype(x.dtype)
    gla_out, dn_out = [], []
    for l in range(DEPTH):
        mod = (sc @ w['w_mod'][l] + w['b_mod'][l])[:, None, :]
        sh1, sc1, g1, sh2, sc2, g2 = jnp.split(mod, N_MOD, axis=-1)
        u = x * (1 + sc1) + sh1
        if l % 2 == 0:
            e = l // 2
            y, st = _gla_pool_mixer(u, gla_s0[:, e], w['a_w_in'][e], w['a_w_gate'][e], w['a_b_gate'][e],
                                    w['a_norm'][e], w['b_proj'][e], w['b_scale'][e], w['a_w_out'][e])
            gla_out.append(st)
        else:
            o_ = l // 2
            y, st = _deltanet_mixer(u, dn_s0[:, o_], w['c_w_in'][o_], w['c_conv'][o_], w['c_a_log'][o_],
                                    w['c_dt_bias'][o_], w['c_norm'][o_], w['c_w_out'][o_])
            dn_out.append(st)
        x = _layernorm(ALPHA * x + g1 * y.astype(x.dtype), w['ln1_g'][l], w['ln1_b'][l])
        u = x * (1 + sc2) + sh2
        f = _conv_ffn(u, w['f_w_up'][l], w['f_conv'][l], w['f_w_down'][l])
        x = _layernorm(ALPHA * x + g2 * f, w['ln2_g'][l], w['ln2_b'][l])
    return x, jnp.stack(gla_out, axis=1), jnp.stack(dn_out, axis=1)


def setup_inputs(seed: int = 0) -> dict:
    key = jax.random.key(seed)
    ks = jax.random.split(key, 32)
    nrm = lambda i, shape, s: jax.random.normal(ks[i], shape, F32) * s
    d_in_a = sum(A_SPLITS)
    d_in_c = sum(C_SPLITS)
    dt = jnp.exp(jax.random.uniform(ks[21], (N_ODD, 2, C_HEADS), F32) * (math.log(0.1) - math.log(0.001)) + math.log(0.001))
    return {
        "x_prompt": nrm(0, (BATCH, SEQ, D_MODEL), 1.0),
        "x_sample": nrm(1, (DEC_BATCH, DEC_SEQ, D_MODEL), 1.0),
        "state_gla": nrm(2, (DEC_BATCH, N_EVEN, 2, A_HEADS, A_DK, A_DV), 0.5),
        "state_dn": nrm(3, (DEC_BATCH, N_ODD, 2, C_HEADS, C_DK, C_DV), 1.0),
        "c": nrm(4, (DEC_BATCH, D_MODEL), 1.0),
        "c_ctx": nrm(5, (D_MODEL,), 1.0),
        "w_mod": nrm(6, (DEPTH, D_MODEL, N_MOD * D_MODEL), 0.5 * D_MODEL ** -0.5),
        "b_mod": nrm(7, (DEPTH, N_MOD * D_MODEL), 0.02),
        "ln1_g": 1.0 + nrm(8, (DEPTH, D_MODEL), 0.02),
        "ln1_b": nrm(9, (DEPTH, D_MODEL), 0.02),
        "ln2_g": 1.0 + nrm(10, (DEPTH, D_MODEL), 0.02),
        "ln2_b": nrm(11, (DEPTH, D_MODEL), 0.02),
        "a_w_in": nrm(12, (N_EVEN, D_MODEL, d_in_a), D_MODEL ** -0.5),
        "a_w_gate": nrm(13, (N_EVEN, 2, A_RANK, A_QK), A_RANK ** -0.5),
        "a_b_gate": nrm(14, (N_EVEN, 2, A_QK), 0.02),
        "a_norm": 1.0 + nrm(15, (N_EVEN, A_DV), 0.02),
        "b_proj": nrm(16, (N_EVEN, B_GROUPS, B_GW, B_GW), B_GW ** -0.5),
        "b_scale": 1.0 + nrm(17, (N_EVEN, B_WIDTH), 0.02),
        "a_w_out": nrm(18, (N_EVEN, A_WIDTH + B_WIDTH, D_MODEL), BETA_INIT * (A_WIDTH + B_WIDTH) ** -0.5),
        "c_w_in": nrm(19, (N_ODD, D_MODEL, d_in_c), D_MODEL ** -0.5),
        "c_conv": nrm(20, (N_ODD, SHORT_CONV, 3 * C_WIDTH), SHORT_CONV ** -0.5),
        "c_a_log": jnp.log(jax.random.uniform(ks[22], (N_ODD, 2, C_HEADS), F32, 1.0, 16.0)),
        "c_dt_bias": dt + jnp.log(-jnp.expm1(-dt)),
        "c_norm": 1.0 + nrm(23, (N_ODD, C_DV), 0.02),
        "c_w_out": nrm(24, (N_ODD, C_WIDTH, D_MODEL), BETA_INIT * C_WIDTH ** -0.5),
        "f_w_up": nrm(25, (DEPTH, D_MODEL, 2 * D_FF), D_MODEL ** -0.5),
        "f_conv": nrm(26, (DEPTH, FFN_CONV, 2 * D_FF), FFN_CONV ** -0.5),
        "f_w_down": nrm(27, (DEPTH, D_FF, D_MODEL), BETA_INIT * D_FF ** -0.5),
    }


def reference(x_prompt, x_sample, state_gla, state_dn, c, c_ctx, w_mod, b_mod, ln1_g, ln1_b, ln2_g, ln2_b,
              a_w_in, a_w_gate, a_b_gate, a_norm, b_proj, b_scale, a_w_out,
              c_w_in, c_conv, c_a_log, c_dt_bias, c_norm, c_w_out, f_w_up, f_conv, f_w_down):
    w = dict(w_mod=w_mod, b_mod=b_mod, ln1_g=ln1_g, ln1_b=ln1_b, ln2_g=ln2_g, ln2_b=ln2_b,
             a_w_in=a_w_in, a_w_gate=a_w_gate, a_b_gate=a_b_gate, a_norm=a_norm, b_proj=b_proj,
             b_scale=b_scale, a_w_out=a_w_out, c_w_in=c_w_in, c_conv=c_conv, c_a_log=c_a_log,
             c_dt_bias=c_dt_bias, c_norm=c_norm, c_w_out=c_w_out, f_w_up=f_w_up, f_conv=f_conv,
             f_w_down=f_w_down)
    nb = x_prompt.shape[0]
    g0 = jnp.zeros((nb, N_EVEN, 2, A_HEADS, A_DK, A_DV), F32)
    d0 = jnp.zeros((nb, N_ODD, 2, C_HEADS, C_DK, C_DV), F32)
    y_prompt, new_gla, new_dn = _trunk(x_prompt, c_ctx[None, :], g0, d0, w)
    pe = _grid_pos_embed(x_sample.shape[1]).astype(x_sample.dtype)
    y_sample, _, _ = _trunk(x_sample + pe, c, state_gla.astype(F32), state_dn.astype(F32), w)
    return (y_prompt, y_sample, new_gla.astype(x_prompt.dtype), new_dn.astype(x_prompt.dtype))
```

```python
import functools
import math

import numpy as np
import jax
import jax.numpy as jnp
from jax import lax
from jax.experimental import pallas as pl
from jax.experimental.pallas import tpu as pltpu

F32 = jnp.float32
BF16 = jnp.bfloat16

D_MODEL = 1024
DEPTH = 4
GRID_W = 64
A_HEADS = 4
A_DK = 64
A_DV = 128
A_QK = A_HEADS * A_DK
A_WIDTH = A_HEADS * A_DV
A_RANK = 16
A_GATE_NORM = 16.0
B_WIDTH = 512
POOL_WINDOWS = (2, 4, 8, 16)
B_GW = 128
C_HEADS = 8
C_DK = 128
C_WIDTH = 1024
D_FF = 2816
N_MOD = 6
ALPHA = (2 * DEPTH) ** 0.25
EPS = 1e-6

SEQ_TILE = 256
HALO = 8
MOD_ROWS = 8
V7X_VMEM_LIMIT = 56 * 1024 * 1024


def _cparams(sem):
    return pltpu.CompilerParams(dimension_semantics=sem, vmem_limit_bytes=V7X_VMEM_LIMIT)


class _Cfg:
    def __init__(self, n_ctx, n_smp, smp_len):
        assert smp_len % SEQ_TILE == 0 and smp_len & (smp_len - 1) == 0
        self.n_ctx, self.n_smp, self.smp_len = n_ctx, n_smp, smp_len
        self.t_ctx = n_ctx * SEQ_TILE
        self.t_all = self.t_ctx + n_smp * smp_len
        self.tps = smp_len // SEQ_TILE
        self.n_tiles = self.t_all // SEQ_TILE
        self.n_seq = n_ctx + n_smp
        tm = 1024
        while self.t_ctx % tm or smp_len % tm:
            tm //= 2
        self.tm = tm

    def cond_of_row(self, g):
        return jnp.where(g < self.t_ctx, 0, 1 + (g - self.t_ctx) // self.smp_len)

    def seq_of_tile(self, i):
        return jnp.where(i < self.n_ctx, i, self.n_ctx + (i - self.n_ctx) // self.tps)

    def rev_tile(self, i):
        m = (i - self.n_ctx) % self.tps
        return jnp.where(i < self.n_ctx, i, i - m + (self.tps - 1 - m))

    def first_of_seq(self, i):
        return jnp.logical_or(i < self.n_ctx, (i - self.n_ctx) % self.tps == 0)

    def last_of_seq(self, i):
        return jnp.logical_or(i < self.n_ctx, (i - self.n_ctx) % self.tps == self.tps - 1)


def _mod_map(cfg, layer, which, rows):
    def index_map(i, *_):
        return ((layer * MOD_ROWS + cfg.cond_of_row(i * rows)) * N_MOD + which, 0, 0)
    return index_map


def _mod_spec(cfg, layer, which, rows):
    return pl.BlockSpec((None, 1, D_MODEL), _mod_map(cfg, layer, which, rows))


def _halo_specs(cfg, rows, width, col_block=0):
    per = rows // HALO
    last = cfg.t_all // HALO - 1
    prev = pl.BlockSpec((HALO, width), lambda i, *_: (jnp.maximum(i * per - 1, 0), col_block))
    nxt = pl.BlockSpec((HALO, width), lambda i, *_: (jnp.minimum((i + 1) * per, last), col_block))
    return prev, nxt


def _seq_edges(cfg, tile, rows):
    g = tile * rows + lax.broadcasted_iota(jnp.int32, (rows, 1), 0)
    is_ctx = g < cfg.t_ctx
    length = jnp.where(is_ctx, SEQ_TILE, cfg.smp_len)
    pos = jnp.bitwise_and(jnp.where(is_ctx, g, g - cfg.t_ctx), length - 1)
    return pos, length


def _silu(x):
    return x * jax.nn.sigmoid(x)


def _dot(a, b):
    return jnp.dot(a, b, preferred_element_type=F32)


def _dot_nt(a, b):
    return lax.dot_general(a, b, (((1,), (1,)), ((), ())), preferred_element_type=F32)


def _split_bf16(x):
    hi = x.astype(BF16)
    lo = (x - hi.astype(F32)).astype(BF16)
    return hi, lo


def _layernorm_rows(x, g, b):
    mu = jnp.mean(x, axis=-1, keepdims=True)
    xc = x - mu
    var = jnp.mean(xc * xc, axis=-1, keepdims=True)
    return xc * lax.rsqrt(var + EPS) * g + b


def _embed_kernel(cfg, xp_ref, xs_ref, pe_ref, o_ref):
    i = pl.program_id(0)

    @pl.when(i < cfg.n_ctx)
    def _():
        o_ref[...] = xp_ref[...]

    @pl.when(i >= cfg.n_ctx)
    def _():
        o_ref[...] = xs_ref[...] + pe_ref[...]


def _embed(cfg, xp, xs, pe):
    n_ctx, tps = cfg.n_ctx, cfg.tps
    return pl.pallas_call(
        functools.partial(_embed_kernel, cfg),
        grid=(cfg.n_tiles,),
        in_specs=[
            pl.BlockSpec((SEQ_TILE, D_MODEL), lambda i: (jnp.minimum(i, n_ctx - 1), 0)),
            pl.BlockSpec((SEQ_TILE, D_MODEL), lambda i: (jnp.maximum(i - n_ctx, 0), 0)),
            pl.BlockSpec((SEQ_TILE, D_MODEL), lambda i: (jnp.maximum(i - n_ctx, 0) % tps, 0)),
        ],
        out_specs=pl.BlockSpec((SEQ_TILE, D_MODEL), lambda i: (i, 0)),
        out_shape=jax.ShapeDtypeStruct((cfg.t_all, D_MODEL), F32),
        compiler_params=_cparams(("arbitrary",)),
        name="embed",
    )(xp, xs, pe)


def _mod_kernel(c_ref, w_ref, b_ref, o_ref):
    sc = _silu(c_ref[...]).astype(BF16)
    o_ref[...] = _dot(sc, w_ref[...].astype(BF16)) + b_ref[...]


def _modulation(cond, w_mod, b_mod):
    out = pl.pallas_call(
        _mod_kernel,
        grid=(DEPTH, N_MOD),
        in_specs=[
            pl.BlockSpec((MOD_ROWS, D_MODEL), lambda l, j: (0, 0)),
            pl.BlockSpec((None, D_MODEL, D_MODEL), lambda l, j: (l, 0, j)),
            pl.BlockSpec((None, 1, D_MODEL), lambda l, j: (l, 0, j)),
        ],
        out_specs=pl.BlockSpec((None, MOD_ROWS, D_MODEL), lambda l, j: (l, 0, j)),
        out_shape=jax.ShapeDtypeStruct((DEPTH, MOD_ROWS, N_MOD * D_MODEL), F32),
        compiler_params=_cparams(("arbitrary", "arbitrary")),
        name="modulation",
    )(cond, w_mod, b_mod.reshape(DEPTH, 1, N_MOD * D_MODEL))
    return out.reshape(DEPTH * MOD_ROWS * N_MOD, 1, D_MODEL)


def _proj_kernel(x_ref, sc_ref, sh_ref, w_ref, o_ref, u_scr):
    @pl.when(pl.program_id(1) == 0)
    def _():
        u_scr[...] = (x_ref[...] * (1.0 + sc_ref[...]) + sh_ref[...]).astype(BF16)

    o_ref[...] = _dot(u_scr[...], w_ref[...].astype(BF16))


def _proj(cfg, x, mod, layer, w, tn):
    n = w.shape[1]
    tm = cfg.tm
    return pl.pallas_call(
        _proj_kernel,
        grid=(cfg.t_all // tm, n // tn),
        in_specs=[
            pl.BlockSpec((tm, D_MODEL), lambda i, j: (i, 0)),
            _mod_spec(cfg, layer, 1, tm),
            _mod_spec(cfg, layer, 0, tm),
            pl.BlockSpec((D_MODEL, tn), lambda i, j: (0, j)),
        ],
        out_specs=pl.BlockSpec((tm, tn), lambda i, j: (i, j)),
        out_shape=jax.ShapeDtypeStruct((cfg.t_all, n), F32),
        scratch_shapes=[pltpu.VMEM((tm, D_MODEL), BF16)],
        compiler_params=_cparams(("arbitrary", "arbitrary")),
        name=f"proj_l{layer}",
    )(x, mod, mod, w)


def _gate_kernel(x_ref, sc_ref, sh_ref, wlr_ref, wg_ref, bg_ref, o_ref):
    u = (x_ref[...] * (1.0 + sc_ref[...]) + sh_ref[...]).astype(BF16)
    lr = _dot(u, wlr_ref[...].astype(BF16))
    z = _dot(lr.astype(BF16), wg_ref[...].astype(BF16)) + bg_ref[...]
    log_sig = jnp.minimum(z, 0.0) - jnp.log1p(jnp.exp(-jnp.abs(z)))
    o_ref[...] = log_sig / A_GATE_NORM


def _gates(cfg, x, mod, layer, w_lr, w_gate_bd, b_gate):
    tm = cfg.tm
    return pl.pallas_call(
        _gate_kernel,
        grid=(cfg.t_all // tm,),
        in_specs=[
            pl.BlockSpec((tm, D_MODEL), lambda i: (i, 0)),
            _mod_spec(cfg, layer, 1, tm),
            _mod_spec(cfg, layer, 0, tm),
            pl.BlockSpec((D_MODEL, 128), lambda i: (0, 0)),
            pl.BlockSpec((128, 2 * A_QK), lambda i: (0, 0)),
            pl.BlockSpec((1, 2 * A_QK), lambda i: (0, 0)),
        ],
        out_specs=pl.BlockSpec((tm, 2 * A_QK), lambda i: (i, 0)),
        out_shape=jax.ShapeDtypeStruct((cfg.t_all, 2 * A_QK), F32),
        compiler_params=_cparams(("arbitrary",)),
        name=f"gates_l{layer}",
    )(x, mod, mod, w_lr, w_gate_bd, b_gate)


GLA_LEVELS = 8
GLA_STACK = GLA_LEVELS + 2


def _gla_constants():
    n = SEQ_TILE
    idx = np.arange(n)
    i, m = idx[:, None], idx[None, :]
    stack = np.zeros((2, GLA_STACK, n, n), np.float32)
    masks = np.zeros((2, GLA_LEVELS + 1, n, n), np.float32)
    stack[0, 0] = m <= i
    stack[0, 1] = m > i
    stack[1, 0] = m >= i
    stack[1, 1] = m < i
    for l in range(GLA_LEVELS):
        s = 1 << l
        blk = idx // (2 * s)
        upper = (idx // s) % 2 == 1
        piv_f = blk * 2 * s + s - 1
        piv_b = blk * 2 * s + s
        up, pf, pb = upper[:, None], piv_f[:, None], piv_b[:, None]
        stack[0, 2 + l] = np.where(up, (m > pf) & (m <= i), (m > i) & (m <= pf))
        stack[1, 2 + l] = np.where(up, (m >= pb) & (m < i), (m >= i) & (m < pb))
        same = blk[:, None] == blk[None, :]
        masks[0, l] = same & upper[:, None] & ~upper[None, :]
        masks[1, l] = same & ~upper[:, None] & upper[None, :]
    masks[:, GLA_LEVELS] = np.eye(n)
    return (jnp.asarray(stack.reshape(2, GLA_STACK * n, n), BF16), jnp.asarray(masks, F32))


def _gla_direction(qk, v, g, stack_ref, mask_ref, s_scr, o_ref):
    n = SEQ_TILE
    q = qk[:, :A_QK] * (A_DK ** -0.5)
    k = qk[:, A_QK:]
    g_hi, g_lo = _split_bf16(g)
    sums = _dot(stack_ref[...], g_hi) + _dot(stack_ref[...], g_lo)
    e_in = jnp.exp(sums[0:n])
    e_out = jnp.exp(sums[n:2 * n])
    lane = lax.broadcasted_iota(jnp.int32, (n, 128), 1)
    low_half = lane < A_DK

    att = [jnp.zeros((n, n), F32) for _ in range(A_HEADS)]
    for l in range(GLA_LEVELS + 1):
        if l < GLA_LEVELS:
            e = jnp.exp(sums[(2 + l) * n:(3 + l) * n])
            qs, ks = q * e, k * e
        else:
            qs, ks = q, k
        m = mask_ref[l]
        for p in range(A_HEADS // 2):
            qp = qs[:, 128 * p:128 * (p + 1)]
            kp = ks[:, 128 * p:128 * (p + 1)].astype(BF16)
            for hh in range(2):
                qh = jnp.where(low_half if hh == 0 else ~low_half, qp, 0.0).astype(BF16)
                att[2 * p + hh] = att[2 * p + hh] + m * _dot_nt(qh, kp)

    qd = q * e_in
    kd = k * e_out
    kd_t = kd.T.astype(BF16)
    g_t = g.T
    gt_hi, gt_lo = _split_bf16(g_t)
    ones = jnp.ones((n, 128), BF16)
    dec = jnp.exp(_dot(gt_hi, ones) + _dot(gt_lo, ones))
    for h in range(A_HEADS):
        p, hh = h // 2, h % 2
        v_h = v[:, A_DV * h:A_DV * (h + 1)].astype(BF16)
        s_pair = s_scr[p]
        qp = qd[:, 128 * p:128 * (p + 1)]
        qh = jnp.where(low_half if hh == 0 else ~low_half, qp, 0.0).astype(BF16)
        o_h = _dot(att[h].astype(BF16), v_h) + _dot(qh, s_pair.astype(BF16))
        o_ref[:, A_DV * h:A_DV * (h + 1)] = o_h
    for h in range(A_HEADS):
        p, hh = h // 2, h % 2
        v_h = v[:, A_DV * h:A_DV * (h + 1)].astype(BF16)
        rows = slice(A_DK * hh, A_DK * (hh + 1))
        s_old = s_scr[p, rows, :]
        s_scr[p, rows, :] = s_old * dec[A_DK * h:A_DK * (h + 1), :] + _dot(kd_t[A_DK * h:A_DK * (h + 1), :], v_h)


def _gla_kernel(cfg, qk_f, v_f, g_f, qk_b, v_b, g_b, s0_ref, stack_ref, mask_ref,
                of_ref, ob_ref, st_ref, sf_scr, sb_scr):
    i = pl.program_id(0)

    @pl.when(cfg.first_of_seq(i))
    def _():
        sf_scr[...] = s0_ref[0]
        sb_scr[...] = s0_ref[1]

    _gla_direction(qk_f[...], v_f[...], g_f[...], stack_ref.at[0], mask_ref.at[0], sf_scr, of_ref)
    _gla_direction(qk_b[...], v_b[...], g_b[...], stack_ref.at[1], mask_ref.at[1], sb_scr, ob_ref)

    @pl.when(cfg.last_of_seq(i))
    def _():
        st_ref[0] = sf_scr[...]
        st_ref[1] = sb_scr[...]


def _gla(cfg, proj, gates, s0, consts):
    stack, masks = consts
    rev = cfg.rev_tile
    n = SEQ_TILE
    st_shape = (cfg.n_seq, 2, A_HEADS // 2, 128, A_DV)
    return pl.pallas_call(
        functools.partial(_gla_kernel, cfg),
        grid=(cfg.n_tiles,),
        in_specs=[
            pl.BlockSpec((n, 2 * A_QK), lambda i: (i, 0)),
            pl.BlockSpec((n, A_WIDTH), lambda i: (i, 1)),
            pl.BlockSpec((n, A_QK), lambda i: (i, 0)),
            pl.BlockSpec((n, 2 * A_QK), lambda i: (rev(i), 0)),
            pl.BlockSpec((n, A_WIDTH), lambda i: (rev(i), 1)),
            pl.BlockSpec((n, A_QK), lambda i: (rev(i), 1)),
            pl.BlockSpec((None,) + st_shape[1:], lambda i: (cfg.seq_of_tile(i), 0, 0, 0, 0)),
            pl.BlockSpec(stack.shape, lambda i: (0, 0, 0)),
            pl.BlockSpec(masks.shape, lambda i: (0, 0, 0, 0)),
        ],
        out_specs=[
            pl.BlockSpec((n, A_WIDTH), lambda i: (i, 0)),
            pl.BlockSpec((n, A_WIDTH), lambda i: (rev(i), 0)),
            pl.BlockSpec((None,) + st_shape[1:], lambda i: (cfg.seq_of_tile(i), 0, 0, 0, 0)),
        ],
        out_shape=[
            jax.ShapeDtypeStruct((cfg.t_all, A_WIDTH), F32),
            jax.ShapeDtypeStruct((cfg.t_all, A_WIDTH), F32),
            jax.ShapeDtypeStruct(st_shape, F32),
        ],
        scratch_shapes=[pltpu.VMEM(st_shape[2:], F32), pltpu.VMEM(st_shape[2:], F32)],
        compiler_params=_cparams(("arbitrary",)),
        name="gla_scan",
    )(proj, proj, gates, proj, proj, gates, s0, stack, masks)


def _rms_heads(o, gain, n_heads, width):
    parts = []
    for h in range(n_heads):
        oh = o[:, width * h:width * (h + 1)]
        parts.append(oh * lax.rsqrt(jnp.mean(oh * oh, axis=-1, keepdims=True) + EPS) * gain)
    return parts


def _out_even_kernel(cfg, of_ref, ob_ref, r_ref, pz_ref, pzp_ref, pzn_ref, x_ref, g1_ref,
                     norm_ref, bproj_ref, bscale_ref, wout_ref, lng_ref, lnb_ref, o_ref, ext_scr):
    n = SEQ_TILE
    i = pl.program_id(0)
    o = of_ref[...] + ob_ref[...]
    r = r_ref[...]
    y = jnp.zeros((n, D_MODEL), F32)
    heads = _rms_heads(o, norm_ref[...], A_HEADS, A_DV)
    for h in range(A_HEADS):
        a = (heads[h] * _silu(r[:, A_DV * h:A_DV * (h + 1)])).astype(BF16)
        y = y + _dot(a, wout_ref[A_DV * h:A_DV * (h + 1), :].astype(BF16))

    pos, length = _seq_edges(cfg, i, n)
    ext_scr[0:HALO, :] = pzp_ref[...]
    ext_scr[HALO:HALO + n, :] = pz_ref[...]
    ext_scr[HALO + n:, :] = pzn_ref[...]
    for gi, win in enumerate(POOL_WINDOWS):
        lo = win // 2
        hi = win - 1 - lo
        cols = slice(B_GW * gi, B_GW * (gi + 1))
        acc = jnp.zeros((n, B_GW), F32)
        for d in range(-lo, hi + 1):
            valid = jnp.logical_and(pos + d >= 0, pos + d < length)
            acc = acc + jnp.where(valid, ext_scr[HALO + d:HALO + d + n, cols], 0.0)
        cnt = (jnp.minimum(pos + hi + 1, length) - jnp.maximum(pos - lo, 0)).astype(F32)
        pooled = (acc / cnt - ext_scr[HALO:HALO + n, cols]).astype(BF16)
        mixed = _dot(pooled, bproj_ref[gi].astype(BF16)) * bscale_ref[:, cols]
        y = y + _dot(mixed.astype(BF16), wout_ref[A_WIDTH + B_GW * gi:A_WIDTH + B_GW * (gi + 1), :].astype(BF16))

    o_ref[...] = _layernorm_rows(ALPHA * x_ref[...] + g1_ref[...] * y, lng_ref[...], lnb_ref[...])


def _out_even(cfg, of, ob, proj, x, mod, layer, norm_g, b_proj, b_scale, w_out, ln_g, ln_b):
    n = SEQ_TILE
    pz_prev, pz_next = _halo_specs(cfg, n, B_WIDTH, col_block=3)
    return pl.pallas_call(
        functools.partial(_out_even_kernel, cfg),
        grid=(cfg.n_tiles,),
        in_specs=[
            pl.BlockSpec((n, A_WIDTH), lambda i: (i, 0)),
            pl.BlockSpec((n, A_WIDTH), lambda i: (i, 0)),
            pl.BlockSpec((n, A_WIDTH), lambda i: (i, 2)),
            pl.BlockSpec((n, B_WIDTH), lambda i: (i, 3)),
            pz_prev, pz_next,
            pl.BlockSpec((n, D_MODEL), lambda i: (i, 0)),
            _mod_spec(cfg, layer, 2, n),
            pl.BlockSpec((1, A_DV), lambda i: (0, 0)),
            pl.BlockSpec((len(POOL_WINDOWS), B_GW, B_GW), lambda i: (0, 0, 0)),
            pl.BlockSpec((1, B_WIDTH), lambda i: (0, 0)),
            pl.BlockSpec((D_MODEL, D_MODEL), lambda i: (0, 0)),
            pl.BlockSpec((1, D_MODEL), lambda i: (0, 0)),
            pl.BlockSpec((1, D_MODEL), lambda i: (0, 0)),
        ],
        out_specs=pl.BlockSpec((n, D_MODEL), lambda i: (i, 0)),
        out_shape=jax.ShapeDtypeStruct((cfg.t_all, D_MODEL), F32),
        scratch_shapes=[pltpu.VMEM((n + 2 * HALO, B_WIDTH), F32)],
        compiler_params=_cparams(("arbitrary",)),
        name=f"out_even_l{layer}",
    )(of, ob, proj, proj, proj, proj, x, mod, norm_g, b_proj, b_scale, w_out, ln_g, ln_b)


def _qkv_conv_kernel(cfg, tm, tn, x_ref, xp_ref, xn_ref, sc_ref, sh_ref, w_ref, cw_ref, o_ref, u_scr, y_scr):
    i, j = pl.program_id(0), pl.program_id(1)

    @pl.when(j == 0)
    def _():
        sc, sh = 1.0 + sc_ref[...], sh_ref[...]
        u_scr[0:HALO, :] = (xp_ref[...] * sc + sh).astype(BF16)
        u_scr[HALO:HALO + tm, :] = (x_ref[...] * sc + sh).astype(BF16)
        u_scr[HALO + tm:, :] = (xn_ref[...] * sc + sh).astype(BF16)

    y_scr[...] = _dot(u_scr[...], w_ref[...].astype(BF16))
    pos, length = _seq_edges(cfg, i, tm)
    cw = cw_ref[...]
    h = (jnp.where(pos == 0, 0.0, y_scr[HALO - 1:HALO - 1 + tm, :] * cw[0:1, :])
         + y_scr[HALO:HALO + tm, :] * cw[1:2, :]
         + jnp.where(pos == length - 1, 0.0, y_scr[HALO + 1:HALO + 1 + tm, :] * cw[2:3, :]))
    h = _silu(h)
    is_q = j * tn < C_WIDTH
    is_qk = j * tn < 2 * C_WIDTH
    for c in range(tn // C_DK):
        hc = h[:, C_DK * c:C_DK * (c + 1)]
        inv = lax.rsqrt(jnp.sum(hc * hc, axis=-1, keepdims=True) + EPS)
        scale = jnp.where(is_qk, inv * jnp.where(is_q, C_DK ** -0.5, 1.0), 1.0)
        o_ref[:, C_DK * c:C_DK * (c + 1)] = hc * scale


def _qkv_conv(cfg, x, mod, layer, w, conv_w):
    tm, tn = cfg.tm, 256
    n = w.shape[1]
    xp, xn = _halo_specs(cfg, tm, D_MODEL)
    return pl.pallas_call(
        functools.partial(_qkv_conv_kernel, cfg, tm, tn),
        grid=(cfg.t_all // tm, n // tn),
        in_specs=[
            pl.BlockSpec((tm, D_MODEL), lambda i, j: (i, 0)),
            xp, xn,
            _mod_spec(cfg, layer, 1, tm),
            _mod_spec(cfg, layer, 0, tm),
            pl.BlockSpec((D_MODEL, tn), lambda i, j: (0, j)),
            pl.BlockSpec((3, tn), lambda i, j: (0, j)),
        ],
        out_specs=pl.BlockSpec((tm, tn), lambda i, j: (i, j)),
        out_shape=jax.ShapeDtypeStruct((cfg.t_all, n), F32),
        scratch_shapes=[pltpu.VMEM((tm + 2 * HALO, D_MODEL), BF16), pltpu.VMEM((tm + 2 * HALO, tn), F32)],
        compiler_params=_cparams(("arbitrary", "arbitrary")),
        name=f"qkv_conv_l{layer}",
    )(x, x, x, mod, mod, w, conv_w)


DN_BLOCK = 64
DN_NBLK = SEQ_TILE // DN_BLOCK


def _tri_ones():
    n = SEQ_TILE
    idx = np.arange(n)
    lower = (idx[None, :] <= idx[:, None]).astype(np.float32)
    return jnp.asarray(np.stack([lower, lower.T]), BF16)


def _unit_tri_inverse(a):
    n = DN_BLOCK
    row = lax.broadcasted_iota(jnp.int32, (n, n), 0)
    col = lax.broadcasted_iota(jnp.int32, (n, n), 1)
    eye = (row == col).astype(F32)
    t = eye - jnp.where((row >> 1) == (col >> 1), a, 0.0)
    for l in range(1, int(math.log2(n))):
        couple = jnp.logical_and((row >> (l + 1)) == (col >> (l + 1)), (row >> l) != (col >> l))
        tb = t.astype(BF16)
        w = _dot(jnp.where(couple, a, 0.0).astype(BF16), tb)
        t = t - _dot(tb, w.astype(BF16))
    return t


def _delta_direction(d, q_ref, k_ref, v_ref, ab, neg_a, dt_bias, tri_ref, s_scr, o_ref):
    n = SEQ_TILE
    row = lax.broadcasted_iota(jnp.int32, (n, n), 0)
    col = lax.broadcasted_iota(jnp.int32, (n, n), 1)
    incl = (col <= row) if d == 0 else (col >= row)
    strict = (col < row) if d == 0 else (col > row)
    last = n - 1 if d == 0 else 0

    a_in = ab + dt_bias
    softplus = jnp.maximum(a_in, 0.0) + jnp.log1p(jnp.exp(-jnp.abs(a_in)))
    g = neg_a * softplus
    beta = jax.nn.sigmoid(ab)
    g_hi, g_lo = _split_bf16(g)
    b_col = _dot(tri_ref[...], g_hi) + _dot(tri_ref[...], g_lo)
    b_row = b_col.T

    for h in range(C_HEADS):
        c = d * C_HEADS + h
        cols = slice(C_DK * h, C_DK * (h + 1))
        q, k, v = q_ref[:, cols], k_ref[:, cols], v_ref[:, cols]
        bc = b_col[:, c:c + 1]
        br = b_row[c:c + 1, :]
        b_last = b_col[last:last + 1, c:c + 1]
        ld = jnp.exp(jnp.where(incl, bc - br, -jnp.inf))
        bt = beta[:, 2 * C_HEADS + c:2 * C_HEADS + c + 1]
        kb = k * bt
        kbf = k.astype(BF16)
        a_mat = jnp.where(strict, _dot_nt(kb.astype(BF16), kbf) * ld, 0.0)
        aqk = (_dot_nt(q.astype(BF16), kbf) * ld).astype(BF16)
        e_col = jnp.exp(bc)
        rhs = jnp.concatenate([v * bt, kb * e_col], axis=1)

        order = range(DN_NBLK) if d == 0 else range(DN_NBLK - 1, -1, -1)
        x_blocks = [None] * DN_NBLK
        for b in order:
            rows = slice(DN_BLOCK * b, DN_BLOCK * (b + 1))
            a_rows = a_mat[rows, :]
            resid = rhs[rows, :]
            done = [bb for bb in range(DN_NBLK) if x_blocks[bb] is not None]
            for bb in done:
                a_blk = a_rows[:, DN_BLOCK * bb:DN_BLOCK * (bb + 1)].astype(BF16)
                resid = resid - _dot(a_blk, x_blocks[bb].astype(BF16))
            t_inv = _unit_tri_inverse(a_rows[:, rows])
            x_blocks[b] = _dot(t_inv.astype(BF16), resid.astype(BF16))
        x = jnp.concatenate(x_blocks, axis=0)
        u, w = x[:, :C_DK], x[:, C_DK:]

        s = s_scr[h]
        sb = s.astype(BF16)
        v_new = u - _dot(w.astype(BF16), sb)
        vb = v_new.astype(BF16)
        o_ref[:, cols] = _dot((q * e_col).astype(BF16), sb) + _dot(aqk, vb)
        kd = k * jnp.exp(b_last - bc)
        s_scr[h] = s * jnp.exp(b_last) + _dot(kd.T.astype(BF16), vb)


def _delta_kernel(cfg, qf, kf, vf, abf, qb, kb, vb, abb, s0_ref, alog_ref, dtb_ref, tri_ref,
                  of_ref, ob_ref, st_ref, sf_scr, sb_scr):
    i = pl.program_id(0)

    @pl.when(cfg.first_of_seq(i))
    def _():
        sf_scr[...] = s0_ref[0]
        sb_scr[...] = s0_ref[1]

    lane = lax.broadcasted_iota(jnp.int32, (1, 128), 1)
    neg_a = jnp.where(lane < 2 * C_HEADS, -jnp.exp(alog_ref[...]), 0.0)
    dtb = dtb_ref[...]
    _delta_direction(0, qf, kf, vf, abf[...], neg_a, dtb, tri_ref.at[0], sf_scr, of_ref)
    _delta_direction(1, qb, kb, vb, abb[...], neg_a, dtb, tri_ref.at[1], sb_scr, ob_ref)

    @pl.when(cfg.last_of_seq(i))
    def _():
        st_ref[0] = sf_scr[...]
        st_ref[1] = sb_scr[...]


def _delta(cfg, qkv, zab, s0, a_log, dt_bias, tri):
    rev = cfg.rev_tile
    n = SEQ_TILE
    st_shape = (cfg.n_seq, 2, C_HEADS, C_DK, C_DK)
    ab_col = C_WIDTH // 128
    in_specs = []
    for tile_of in (lambda i: i, rev):
        for cb in range(3):
            in_specs.append(pl.BlockSpec((n, C_WIDTH), functools.partial(lambda i, f, c: (f(i), c), f=tile_of, c=cb)))
        in_specs.append(pl.BlockSpec((n, 128), functools.partial(lambda i, f: (f(i), ab_col), f=tile_of)))
    in_specs += [
        pl.BlockSpec((None,) + st_shape[1:], lambda i: (cfg.seq_of_tile(i), 0, 0, 0, 0)),
        pl.BlockSpec((1, 128), lambda i: (0, 0)),
        pl.BlockSpec((1, 128), lambda i: (0, 0)),
        pl.BlockSpec(tri.shape, lambda i: (0, 0, 0)),
    ]
    return pl.pallas_call(
        functools.partial(_delta_kernel, cfg),
        grid=(cfg.n_tiles,),
        in_specs=in_specs,
        out_specs=[
            pl.BlockSpec((n, C_WIDTH), lambda i: (i, 0)),
            pl.BlockSpec((n, C_WIDTH), lambda i: (rev(i), 0)),
            pl.BlockSpec((None,) + st_shape[1:], lambda i: (cfg.seq_of_tile(i), 0, 0, 0, 0)),
        ],
        out_shape=[
            jax.ShapeDtypeStruct((cfg.t_all, C_WIDTH), F32),
            jax.ShapeDtypeStruct((cfg.t_all, C_WIDTH), F32),
            jax.ShapeDtypeStruct(st_shape, F32),
        ],
        scratch_shapes=[pltpu.VMEM(st_shape[2:], F32), pltpu.VMEM(st_shape[2:], F32)],
        compiler_params=_cparams(("arbitrary",)),
        name="delta_scan",
    )(qkv, qkv, qkv, zab, qkv, qkv, qkv, zab, s0, a_log, dt_bias, tri)


def _out_odd_kernel(of_ref, ob_ref, z_ref, x_ref, g1_ref, norm_ref, wout_ref, lng_ref, lnb_ref, o_ref):
    n = SEQ_TILE
    o = of_ref[...] + ob_ref[...]
    z = z_ref[...]
    y = jnp.zeros((n, D_MODEL), F32)
    heads = _rms_heads(o, norm_ref[...], C_HEADS, C_DK)
    for h in range(C_HEADS):
        a = (heads[h] * _silu(z[:, C_DK * h:C_DK * (h + 1)])).astype(BF16)
        y = y + _dot(a, wout_ref[C_DK * h:C_DK * (h + 1), :].astype(BF16))
    o_ref[...] = _layernorm_rows(ALPHA * x_ref[...] + g1_ref[...] * y, lng_ref[...], lnb_ref[...])


def _out_odd(cfg, of, ob, zab, x, mod, layer, norm_g, w_out, ln_g, ln_b):
    n = SEQ_TILE
    return pl.pallas_call(
        _out_odd_kernel,
        grid=(cfg.n_tiles,),
        in_specs=[
            pl.BlockSpec((n, C_WIDTH), lambda i: (i, 0)),
            pl.BlockSpec((n, C_WIDTH), lambda i: (i, 0)),
            pl.BlockSpec((n, C_WIDTH), lambda i: (i, 0)),
            pl.BlockSpec((n, D_MODEL), lambda i: (i, 0)),
            _mod_spec(cfg, layer, 2, n),
            pl.BlockSpec((1, C_DK), lambda i: (0, 0)),
            pl.BlockSpec((D_MODEL, D_MODEL), lambda i: (0, 0)),
            pl.BlockSpec((1, D_MODEL), lambda i: (0, 0)),
            pl.BlockSpec((1, D_MODEL), lambda i: (0, 0)),
        ],
        out_specs=pl.BlockSpec((n, D_MODEL), lambda i: (i, 0)),
        out_shape=jax.ShapeDtypeStruct((cfg.t_all, D_MODEL), F32),
        compiler_params=_cparams(("arbitrary",)),
        name=f"out_odd_l{layer}",
    )(of, ob, zab, x, mod, norm_g, w_out, ln_g, ln_b)


FFN_TF = 256


def _ffn_kernel(cfg, tm, x_ref, xp_ref, xn_ref, sc_ref, sh_ref, g2_ref, wa_ref, wg_ref, ca_ref, cg_ref,
                wd_ref, lng_ref, lnb_ref, o_ref, u_scr, ya_scr, yg_scr, acc_scr):
    i, j = pl.program_id(0), pl.program_id(1)

    @pl.when(j == 0)
    def _():
        sc, sh = 1.0 + sc_ref[...], sh_ref[...]
        u_scr[0:HALO, :] = (xp_ref[...] * sc + sh).astype(BF16)
        u_scr[HALO:HALO + tm, :] = (x_ref[...] * sc + sh).astype(BF16)
        u_scr[HALO + tm:, :] = (xn_ref[...] * sc + sh).astype(BF16)
        acc_scr[...] = jnp.zeros_like(acc_scr)

    u = u_scr[...]
    ya_scr[...] = _dot(u, wa_ref[...].astype(BF16))
    yg_scr[...] = _dot(u, wg_ref[...].astype(BF16))
    pos, length = _seq_edges(cfg, i, tm)
    first, last = pos == 0, pos == length - 1

    def conv(y_scr, cw):
        return (jnp.where(first, 0.0, y_scr[HALO - 1:HALO - 1 + tm, :] * cw[0:1, :])
                + y_scr[HALO:HALO + tm, :] * cw[1:2, :]
                + jnp.where(last, 0.0, y_scr[HALO + 1:HALO + 1 + tm, :] * cw[2:3, :]))

    act = (_silu(conv(yg_scr, cg_ref[...])) * conv(ya_scr, ca_ref[...])).astype(BF16)
    acc_scr[...] += _dot(act, wd_ref[...].astype(BF16))

    @pl.when(j == pl.num_programs(1) - 1)
    def _():
        o_ref[...] = _layernorm_rows(ALPHA * x_ref[...] + g2_ref[...] * acc_scr[...], lng_ref[...], lnb_ref[...])


def _ffn(cfg, x, mod, layer, w_up, conv_w, w_down, ln_g, ln_b):
    tm, tf = cfg.tm, FFN_TF
    nj = D_FF // tf
    xp, xn = _halo_specs(cfg, tm, D_MODEL)
    return pl.pallas_call(
        functools.partial(_ffn_kernel, cfg, tm),
        grid=(cfg.t_all // tm, nj),
        in_specs=[
            pl.BlockSpec((tm, D_MODEL), lambda i, j: (i, 0)),
            xp, xn,
            _mod_spec(cfg, layer, 4, tm),
            _mod_spec(cfg, layer, 3, tm),
            _mod_spec(cfg, layer, 5, tm),
            pl.BlockSpec((D_MODEL, tf), lambda i, j: (0, j)),
            pl.BlockSpec((D_MODEL, tf), lambda i, j: (0, nj + j)),
            pl.BlockSpec((3, tf), lambda i, j: (0, j)),
            pl.BlockSpec((3, tf), lambda i, j: (0, nj + j)),
            pl.BlockSpec((tf, D_MODEL), lambda i, j: (j, 0)),
            pl.BlockSpec((1, D_MODEL), lambda i, j: (0, 0)),
            pl.BlockSpec((1, D_MODEL), lambda i, j: (0, 0)),
        ],
        out_specs=pl.BlockSpec((tm, D_MODEL), lambda i, j: (i, 0)),
        out_shape=jax.ShapeDtypeStruct((cfg.t_all, D_MODEL), F32),
        scratch_shapes=[
            pltpu.VMEM((tm + 2 * HALO, D_MODEL), BF16),
            pltpu.VMEM((tm + 2 * HALO, tf), F32),
            pltpu.VMEM((tm + 2 * HALO, tf), F32),
            pltpu.VMEM((tm, D_MODEL), F32),
        ],
        compiler_params=_cparams(("arbitrary", "arbitrary")),
        name=f"ffn_l{layer}",
    )(x, x, x, mod, mod, mod, w_up, w_up, conv_w, conv_w, w_down, ln_g, ln_b)


def _grid_pos_embed(n_tokens):
    rows = n_tokens // GRID_W
    r = jnp.broadcast_to(jnp.arange(rows, dtype=F32)[:, None], (rows, GRID_W)).reshape(-1)
    col = jnp.broadcast_to(jnp.arange(GRID_W, dtype=F32)[None, :], (rows, GRID_W)).reshape(-1)
    quarter = D_MODEL // 4
    freq = jnp.exp(-math.log(10000.0) * jnp.arange(quarter, dtype=F32) / quarter)
    ra, ca = r[:, None] * freq, col[:, None] * freq
    return jnp.concatenate([jnp.sin(ra), jnp.cos(ra), jnp.sin(ca), jnp.cos(ca)], -1)


def kernel(x_prompt, x_sample, state_gla, state_dn, c, c_ctx, w_mod, b_mod, ln1_g, ln1_b, ln2_g, ln2_b,
           a_w_in, a_w_gate, a_b_gate, a_norm, b_proj, b_scale, a_w_out,
           c_w_in, c_conv, c_a_log, c_dt_bias, c_norm, c_w_out, f_w_up, f_conv, f_w_down):
    n_ctx, seq, _ = x_prompt.shape
    n_smp, smp_len, _ = x_sample.shape
    assert seq == SEQ_TILE and n_smp + 1 <= MOD_ROWS
    cfg = _Cfg(n_ctx, n_smp, smp_len)

    x = _embed(cfg, x_prompt.reshape(-1, D_MODEL), x_sample.reshape(-1, D_MODEL), _grid_pos_embed(smp_len))
    cond = jnp.concatenate([c_ctx[None, :], c, jnp.zeros((MOD_ROWS - 1 - n_smp, D_MODEL), F32)], axis=0)
    mod = _modulation(cond, w_mod, b_mod)

    gla_consts = _gla_constants()
    tri = _tri_ones()
    gla_states, dn_states = [], []
    for layer in range(DEPTH):
        ln1g, ln1b = ln1_g[layer][None, :], ln1_b[layer][None, :]
        if layer % 2 == 0:
            e = layer // 2
            w_in = a_w_in[e]
            w_main = jnp.concatenate([w_in[:, :2 * A_QK + 2 * A_WIDTH], w_in[:, -B_WIDTH:]], axis=1)
            lr0 = 2 * A_QK + 2 * A_WIDTH
            w_lr = jnp.pad(w_in[:, lr0:lr0 + 2 * A_RANK], ((0, 0), (0, 128 - 2 * A_RANK)))
            w_gate_bd = jnp.zeros((128, 2 * A_QK), F32)
            w_gate_bd = w_gate_bd.at[:A_RANK, :A_QK].set(a_w_gate[e, 0]).at[A_RANK:2 * A_RANK, A_QK:].set(a_w_gate[e, 1])
            proj = _proj(cfg, x, mod, layer, w_main, 512)
            gates = _gates(cfg, x, mod, layer, w_lr, w_gate_bd, a_b_gate[e].reshape(1, 2 * A_QK))
            s0 = jnp.concatenate([jnp.zeros((n_ctx, 2, A_HEADS, A_DK, A_DV), F32), state_gla[:, e].astype(F32)], axis=0)
            s0 = s0.reshape(cfg.n_seq, 2, A_HEADS // 2, 128, A_DV)
            of, ob, st = _gla(cfg, proj, gates, s0, gla_consts)
            gla_states.append(st[:n_ctx].reshape(n_ctx, 2, A_HEADS, A_DK, A_DV))
            x = _out_even(cfg, of, ob, proj, x, mod, layer, a_norm[e][None, :], b_proj[e], b_scale[e][None, :],
                          a_w_out[e], ln1g, ln1b)
        else:
            o_ = layer // 2
            w_in = c_w_in[o_]
            w_zab = jnp.pad(w_in[:, 3 * C_WIDTH:], ((0, 0), (0, 128 - 4 * C_HEADS)))
            qkv = _qkv_conv(cfg, x, mod, layer, w_in[:, :3 * C_WIDTH], c_conv[o_])
            zab = _proj(cfg, x, mod, layer, w_zab, 128)
            s0 = jnp.concatenate([jnp.zeros((n_ctx, 2, C_HEADS, C_DK, C_DK), F32), state_dn[:, o_].astype(F32)], axis=0)
            pad16 = lambda t: jnp.pad(t.reshape(1, -1), ((0, 0), (0, 128 - 2 * C_HEADS)))
            of, ob, st = _delta(cfg, qkv, zab, s0, pad16(c_a_log[o_]), pad16(c_dt_bias[o_]), tri)
            dn_states.append(st[:n_ctx])
            x = _out_odd(cfg, of, ob, zab, x, mod, layer, c_norm[o_][None, :], c_w_out[o_], ln1g, ln1b)
        x = _ffn(cfg, x, mod, layer, f_w_up[layer], f_conv[layer], f_w_down[layer],
                 ln2_g[layer][None, :], ln2_b[layer][None, :])

    y_prompt = x[:cfg.t_ctx].reshape(n_ctx, seq, D_MODEL)
    y_sample = x[cfg.t_ctx:].reshape(n_smp, smp_len, D_MODEL)
    return (y_prompt, y_sample, jnp.stack(gla_states, axis=1).astype(x_prompt.dtype),
            jnp.stack(dn_states, axis=1).astype(x_prompt.dtype))
```

```python
import functools
import math

import numpy as np
import jax
import jax.numpy as jnp
from jax import lax
from jax.experimental import pallas as pl
from jax.experimental.pallas import tpu as pltpu

F32 = jnp.float32
BF16 = jnp.bfloat16

D_MODEL = 1024
DEPTH = 4
GRID_W = 64
A_HEADS = 4
A_DK = 64
A_DV = 128
A_QK = A_HEADS * A_DK
A_WIDTH = A_HEADS * A_DV
A_RANK = 16
A_GATE_NORM = 16.0
B_WIDTH = 512
POOL_WINDOWS = (2, 4, 8, 16)
B_GW = 128
C_HEADS = 8
C_DK = 128
C_WIDTH = 1024
D_FF = 2816
N_MOD = 6
ALPHA = (2 * DEPTH) ** 0.25
EPS = 1e-6

SEQ_TILE = 256
HALO = 8
MOD_ROWS = 8
V7X_VMEM_LIMIT = 56 * 1024 * 1024


def _cparams(sem):
    return pltpu.CompilerParams(dimension_semantics=sem, vmem_limit_bytes=V7X_VMEM_LIMIT)


class _Cfg:
    def __init__(self, n_ctx, n_smp, smp_len):
        assert smp_len % SEQ_TILE == 0 and smp_len & (smp_len - 1) == 0
        self.n_ctx, self.n_smp, self.smp_len = n_ctx, n_smp, smp_len
        self.t_ctx = n_ctx * SEQ_TILE
        self.t_all = self.t_ctx + n_smp * smp_len
        self.tps = smp_len // SEQ_TILE
        self.n_tiles = self.t_all // SEQ_TILE
        self.n_seq = n_ctx + n_smp
        tm = 1024
        while self.t_ctx % tm or smp_len % tm:
            tm //= 2
        self.tm = tm

    def cond_of_row(self, g):
        return jnp.where(g < self.t_ctx, 0, 1 + (g - self.t_ctx) // self.smp_len)

    def seq_of_tile(self, i):
        return jnp.where(i < self.n_ctx, i, self.n_ctx + (i - self.n_ctx) // self.tps)

    def rev_tile(self, i):
        m = (i - self.n_ctx) % self.tps
        return jnp.where(i < self.n_ctx, i, i - m + (self.tps - 1 - m))

    def first_of_seq(self, i):
        return jnp.logical_or(i < self.n_ctx, (i - self.n_ctx) % self.tps == 0)

    def last_of_seq(self, i):
        return jnp.logical_or(i < self.n_ctx, (i - self.n_ctx) % self.tps == self.tps - 1)


def _mod_map(cfg, layer, which, rows):
    def index_map(i, *_):
        return ((layer * MOD_ROWS + cfg.cond_of_row(i * rows)) * N_MOD + which, 0, 0)
    return index_map


def _mod_spec(cfg, layer, which, rows):
    return pl.BlockSpec((None, 1, D_MODEL), _mod_map(cfg, layer, which, rows))


def _halo_specs(cfg, rows, width, col_block=0):
    per = rows // HALO
    last = cfg.t_all // HALO - 1
    prev = pl.BlockSpec((HALO, width), lambda i, *_: (jnp.maximum(i * per - 1, 0), col_block))
    nxt = pl.BlockSpec((HALO, width), lambda i, *_: (jnp.minimum((i + 1) * per, last), col_block))
    return prev, nxt


def _seq_edges(cfg, tile, rows):
    g = tile * rows + lax.broadcasted_iota(jnp.int32, (rows, 1), 0)
    is_ctx = g < cfg.t_ctx
    length = jnp.where(is_ctx, SEQ_TILE, cfg.smp_len)
    pos = jnp.bitwise_and(jnp.where(is_ctx, g, g - cfg.t_ctx), length - 1)
    return pos, length


def _silu(x):
    return x * jax.nn.sigmoid(x)


def _dot(a, b):
    return jnp.dot(a, b, preferred_element_type=F32)


def _dot_nt(a, b):
    return lax.dot_general(a, b, (((1,), (1,)), ((), ())), preferred_element_type=F32)


def _split_bf16(x):
    hi = x.astype(BF16)
    lo = (x - hi.astype(F32)).astype(BF16)
    return hi, lo


def _layernorm_rows(x, g, b):
    mu = jnp.mean(x, axis=-1, keepdims=True)
    xc = x - mu
    var = jnp.mean(xc * xc, axis=-1, keepdims=True)
    return xc * lax.rsqrt(var + EPS) * g + b


def _embed_kernel(cfg, xp_ref, xs_ref, pe_ref, o_ref):
    i = pl.program_id(0)

    @pl.when(i < cfg.n_ctx)
    def _():
        o_ref[...] = xp_ref[...]

    @pl.when(i >= cfg.n_ctx)
    def _():
        o_ref[...] = xs_ref[...] + pe_ref[...]


def _embed(cfg, xp, xs, pe):
    n_ctx, tps = cfg.n_ctx, cfg.tps
    return pl.pallas_call(
        functools.partial(_embed_kernel, cfg),
        grid=(cfg.n_tiles,),
        in_specs=[
            pl.BlockSpec((SEQ_TILE, D_MODEL), lambda i: (jnp.minimum(i, n_ctx - 1), 0)),
            pl.BlockSpec((SEQ_TILE, D_MODEL), lambda i: (jnp.maximum(i - n_ctx, 0), 0)),
            pl.BlockSpec((SEQ_TILE, D_MODEL), lambda i: (jnp.maximum(i - n_ctx, 0) % tps, 0)),
        ],
        out_specs=pl.BlockSpec((SEQ_TILE, D_MODEL), lambda i: (i, 0)),
        out_shape=jax.ShapeDtypeStruct((cfg.t_all, D_MODEL), F32),
        compiler_params=_cparams(("arbitrary",)),
        name="embed",
    )(xp, xs, pe)


def _mod_kernel(c_ref, w_ref, b_ref, o_ref):
    sc = _silu(c_ref[...]).astype(BF16)
    o_ref[...] = _dot(sc, w_ref[...].astype(BF16)) + b_ref[...]


def _modulation(cond, w_mod, b_mod):
    out = pl.pallas_call(
        _mod_kernel,
        grid=(DEPTH, N_MOD),
        in_specs=[
            pl.BlockSpec((MOD_ROWS, D_MODEL), lambda l, j: (0, 0)),
            pl.BlockSpec((None, D_MODEL, D_MODEL), lambda l, j: (l, 0, j)),
            pl.BlockSpec((None, 1, D_MODEL), lambda l, j: (l, 0, j)),
        ],
        out_specs=pl.BlockSpec((None, MOD_ROWS, D_MODEL), lambda l, j: (l, 0, j)),
        out_shape=jax.ShapeDtypeStruct((DEPTH, MOD_ROWS, N_MOD * D_MODEL), F32),
        compiler_params=_cparams(("arbitrary", "arbitrary")),
        name="modulation",
    )(cond, w_mod, b_mod.reshape(DEPTH, 1, N_MOD * D_MODEL))
    return out.reshape(DEPTH * MOD_ROWS * N_MOD, 1, D_MODEL)


def _proj_kernel(x_ref, sc_ref, sh_ref, w_ref, o_ref, u_scr):
    @pl.when(pl.program_id(1) == 0)
    def _():
        u_scr[...] = (x_ref[...] * (1.0 + sc_ref[...]) + sh_ref[...]).astype(BF16)

    o_ref[...] = _dot(u_scr[...], w_ref[...].astype(BF16))


def _proj(cfg, x, mod, layer, w, tn):
    n = w.shape[1]
    tm = cfg.tm
    return pl.pallas_call(
        _proj_kernel,
        grid=(cfg.t_all // tm, n // tn),
        in_specs=[
            pl.BlockSpec((tm, D_MODEL), lambda i, j: (i, 0)),
            _mod_spec(cfg, layer, 1, tm),
            _mod_spec(cfg, layer, 0, tm),
            pl.BlockSpec((D_MODEL, tn), lambda i, j: (0, j)),
        ],
        out_specs=pl.BlockSpec((tm, tn), lambda i, j: (i, j)),
        out_shape=jax.ShapeDtypeStruct((cfg.t_all, n), F32),
        scratch_shapes=[pltpu.VMEM((tm, D_MODEL), BF16)],
        compiler_params=_cparams(("arbitrary", "arbitrary")),
        name=f"proj_l{layer}",
    )(x, mod, mod, w)


def _gate_kernel(x_ref, sc_ref, sh_ref, wlr_ref, wg_ref, bg_ref, o_ref):
    u = (x_ref[...] * (1.0 + sc_ref[...]) + sh_ref[...]).astype(BF16)
    lr = _dot(u, wlr_ref[...].astype(BF16))
    z = _dot(lr.astype(BF16), wg_ref[...].astype(BF16)) + bg_ref[...]
    log_sig = jnp.minimum(z, 0.0) - jnp.log1p(jnp.exp(-jnp.abs(z)))
    o_ref[...] = log_sig / A_GATE_NORM


def _gates(cfg, x, mod, layer, w_lr, w_gate_bd, b_gate):
    tm = cfg.tm
    return pl.pallas_call(
        _gate_kernel,
        grid=(cfg.t_all // tm,),
        in_specs=[
            pl.BlockSpec((tm, D_MODEL), lambda i: (i, 0)),
            _mod_spec(cfg, layer, 1, tm),
            _mod_spec(cfg, layer, 0, tm),
            pl.BlockSpec((D_MODEL, 128), lambda i: (0, 0)),
            pl.BlockSpec((128, 2 * A_QK), lambda i: (0, 0)),
            pl.BlockSpec((1, 2 * A_QK), lambda i: (0, 0)),
        ],
        out_specs=pl.BlockSpec((tm, 2 * A_QK), lambda i: (i, 0)),
        out_shape=jax.ShapeDtypeStruct((cfg.t_all, 2 * A_QK), F32),
        compiler_params=_cparams(("arbitrary",)),
        name=f"gates_l{layer}",
    )(x, mod, mod, w_lr, w_gate_bd, b_gate)


GLA_LEVELS = 8
GLA_STACK = GLA_LEVELS + 2


def _gla_constants():
    n = SEQ_TILE
    idx = np.arange(n)
    i, m = idx[:, None], idx[None, :]
    stack = np.zeros((2, GLA_STACK, n, n), np.float32)
    masks = np.zeros((2, GLA_LEVELS + 1, n, n), np.float32)
    stack[0, 0] = m <= i
    stack[0, 1] = m > i
    stack[1, 0] = m >= i
    stack[1, 1] = m < i
    for l in range(GLA_LEVELS):
        s = 1 << l
        blk = idx // (2 * s)
        upper = (idx // s) % 2 == 1
        piv_f = blk * 2 * s + s - 1
        piv_b = blk * 2 * s + s
        up, pf, pb = upper[:, None], piv_f[:, None], piv_b[:, None]
        stack[0, 2 + l] = np.where(up, (m > pf) & (m <= i), (m > i) & (m <= pf))
        stack[1, 2 + l] = np.where(up, (m >= pb) & (m < i), (m >= i) & (m < pb))
        same = blk[:, None] == blk[None, :]
        masks[0, l] = same & upper[:, None] & ~upper[None, :]
        masks[1, l] = same & ~upper[:, None] & upper[None, :]
    masks[:, GLA_LEVELS] = np.eye(n)
    return (jnp.asarray(stack.reshape(2, GLA_STACK * n, n), BF16), jnp.asarray(masks, F32))


def _gla_direction(qk, v, g, stack_ref, mask_ref, s_scr, o_ref):
    n = SEQ_TILE
    q = qk[:, :A_QK] * (A_DK ** -0.5)
    k = qk[:, A_QK:]
    g_hi, g_lo = _split_bf16(g)
    sums = _dot(stack_ref[...], g_hi) + _dot(stack_ref[...], g_lo)
    e_in = jnp.exp(sums[0:n])
    e_out = jnp.exp(sums[n:2 * n])
    lane = lax.broadcasted_iota(jnp.int32, (n, 128), 1)
    low_half = lane < A_DK

    att = [jnp.zeros((n, n), F32) for _ in range(A_HEADS)]
    for l in range(GLA_LEVELS + 1):
        if l < GLA_LEVELS:
            e = jnp.exp(sums[(2 + l) * n:(3 + l) * n])
            qs, ks = q * e, k * e
        else:
            qs, ks = q, k
        m = mask_ref[l]
        for p in range(A_HEADS // 2):
            qp = qs[:, 128 * p:128 * (p + 1)]
            kp = ks[:, 128 * p:128 * (p + 1)].astype(BF16)
            for hh in range(2):
                qh = jnp.where(low_half if hh == 0 else ~low_half, qp, 0.0).astype(BF16)
                att[2 * p + hh] = att[2 * p + hh] + m * _dot_nt(qh, kp)

    qd = q * e_in
    kd = k * e_out
    kd_t = kd.T.astype(BF16)
    g_t = g.T
    gt_hi, gt_lo = _split_bf16(g_t)
    ones = jnp.ones((n, 128), BF16)
    dec = jnp.exp(_dot(gt_hi, ones) + _dot(gt_lo, ones))
    for h in range(A_HEADS):
        p, hh = h // 2, h % 2
        v_h = v[:, A_DV * h:A_DV * (h + 1)].astype(BF16)
        s_pair = s_scr[p]
        qp = qd[:, 128 * p:128 * (p + 1)]
        qh = jnp.where(low_half if hh == 0 else ~low_half, qp, 0.0).astype(BF16)
        o_h = _dot(att[h].astype(BF16), v_h) + _dot(qh, s_pair.astype(BF16))
        o_ref[:, A_DV * h:A_DV * (h + 1)] = o_h
    for h in range(A_HEADS):
        p, hh = h // 2, h % 2
        v_h = v[:, A_DV * h:A_DV * (h + 1)].astype(BF16)
        rows = slice(A_DK * hh, A_DK * (hh + 1))
        s_old = s_scr[p, rows, :]
        s_scr[p, rows, :] = s_old * dec[A_DK * h:A_DK * (h + 1), :] + _dot(kd_t[A_DK * h:A_DK * (h + 1), :], v_h)


def _gla_kernel(cfg, qk_f, v_f, g_f, qk_b, v_b, g_b, s0_ref, stack_ref, mask_ref,
                of_ref, ob_ref, st_ref, sf_scr, sb_scr):
    i = pl.program_id(0)

    @pl.when(cfg.first_of_seq(i))
    def _():
        sf_scr[...] = s0_ref[0]
        sb_scr[...] = s0_ref[1]

    _gla_direction(qk_f[...], v_f[...], g_f[...], stack_ref.at[0], mask_ref.at[0], sf_scr, of_ref)
    _gla_direction(qk_b[...], v_b[...], g_b[...], stack_ref.at[1], mask_ref.at[1], sb_scr, ob_ref)

    @pl.when(cfg.last_of_seq(i))
    def _():
        st_ref[0] = sf_scr[...]
        st_ref[1] = sb_scr[...]


def _gla(cfg, proj, gates, s0, consts):
    stack, masks = consts
    rev = cfg.rev_tile
    n = SEQ_TILE
    st_shape = (cfg.n_seq, 2, A_HEADS // 2, 128, A_DV)
    return pl.pallas_call(
        functools.partial(_gla_kernel, cfg),
        grid=(cfg.n_tiles,),
        in_specs=[
            pl.BlockSpec((n, 2 * A_QK), lambda i: (i, 0)),
            pl.BlockSpec((n, A_WIDTH), lambda i: (i, 1)),
            pl.BlockSpec((n, A_QK), lambda i: (i, 0)),
            pl.BlockSpec((n, 2 * A_QK), lambda i: (rev(i), 0)),
            pl.BlockSpec((n, A_WIDTH), lambda i: (rev(i), 1)),
            pl.BlockSpec((n, A_QK), lambda i: (rev(i), 1)),
            pl.BlockSpec((None,) + st_shape[1:], lambda i: (cfg.seq_of_tile(i), 0, 0, 0, 0)),
            pl.BlockSpec(stack.shape, lambda i: (0, 0, 0)),
            pl.BlockSpec(masks.shape, lambda i: (0, 0, 0, 0)),
        ],
        out_specs=[
            pl.BlockSpec((n, A_WIDTH), lambda i: (i, 0)),
            pl.BlockSpec((n, A_WIDTH), lambda i: (rev(i), 0)),
            pl.BlockSpec((None,) + st_shape[1:], lambda i: (cfg.seq_of_tile(i), 0, 0, 0, 0)),
        ],
        out_shape=[
            jax.ShapeDtypeStruct((cfg.t_all, A_WIDTH), F32),
            jax.ShapeDtypeStruct((cfg.t_all, A_WIDTH), F32),
            jax.ShapeDtypeStruct(st_shape, F32),
        ],
        scratch_shapes=[pltpu.VMEM(st_shape[2:], F32), pltpu.VMEM(st_shape[2:], F32)],
        compiler_params=_cparams(("arbitrary",)),
        name="gla_scan",
    )(proj, proj, gates, proj, proj, gates, s0, stack, masks)


def _rms_heads(o, gain, n_heads, width):
    parts = []
    for h in range(n_heads):
        oh = o[:, width * h:width * (h + 1)]
        parts.append(oh * lax.rsqrt(jnp.mean(oh * oh, axis=-1, keepdims=True) + EPS) * gain)
    return parts


def _out_even_kernel(cfg, of_ref, ob_ref, r_ref, pz_ref, pzp_ref, pzn_ref, x_ref, g1_ref,
                     norm_ref, bproj_ref, bscale_ref, wout_ref, lng_ref, lnb_ref, o_ref, ext_scr):
    n = SEQ_TILE
    i = pl.program_id(0)
    o = of_ref[...] + ob_ref[...]
    r = r_ref[...]
    y = jnp.zeros((n, D_MODEL), F32)
    heads = _rms_heads(o, norm_ref[...], A_HEADS, A_DV)
    for h in range(A_HEADS):
        a = (heads[h] * _silu(r[:, A_DV * h:A_DV * (h + 1)])).astype(BF16)
        y = y + _dot(a, wout_ref[A_DV * h:A_DV * (h + 1), :].astype(BF16))

    pos, length = _seq_edges(cfg, i, n)
    ext_scr[0:HALO, :] = pzp_ref[...]
    ext_scr[HALO:HALO + n, :] = pz_ref[...]
    ext_scr[HALO + n:, :] = pzn_ref[...]
    for gi, win in enumerate(POOL_WINDOWS):
        lo = win // 2
        hi = win - 1 - lo
        cols = slice(B_GW * gi, B_GW * (gi + 1))
        acc = jnp.zeros((n, B_GW), F32)
        for d in range(-lo, hi + 1):
            valid = jnp.logical_and(pos + d >= 0, pos + d < length)
            acc = acc + jnp.where(valid, ext_scr[HALO + d:HALO + d + n, cols], 0.0)
        cnt = (jnp.minimum(pos + hi + 1, length) - jnp.maximum(pos - lo, 0)).astype(F32)
        pooled = (acc / cnt - ext_scr[HALO:HALO + n, cols]).astype(BF16)
        mixed = _dot(pooled, bproj_ref[gi].astype(BF16)) * bscale_ref[:, cols]
        y = y + _dot(mixed.astype(BF16), wout_ref[A_WIDTH + B_GW * gi:A_WIDTH + B_GW * (gi + 1), :].astype(BF16))

    o_ref[...] = _layernorm_rows(ALPHA * x_ref[...] + g1_ref[...] * y, lng_ref[...], lnb_ref[...])


def _out_even(cfg, of, ob, proj, x, mod, layer, norm_g, b_proj, b_scale, w_out, ln_g, ln_b):
    n = SEQ_TILE
    pz_prev, pz_next = _halo_specs(cfg, n, B_WIDTH, col_block=3)
    return pl.pallas_call(
        functools.partial(_out_even_kernel, cfg),
        grid=(cfg.n_tiles,),
        in_specs=[
            pl.BlockSpec((n, A_WIDTH), lambda i: (i, 0)),
            pl.BlockSpec((n, A_WIDTH), lambda i: (i, 0)),
            pl.BlockSpec((n, A_WIDTH), lambda i: (i, 2)),
            pl.BlockSpec((n, B_WIDTH), lambda i: (i, 3)),
            pz_prev, pz_next,
            pl.BlockSpec((n, D_MODEL), lambda i: (i, 0)),
            _mod_spec(cfg, layer, 2, n),
            pl.BlockSpec((1, A_DV), lambda i: (0, 0)),
            pl.BlockSpec((len(POOL_WINDOWS), B_GW, B_GW), lambda i: (0, 0, 0)),
            pl.BlockSpec((1, B_WIDTH), lambda i: (0, 0)),
            pl.BlockSpec((D_MODEL, D_MODEL), lambda i: (0, 0)),
            pl.BlockSpec((1, D_MODEL), lambda i: (0, 0)),
            pl.BlockSpec((1, D_MODEL), lambda i: (0, 0)),
        ],
        out_specs=pl.BlockSpec((n, D_MODEL), lambda i: (i, 0)),
        out_shape=jax.ShapeDtypeStruct((cfg.t_all, D_MODEL), F32),
        scratch_shapes=[pltpu.VMEM((n + 2 * HALO, B_WIDTH), F32)],
        compiler_params=_cparams(("arbitrary",)),
        name=f"out_even_l{layer}",
    )(of, ob, proj, proj, proj, proj, x, mod, norm_g, b_proj, b_scale, w_out, ln_g, ln_b)


def _qkv_conv_kernel(cfg, tm, tn, x_ref, xp_ref, xn_ref, sc_ref, sh_ref, w_ref, cw_ref, o_ref, u_scr, y_scr):
    i, j = pl.program_id(0), pl.program_id(1)

    @pl.when(j == 0)
    def _():
        sc, sh = 1.0 + sc_ref[...], sh_ref[...]
        u_scr[0:HALO, :] = (xp_ref[...] * sc + sh).astype(BF16)
        u_scr[HALO:HALO + tm, :] = (x_ref[...] * sc + sh).astype(BF16)
        u_scr[HALO + tm:, :] = (xn_ref[...] * sc + sh).astype(BF16)

    y_scr[...] = _dot(u_scr[...], w_ref[...].astype(BF16))
    pos, length = _seq_edges(cfg, i, tm)
    cw = cw_ref[...]
    h = (jnp.where(pos == 0, 0.0, y_scr[HALO - 1:HALO - 1 + tm, :] * cw[0:1, :])
         + y_scr[HALO:HALO + tm, :] * cw[1:2, :]
         + jnp.where(pos == length - 1, 0.0, y_scr[HALO + 1:HALO + 1 + tm, :] * cw[2:3, :]))
    h = _silu(h)
    is_q = j * tn < C_WIDTH
    is_qk = j * tn < 2 * C_WIDTH
    for c in range(tn // C_DK):
        hc = h[:, C_DK * c:C_DK * (c + 1)]
        inv = lax.rsqrt(jnp.sum(hc * hc, axis=-1, keepdims=True) + EPS)
        scale = jnp.where(is_qk, inv * jnp.where(is_q, C_DK ** -0.5, 1.0), 1.0)
        o_ref[:, C_DK * c:C_DK * (c + 1)] = hc * scale


def _qkv_conv(cfg, x, mod, layer, w, conv_w):
    tm, tn = cfg.tm, 256
    n = w.shape[1]
    xp, xn = _halo_specs(cfg, tm, D_MODEL)
    return pl.pallas_call(
        functools.partial(_qkv_conv_kernel, cfg, tm, tn),
        grid=(cfg.t_all // tm, n // tn),
        in_specs=[
            pl.BlockSpec((tm, D_MODEL), lambda i, j: (i, 0)),
            xp, xn,
            _mod_spec(cfg, layer, 1, tm),
            _mod_spec(cfg, layer, 0, tm),
            pl.BlockSpec((D_MODEL, tn), lambda i, j: (0, j)),
            pl.BlockSpec((3, tn), lambda i, j: (0, j)),
        ],
        out_specs=pl.BlockSpec((tm, tn), lambda i, j: (i, j)),
        out_shape=jax.ShapeDtypeStruct((cfg.t_all, n), F32),
        scratch_shapes=[pltpu.VMEM((tm + 2 * HALO, D_MODEL), BF16), pltpu.VMEM((tm + 2 * HALO, tn), F32)],
        compiler_params=_cparams(("arbitrary", "arbitrary")),
        name=f"qkv_conv_l{layer}",
    )(x, x, x, mod, mod, w, conv_w)


DN_BLOCK = 64
DN_NBLK = SEQ_TILE // DN_BLOCK


def _tri_ones():
    n = SEQ_TILE
    idx = np.arange(n)
    lower = (idx[None, :] <= idx[:, None]).astype(np.float32)
    return jnp.asarray(np.stack([lower, lower.T]), BF16)


def _unit_tri_inverses(mats):
    n = DN_BLOCK
    row = lax.broadcasted_iota(jnp.int32, (n, n), 0)
    col = lax.broadcasted_iota(jnp.int32, (n, n), 1)
    eye = (row == col).astype(F32)
    ts = [eye - jnp.where((row >> 1) == (col >> 1), a, 0.0) for a in mats]
    for l in range(1, int(math.log2(n))):
        couple = jnp.logical_and((row >> (l + 1)) == (col >> (l + 1)), (row >> l) != (col >> l))
        tbs = [t.astype(BF16) for t in ts]
        ws = [_dot(jnp.where(couple, a, 0.0).astype(BF16), tb) for a, tb in zip(mats, tbs)]
        ts = [t - _dot(tb, w.astype(BF16)) for t, tb, w in zip(ts, tbs, ws)]
    return ts


def _delta_tile(dirs):
    n = SEQ_TILE
    row = lax.broadcasted_iota(jnp.int32, (n, n), 0)
    col = lax.broadcasted_iota(jnp.int32, (n, n), 1)
    probs = []
    for d, (q_ref, k_ref, v_ref, g, beta, tri_ref, s_scr, o_ref) in enumerate(dirs):
        incl = (col <= row) if d == 0 else (col >= row)
        strict = (col < row) if d == 0 else (col > row)
        last = n - 1 if d == 0 else 0
        g_hi, g_lo = _split_bf16(g)
        b_col = _dot(tri_ref[...], g_hi) + _dot(tri_ref[...], g_lo)
        b_row = b_col.T
        for h in range(C_HEADS):
            c = d * C_HEADS + h
            cols = slice(C_DK * h, C_DK * (h + 1))
            q, k, v = q_ref[:, cols], k_ref[:, cols], v_ref[:, cols]
            bc = b_col[:, c:c + 1]
            br = b_row[c:c + 1, :]
            b_last = b_col[last:last + 1, c:c + 1]
            ld = jnp.exp(jnp.where(incl, bc - br, -jnp.inf))
            bt = beta[:, 2 * C_HEADS + c:2 * C_HEADS + c + 1]
            kb = k * bt
            kbf = k.astype(BF16)
            e_col = jnp.exp(bc)
            probs.append(dict(
                d=d, h=h, cols=cols, s_scr=s_scr, o_ref=o_ref,
                a_mat=jnp.where(strict, _dot_nt(kb.astype(BF16), kbf) * ld, 0.0),
                aqk=(_dot_nt(q.astype(BF16), kbf) * ld).astype(BF16),
                rhs=jnp.concatenate([v * bt, kb * e_col], axis=1),
                qd=(q * e_col).astype(BF16),
                kd_t=(k * jnp.exp(b_last - bc)).T.astype(BF16),
                dec=jnp.exp(b_last)))

    blk = lambda b: slice(DN_BLOCK * b, DN_BLOCK * (b + 1))
    diag = [p["a_mat"][blk(b), blk(b)] for p in probs for b in range(DN_NBLK)]
    t_inv = _unit_tri_inverses(diag)

    xs = [[None] * DN_NBLK for _ in probs]
    for step in range(DN_NBLK):
        for pi, p in enumerate(probs):
            b = step if p["d"] == 0 else DN_NBLK - 1 - step
            resid = p["rhs"][blk(b), :]
            for bb in range(DN_NBLK):
                if xs[pi][bb] is not None:
                    resid = resid - _dot(p["a_mat"][blk(b), blk(bb)].astype(BF16), xs[pi][bb].astype(BF16))
            p["resid"] = resid
        for pi, p in enumerate(probs):
            b = step if p["d"] == 0 else DN_NBLK - 1 - step
            xs[pi][b] = _dot(t_inv[pi * DN_NBLK + b].astype(BF16), p["resid"].astype(BF16))

    for pi, p in enumerate(probs):
        x = jnp.concatenate(xs[pi], axis=0)
        p["u"], p["w"] = x[:, :C_DK], x[:, C_DK:].astype(BF16)
        p["s"] = p["s_scr"][p["h"]]
        p["sb"] = p["s"].astype(BF16)
    for p in probs:
        p["vb"] = (p["u"] - _dot(p["w"], p["sb"])).astype(BF16)
    for p in probs:
        p["o_ref"][:, p["cols"]] = _dot(p["qd"], p["sb"]) + _dot(p["aqk"], p["vb"])
    for p in probs:
        p["s_scr"][p["h"]] = p["s"] * p["dec"] + _dot(p["kd_t"], p["vb"])


def _delta_kernel(cfg, qf, kf, vf, abf, qb, kb, vb, abb, s0_ref, alog_ref, dtb_ref, tri_ref,
                  of_ref, ob_ref, st_ref, sf_scr, sb_scr):
    i = pl.program_id(0)

    @pl.when(cfg.first_of_seq(i))
    def _():
        sf_scr[...] = s0_ref[0]
        sb_scr[...] = s0_ref[1]

    lane = lax.broadcasted_iota(jnp.int32, (1, 128), 1)
    neg_a = jnp.where(lane < 2 * C_HEADS, -jnp.exp(alog_ref[...]), 0.0)

    def gate(ab):
        a_in = ab + dtb_ref[...]
        softplus = jnp.maximum(a_in, 0.0) + jnp.log1p(jnp.exp(-jnp.abs(a_in)))
        return neg_a * softplus, jax.nn.sigmoid(ab)

    g_f, beta_f = gate(abf[...])
    g_b, beta_b = gate(abb[...])
    _delta_tile([(qf, kf, vf, g_f, beta_f, tri_ref.at[0], sf_scr, of_ref),
                 (qb, kb, vb, g_b, beta_b, tri_ref.at[1], sb_scr, ob_ref)])

    @pl.when(cfg.last_of_seq(i))
    def _():
        st_ref[0] = sf_scr[...]
        st_ref[1] = sb_scr[...]


def _delta(cfg, qkv, zab, s0, a_log, dt_bias, tri):
    rev = cfg.rev_tile
    n = SEQ_TILE
    st_shape = (cfg.n_seq, 2, C_HEADS, C_DK, C_DK)
    ab_col = C_WIDTH // 128
    in_specs = []
    for tile_of in (lambda i: i, rev):
        for cb in range(3):
            in_specs.append(pl.BlockSpec((n, C_WIDTH), functools.partial(lambda i, f, c: (f(i), c), f=tile_of, c=cb)))
        in_specs.append(pl.BlockSpec((n, 128), functools.partial(lambda i, f: (f(i), ab_col), f=tile_of)))
    in_specs += [
        pl.BlockSpec((None,) + st_shape[1:], lambda i: (cfg.seq_of_tile(i), 0, 0, 0, 0)),
        pl.BlockSpec((1, 128), lambda i: (0, 0)),
        pl.BlockSpec((1, 128), lambda i: (0, 0)),
        pl.BlockSpec(tri.shape, lambda i: (0, 0, 0)),
    ]
    return pl.pallas_call(
        functools.partial(_delta_kernel, cfg),
        grid=(cfg.n_tiles,),
        in_specs=in_specs,
        out_specs=[
            pl.BlockSpec((n, C_WIDTH), lambda i: (i, 0)),
            pl.BlockSpec((n, C_WIDTH), lambda i: (rev(i), 0)),
            pl.BlockSpec((None,) + st_shape[1:], lambda i: (cfg.seq_of_tile(i), 0, 0, 0, 0)),
        ],
        out_shape=[
            jax.ShapeDtypeStruct((cfg.t_all, C_WIDTH), F32),
            jax.ShapeDtypeStruct((cfg.t_all, C_WIDTH), F32),
            jax.ShapeDtypeStruct(st_shape, F32),
        ],
        scratch_shapes=[pltpu.VMEM(st_shape[2:], F32), pltpu.VMEM(st_shape[2:], F32)],
        compiler_params=_cparams(("arbitrary",)),
        name="delta_scan",
    )(qkv, qkv, qkv, zab, qkv, qkv, qkv, zab, s0, a_log, dt_bias, tri)


def _out_odd_kernel(of_ref, ob_ref, z_ref, x_ref, g1_ref, norm_ref, wout_ref, lng_ref, lnb_ref, o_ref):
    n = SEQ_TILE
    o = of_ref[...] + ob_ref[...]
    z = z_ref[...]
    y = jnp.zeros((n, D_MODEL), F32)
    heads = _rms_heads(o, norm_ref[...], C_HEADS, C_DK)
    for h in range(C_HEADS):
        a = (heads[h] * _silu(z[:, C_DK * h:C_DK * (h + 1)])).astype(BF16)
        y = y + _dot(a, wout_ref[C_DK * h:C_DK * (h + 1), :].astype(BF16))
    o_ref[...] = _layernorm_rows(ALPHA * x_ref[...] + g1_ref[...] * y, lng_ref[...], lnb_ref[...])


def _out_odd(cfg, of, ob, zab, x, mod, layer, norm_g, w_out, ln_g, ln_b):
    n = SEQ_TILE
    return pl.pallas_call(
        _out_odd_kernel,
        grid=(cfg.n_tiles,),
        in_specs=[
            pl.BlockSpec((n, C_WIDTH), lambda i: (i, 0)),
            pl.BlockSpec((n, C_WIDTH), lambda i: (i, 0)),
            pl.BlockSpec((n, C_WIDTH), lambda i: (i, 0)),
            pl.BlockSpec((n, D_MODEL), lambda i: (i, 0)),
            _mod_spec(cfg, layer, 2, n),
            pl.BlockSpec((1, C_DK), lambda i: (0, 0)),
            pl.BlockSpec((D_MODEL, D_MODEL), lambda i: (0, 0)),
            pl.BlockSpec((1, D_MODEL), lambda i: (0, 0)),
            pl.BlockSpec((1, D_MODEL), lambda i: (0, 0)),
        ],
        out_specs=pl.BlockSpec((n, D_MODEL), lambda i: (i, 0)),
        out_shape=jax.ShapeDtypeStruct((cfg.t_all, D_MODEL), F32),
        compiler_params=_cparams(("arbitrary",)),
        name=f"out_odd_l{layer}",
    )(of, ob, zab, x, mod, norm_g, w_out, ln_g, ln_b)


FFN_TF = 256


def _ffn_kernel(cfg, tm, x_ref, xp_ref, xn_ref, sc_ref, sh_ref, g2_ref, wa_ref, wg_ref, ca_ref, cg_ref,
                wd_ref, lng_ref, lnb_ref, o_ref, u_scr, ya_scr, yg_scr, acc_scr):
    i, j = pl.program_id(0), pl.program_id(1)

    @pl.when(j == 0)
    def _():
        sc, sh = 1.0 + sc_ref[...], sh_ref[...]
        u_scr[0:HALO, :] = (xp_ref[...] * sc + sh).astype(BF16)
        u_scr[HALO:HALO + tm, :] = (x_ref[...] * sc + sh).astype(BF16)
        u_scr[HALO + tm:, :] = (xn_ref[...] * sc + sh).astype(BF16)
        acc_scr[...] = jnp.zeros_like(acc_scr)

    u = u_scr[...]
    ya_scr[...] = _dot(u, wa_ref[...].astype(BF16))
    yg_scr[...] = _dot(u, wg_ref[...].astype(BF16))
    pos, length = _seq_edges(cfg, i, tm)
    first, last = pos == 0, pos == length - 1

    def conv(y_scr, cw):
        return (jnp.where(first, 0.0, y_scr[HALO - 1:HALO - 1 + tm, :] * cw[0:1, :])
                + y_scr[HALO:HALO + tm, :] * cw[1:2, :]
                + jnp.where(last, 0.0, y_scr[HALO + 1:HALO + 1 + tm, :] * cw[2:3, :]))

    act = (_silu(conv(yg_scr, cg_ref[...])) * conv(ya_scr, ca_ref[...])).astype(BF16)
    acc_scr[...] += _dot(act, wd_ref[...].astype(BF16))

    @pl.when(j == pl.num_programs(1) - 1)
    def _():
        o_ref[...] = _layernorm_rows(ALPHA * x_ref[...] + g2_ref[...] * acc_scr[...], lng_ref[...], lnb_ref[...])


def _ffn(cfg, x, mod, layer, w_up, conv_w, w_down, ln_g, ln_b):
    tm, tf = cfg.tm, FFN_TF
    nj = D_FF // tf
    xp, xn = _halo_specs(cfg, tm, D_MODEL)
    return pl.pallas_call(
        functools.partial(_ffn_kernel, cfg, tm),
        grid=(cfg.t_all // tm, nj),
        in_specs=[
            pl.BlockSpec((tm, D_MODEL), lambda i, j: (i, 0)),
            xp, xn,
            _mod_spec(cfg, layer, 4, tm),
            _mod_spec(cfg, layer, 3, tm),
            _mod_spec(cfg, layer, 5, tm),
            pl.BlockSpec((D_MODEL, tf), lambda i, j: (0, j)),
            pl.BlockSpec((D_MODEL, tf), lambda i, j: (0, nj + j)),
            pl.BlockSpec((3, tf), lambda i, j: (0, j)),
            pl.BlockSpec((3, tf), lambda i, j: (0, nj + j)),
            pl.BlockSpec((tf, D_MODEL), lambda i, j: (j, 0)),
            pl.BlockSpec((1, D_MODEL), lambda i, j: (0, 0)),
            pl.BlockSpec((1, D_MODEL), lambda i, j: (0, 0)),
        ],
        out_specs=pl.BlockSpec((tm, D_MODEL), lambda i, j: (i, 0)),
        out_shape=jax.ShapeDtypeStruct((cfg.t_all, D_MODEL), F32),
        scratch_shapes=[
            pltpu.VMEM((tm + 2 * HALO, D_MODEL), BF16),
            pltpu.VMEM((tm + 2 * HALO, tf), F32),
            pltpu.VMEM((tm + 2 * HALO, tf), F32),
            pltpu.VMEM((tm, D_MODEL), F32),
        ],
        compiler_params=_cparams(("arbitrary", "arbitrary")),
        name=f"ffn_l{layer}",
    )(x, x, x, mod, mod, mod, w_up, w_up, conv_w, conv_w, w_down, ln_g, ln_b)


def _grid_pos_embed(n_tokens):
    rows = n_tokens // GRID_W
    r = jnp.broadcast_to(jnp.arange(rows, dtype=F32)[:, None], (rows, GRID_W)).reshape(-1)
    col = jnp.broadcast_to(jnp.arange(GRID_W, dtype=F32)[None, :], (rows, GRID_W)).reshape(-1)
    quarter = D_MODEL // 4
    freq = jnp.exp(-math.log(10000.0) * jnp.arange(quarter, dtype=F32) / quarter)
    ra, ca = r[:, None] * freq, col[:, None] * freq
    return jnp.concatenate([jnp.sin(ra), jnp.cos(ra), jnp.sin(ca), jnp.cos(ca)], -1)


def kernel(x_prompt, x_sample, state_gla, state_dn, c, c_ctx, w_mod, b_mod, ln1_g, ln1_b, ln2_g, ln2_b,
           a_w_in, a_w_gate, a_b_gate, a_norm, b_proj, b_scale, a_w_out,
           c_w_in, c_conv, c_a_log, c_dt_bias, c_norm, c_w_out, f_w_up, f_conv, f_w_down):
    n_ctx, seq, _ = x_prompt.shape
    n_smp, smp_len, _ = x_sample.shape
    assert seq == SEQ_TILE and n_smp + 1 <= MOD_ROWS
    cfg = _Cfg(n_ctx, n_smp, smp_len)

    x = _embed(cfg, x_prompt.reshape(-1, D_MODEL), x_sample.reshape(-1, D_MODEL), _grid_pos_embed(smp_len))
    cond = jnp.concatenate([c_ctx[None, :], c, jnp.zeros((MOD_ROWS - 1 - n_smp, D_MODEL), F32)], axis=0)
    mod = _modulation(cond, w_mod, b_mod)

    gla_consts = _gla_constants()
    tri = _tri_ones()
    gla_states, dn_states = [], []
    for layer in range(DEPTH):
        ln1g, ln1b = ln1_g[layer][None, :], ln1_b[layer][None, :]
        if layer % 2 == 0:
            e = layer // 2
            w_in = a_w_in[e]
            w_main = jnp.concatenate([w_in[:, :2 * A_QK + 2 * A_WIDTH], w_in[:, -B_WIDTH:]], axis=1)
            lr0 = 2 * A_QK + 2 * A_WIDTH
            w_lr = jnp.pad(w_in[:, lr0:lr0 + 2 * A_RANK], ((0, 0), (0, 128 - 2 * A_RANK)))
            w_gate_bd = jnp.zeros((128, 2 * A_QK), F32)
            w_gate_bd = w_gate_bd.at[:A_RANK, :A_QK].set(a_w_gate[e, 0]).at[A_RANK:2 * A_RANK, A_QK:].set(a_w_gate[e, 1])
            proj = _proj(cfg, x, mod, layer, w_main, 512)
            gates = _gates(cfg, x, mod, layer, w_lr, w_gate_bd, a_b_gate[e].reshape(1, 2 * A_QK))
            s0 = jnp.concatenate([jnp.zeros((n_ctx, 2, A_HEADS, A_DK, A_DV), F32), state_gla[:, e].astype(F32)], axis=0)
            s0 = s0.reshape(cfg.n_seq, 2, A_HEADS // 2, 128, A_DV)
            of, ob, st = _gla(cfg, proj, gates, s0, gla_consts)
            gla_states.append(st[:n_ctx].reshape(n_ctx, 2, A_HEADS, A_DK, A_DV))
            x = _out_even(cfg, of, ob, proj, x, mod, layer, a_norm[e][None, :], b_proj[e], b_scale[e][None, :],
                          a_w_out[e], ln1g, ln1b)
        else:
            o_ = layer // 2
            w_in = c_w_in[o_]
            w_zab = jnp.pad(w_in[:, 3 * C_WIDTH:], ((0, 0), (0, 128 - 4 * C_HEADS)))
            qkv = _qkv_conv(cfg, x, mod, layer, w_in[:, :3 * C_WIDTH], c_conv[o_])
            zab = _proj(cfg, x, mod, layer, w_zab, 128)
            s0 = jnp.concatenate([jnp.zeros((n_ctx, 2, C_HEADS, C_DK, C_DK), F32), state_dn[:, o_].astype(F32)], axis=0)
            pad16 = lambda t: jnp.pad(t.reshape(1, -1), ((0, 0), (0, 128 - 2 * C_HEADS)))
            of, ob, st = _delta(cfg, qkv, zab, s0, pad16(c_a_log[o_]), pad16(c_dt_bias[o_]), tri)
            dn_states.append(st[:n_ctx])
            x = _out_odd(cfg, of, ob, zab, x, mod, layer, c_norm[o_][None, :], c_w_out[o_], ln1g, ln1b)
        x = _ffn(cfg, x, mod, layer, f_w_up[layer], f_conv[layer], f_w_down[layer],
                 ln2_g[layer][None, :], ln2_b[layer][None, :])

    y_prompt = x[:cfg.t_ctx].reshape(n_ctx, seq, D_MODEL)
    y_sample = x[cfg.t_ctx:].reshape(n_smp, smp_len, D_MODEL)
    return (y_prompt, y_sample, jnp.stack(gla_states, axis=1).astype(x_prompt.dtype),
            jnp.stack(dn_states, axis=1).astype(x_prompt.dtype))
```

```python
import functools
import math

import numpy as np
import jax
import jax.numpy as jnp
from jax import lax
from jax.experimental import pallas as pl
from jax.experimental.pallas import tpu as pltpu

F32 = jnp.float32
BF16 = jnp.bfloat16

D_MODEL = 1024
DEPTH = 4
GRID_W = 64
A_HEADS = 4
A_DK = 64
A_DV = 128
A_QK = A_HEADS * A_DK
A_WIDTH = A_HEADS * A_DV
A_RANK = 16
A_GATE_NORM = 16.0
B_WIDTH = 512
POOL_WINDOWS = (2, 4, 8, 16)
B_GW = 128
C_HEADS = 8
C_DK = 128
C_WIDTH = 1024
D_FF = 2816
N_MOD = 6
ALPHA = (2 * DEPTH) ** 0.25
EPS = 1e-6

SEQ_TILE = 256
HALO = 8
ROW_CHUNK = 256
MOD_ROWS = 8
V7X_VMEM_LIMIT = 56 * 1024 * 1024


def _cparams(sem):
    return pltpu.CompilerParams(dimension_semantics=sem, vmem_limit_bytes=V7X_VMEM_LIMIT)


class _Cfg:
    def __init__(self, n_ctx, n_smp, smp_len):
        assert smp_len % SEQ_TILE == 0 and smp_len & (smp_len - 1) == 0
        self.n_ctx, self.n_smp, self.smp_len = n_ctx, n_smp, smp_len
        self.t_ctx = n_ctx * SEQ_TILE
        self.t_all = self.t_ctx + n_smp * smp_len
        self.tps = smp_len // SEQ_TILE
        self.n_tiles = self.t_all // SEQ_TILE
        self.n_seq = n_ctx + n_smp
        tm = 1024
        while self.t_ctx % tm or smp_len % tm:
            tm //= 2
        self.tm = tm

    def cond_of_row(self, g):
        return jnp.where(g < self.t_ctx, 0, 1 + (g - self.t_ctx) // self.smp_len)

    def seq_of_tile(self, i):
        return jnp.where(i < self.n_ctx, i, self.n_ctx + (i - self.n_ctx) // self.tps)

    def rev_tile(self, i):
        m = (i - self.n_ctx) % self.tps
        return jnp.where(i < self.n_ctx, i, i - m + (self.tps - 1 - m))

    def chunk_edges(self, row0):
        is_ctx = row0 < self.t_ctx
        length = jnp.where(is_ctx, SEQ_TILE, self.smp_len)
        pos0 = jnp.bitwise_and(jnp.where(is_ctx, row0, row0 - self.t_ctx), length - 1)
        return pos0 == 0, pos0 + ROW_CHUNK == length

    def first_of_seq(self, i):
        return jnp.logical_or(i < self.n_ctx, (i - self.n_ctx) % self.tps == 0)

    def last_of_seq(self, i):
        return jnp.logical_or(i < self.n_ctx, (i - self.n_ctx) % self.tps == self.tps - 1)


def _mod_map(cfg, layer, which, rows):
    def index_map(i, *_):
        return ((layer * MOD_ROWS + cfg.cond_of_row(i * rows)) * N_MOD + which, 0, 0)
    return index_map


def _mod_spec(cfg, layer, which, rows):
    return pl.BlockSpec((None, 1, D_MODEL), _mod_map(cfg, layer, which, rows))


def _halo_specs(cfg, rows, width, col_block=0):
    per = rows // HALO
    last = cfg.t_all // HALO - 1
    prev = pl.BlockSpec((HALO, width), lambda i, *_: (jnp.maximum(i * per - 1, 0), col_block))
    nxt = pl.BlockSpec((HALO, width), lambda i, *_: (jnp.minimum((i + 1) * per, last), col_block))
    return prev, nxt


def _seq_edges(cfg, tile, rows):
    g = tile * rows + lax.broadcasted_iota(jnp.int32, (rows, 1), 0)
    is_ctx = g < cfg.t_ctx
    length = jnp.where(is_ctx, SEQ_TILE, cfg.smp_len)
    pos = jnp.bitwise_and(jnp.where(is_ctx, g, g - cfg.t_ctx), length - 1)
    return pos, length


def _silu(x):
    return x * jax.nn.sigmoid(x)


def _dot(a, b):
    return jnp.dot(a, b, preferred_element_type=F32)


def _dot_nt(a, b):
    return lax.dot_general(a, b, (((1,), (1,)), ((), ())), preferred_element_type=F32)


def _split_bf16(x):
    hi = x.astype(BF16)
    lo = (x - hi.astype(F32)).astype(BF16)
    return hi, lo


def _conv3(y_ref, cw, starts_seq, ends_seq):
    n = ROW_CHUNK
    y_ref[HALO - 1:HALO, :] = jnp.where(starts_seq, 0.0, y_ref[HALO - 1:HALO, :])
    y_ref[HALO + n:HALO + n + 1, :] = jnp.where(ends_seq, 0.0, y_ref[HALO + n:HALO + n + 1, :])
    return (y_ref[HALO - 1:HALO - 1 + n, :] * cw[0:1, :]
            + y_ref[HALO:HALO + n, :] * cw[1:2, :]
            + y_ref[HALO + 1:HALO + 1 + n, :] * cw[2:3, :])


def _layernorm_rows(x, g, b):
    mu = jnp.mean(x, axis=-1, keepdims=True)
    xc = x - mu
    var = jnp.mean(xc * xc, axis=-1, keepdims=True)
    return xc * lax.rsqrt(var + EPS) * g + b


def _state_specs(cfg, slot, per_seq):
    zeros = (0,) * len(per_seq)
    n_ctx = cfg.n_ctx
    in_spec = pl.BlockSpec((None, None) + per_seq,
                           lambda i: (jnp.maximum(cfg.seq_of_tile(i) - n_ctx, 0), slot) + zeros)
    out_spec = pl.BlockSpec((None,) + per_seq, lambda i: (jnp.minimum(cfg.seq_of_tile(i), n_ctx - 1),) + zeros)
    return in_spec, out_spec


def _load_states(cfg, i, s0_ref, sf_scr, sb_scr):
    @pl.when(cfg.first_of_seq(i))
    def _():
        is_ctx = i < cfg.n_ctx
        sf_scr[...] = jnp.where(is_ctx, 0.0, s0_ref[0])
        sb_scr[...] = jnp.where(is_ctx, 0.0, s0_ref[1])


def _store_states(cfg, i, st_ref, sf_scr, sb_scr):
    @pl.when(i < cfg.n_ctx)
    def _():
        st_ref[0] = sf_scr[...]
        st_ref[1] = sb_scr[...]


def _embed_kernel(cfg, xp_ref, xs_ref, pe_ref, o_ref):
    i = pl.program_id(0)

    @pl.when(i < cfg.n_ctx)
    def _():
        o_ref[...] = xp_ref[...]

    @pl.when(i >= cfg.n_ctx)
    def _():
        o_ref[...] = xs_ref[...] + pe_ref[...]


def _embed(cfg, xp, xs, pe):
    n_ctx, tps = cfg.n_ctx, cfg.tps
    return pl.pallas_call(
        functools.partial(_embed_kernel, cfg),
        grid=(cfg.n_tiles,),
        in_specs=[
            pl.BlockSpec((SEQ_TILE, D_MODEL), lambda i: (jnp.minimum(i, n_ctx - 1), 0)),
            pl.BlockSpec((SEQ_TILE, D_MODEL), lambda i: (jnp.maximum(i - n_ctx, 0), 0)),
            pl.BlockSpec((SEQ_TILE, D_MODEL), lambda i: (jnp.maximum(i - n_ctx, 0) % tps, 0)),
        ],
        out_specs=pl.BlockSpec((SEQ_TILE, D_MODEL), lambda i: (i, 0)),
        out_shape=jax.ShapeDtypeStruct((cfg.t_all, D_MODEL), F32),
        compiler_params=_cparams(("arbitrary",)),
        name="embed",
    )(xp, xs, pe)


def _mod_kernel(c_ref, w_ref, b_ref, o_ref):
    sc = _silu(c_ref[...]).astype(BF16)
    o_ref[...] = _dot(sc, w_ref[...].astype(BF16)) + b_ref[...]


def _modulation(cond, w_mod, b_mod):
    out = pl.pallas_call(
        _mod_kernel,
        grid=(DEPTH, N_MOD),
        in_specs=[
            pl.BlockSpec((MOD_ROWS, D_MODEL), lambda l, j: (0, 0)),
            pl.BlockSpec((None, D_MODEL, D_MODEL), lambda l, j: (l, 0, j)),
            pl.BlockSpec((None, 1, D_MODEL), lambda l, j: (l, 0, j)),
        ],
        out_specs=pl.BlockSpec((None, MOD_ROWS, D_MODEL), lambda l, j: (l, 0, j)),
        out_shape=jax.ShapeDtypeStruct((DEPTH, MOD_ROWS, N_MOD * D_MODEL), F32),
        compiler_params=_cparams(("arbitrary", "arbitrary")),
        name="modulation",
    )(cond, w_mod, b_mod.reshape(DEPTH, 1, N_MOD * D_MODEL))
    return out.reshape(DEPTH * MOD_ROWS * N_MOD, 1, D_MODEL)


def _proj_kernel(x_ref, sc_ref, sh_ref, w_ref, o_ref, u_scr):
    @pl.when(pl.program_id(1) == 0)
    def _():
        u_scr[...] = (x_ref[...] * (1.0 + sc_ref[...]) + sh_ref[...]).astype(BF16)

    o_ref[...] = _dot(u_scr[...], w_ref[...].astype(BF16))


def _proj(cfg, x, mod, layer, w, tn):
    n = w.shape[1]
    tm = cfg.tm
    return pl.pallas_call(
        _proj_kernel,
        grid=(cfg.t_all // tm, n // tn),
        in_specs=[
            pl.BlockSpec((tm, D_MODEL), lambda i, j: (i, 0)),
            _mod_spec(cfg, layer, 1, tm),
            _mod_spec(cfg, layer, 0, tm),
            pl.BlockSpec((D_MODEL, tn), lambda i, j: (0, j)),
        ],
        out_specs=pl.BlockSpec((tm, tn), lambda i, j: (i, j)),
        out_shape=jax.ShapeDtypeStruct((cfg.t_all, n), F32),
        scratch_shapes=[pltpu.VMEM((tm, D_MODEL), BF16)],
        compiler_params=_cparams(("arbitrary", "arbitrary")),
        name=f"proj_l{layer}",
    )(x, mod, mod, w)


def _gate_kernel(x_ref, sc_ref, sh_ref, wlr_ref, wg_ref, bg_ref, o_ref):
    u = (x_ref[...] * (1.0 + sc_ref[...]) + sh_ref[...]).astype(BF16)
    lr = _dot(u, wlr_ref[...].astype(BF16))
    z = _dot(lr.astype(BF16), wg_ref[...].astype(BF16)) + bg_ref[...]
    log_sig = jnp.minimum(z, 0.0) - jnp.log1p(jnp.exp(-jnp.abs(z)))
    o_ref[...] = log_sig / A_GATE_NORM


def _gates(cfg, x, mod, layer, w_lr, w_gate_bd, b_gate):
    tm = cfg.tm
    return pl.pallas_call(
        _gate_kernel,
        grid=(cfg.t_all // tm,),
        in_specs=[
            pl.BlockSpec((tm, D_MODEL), lambda i: (i, 0)),
            _mod_spec(cfg, layer, 1, tm),
            _mod_spec(cfg, layer, 0, tm),
            pl.BlockSpec((D_MODEL, 128), lambda i: (0, 0)),
            pl.BlockSpec((128, 2 * A_QK), lambda i: (0, 0)),
            pl.BlockSpec((1, 2 * A_QK), lambda i: (0, 0)),
        ],
        out_specs=pl.BlockSpec((tm, 2 * A_QK), lambda i: (i, 0)),
        out_shape=jax.ShapeDtypeStruct((cfg.t_all, 2 * A_QK), F32),
        compiler_params=_cparams(("arbitrary",)),
        name=f"gates_l{layer}",
    )(x, mod, mod, w_lr, w_gate_bd, b_gate)


GLA_LEVELS = 8
GLA_STACK = GLA_LEVELS + 2


def _gla_constants():
    n = SEQ_TILE
    idx = np.arange(n)
    i, m = idx[:, None], idx[None, :]
    stack = np.zeros((2, GLA_STACK, n, n), np.float32)
    masks = np.zeros((2, GLA_LEVELS + 1, n, n), np.float32)
    stack[0, 0] = m <= i
    stack[0, 1] = m > i
    stack[1, 0] = m >= i
    stack[1, 1] = m < i
    for l in range(GLA_LEVELS):
        s = 1 << l
        blk = idx // (2 * s)
        upper = (idx // s) % 2 == 1
        piv_f = blk * 2 * s + s - 1
        piv_b = blk * 2 * s + s
        up, pf, pb = upper[:, None], piv_f[:, None], piv_b[:, None]
        stack[0, 2 + l] = np.where(up, (m > pf) & (m <= i), (m > i) & (m <= pf))
        stack[1, 2 + l] = np.where(up, (m >= pb) & (m < i), (m >= i) & (m < pb))
        same = blk[:, None] == blk[None, :]
        masks[0, l] = same & upper[:, None] & ~upper[None, :]
        masks[1, l] = same & ~upper[:, None] & upper[None, :]
    masks[:, GLA_LEVELS] = np.eye(n)
    return (jnp.asarray(stack.reshape(2, GLA_STACK * n, n), BF16), jnp.asarray(masks, F32))


def _gla_direction(qk, v, g, stack_ref, mask_ref, s_scr, o_ref):
    n = SEQ_TILE
    q = qk[:, :A_QK] * (A_DK ** -0.5)
    k = qk[:, A_QK:]
    g_hi, g_lo = _split_bf16(g)
    sums = _dot(stack_ref[...], g_hi) + _dot(stack_ref[...], g_lo)
    e_in = jnp.exp(sums[0:n])
    e_out = jnp.exp(sums[n:2 * n])
    lane = lax.broadcasted_iota(jnp.int32, (n, 128), 1)
    low_half = lane < A_DK

    att = [jnp.zeros((n, n), F32) for _ in range(A_HEADS)]
    for l in range(GLA_LEVELS + 1):
        if l < GLA_LEVELS:
            e = jnp.exp(sums[(2 + l) * n:(3 + l) * n])
            qs, ks = q * e, k * e
        else:
            qs, ks = q, k
        m = mask_ref[l]
        for p in range(A_HEADS // 2):
            qp = qs[:, 128 * p:128 * (p + 1)]
            kp = ks[:, 128 * p:128 * (p + 1)].astype(BF16)
            for hh in range(2):
                qh = jnp.where(low_half if hh == 0 else ~low_half, qp, 0.0).astype(BF16)
                att[2 * p + hh] = att[2 * p + hh] + m * _dot_nt(qh, kp)

    qd = q * e_in
    kd = k * e_out
    kd_t = kd.T.astype(BF16)
    g_t = g.T
    gt_hi, gt_lo = _split_bf16(g_t)
    ones = jnp.ones((n, 128), BF16)
    dec = jnp.exp(_dot(gt_hi, ones) + _dot(gt_lo, ones))
    for h in range(A_HEADS):
        p, hh = h // 2, h % 2
        v_h = v[:, A_DV * h:A_DV * (h + 1)].astype(BF16)
        s_pair = s_scr[p]
        qp = qd[:, 128 * p:128 * (p + 1)]
        qh = jnp.where(low_half if hh == 0 else ~low_half, qp, 0.0).astype(BF16)
        o_h = _dot(att[h].astype(BF16), v_h) + _dot(qh, s_pair.astype(BF16))
        o_ref[:, A_DV * h:A_DV * (h + 1)] = o_h
    for h in range(A_HEADS):
        p, hh = h // 2, h % 2
        v_h = v[:, A_DV * h:A_DV * (h + 1)].astype(BF16)
        rows = slice(A_DK * hh, A_DK * (hh + 1))
        s_old = s_scr[p, rows, :]
        s_scr[p, rows, :] = s_old * dec[A_DK * h:A_DK * (h + 1), :] + _dot(kd_t[A_DK * h:A_DK * (h + 1), :], v_h)


def _gla_kernel(cfg, qk_f, v_f, g_f, qk_b, v_b, g_b, s0_ref, stack_ref, mask_ref,
                of_ref, ob_ref, st_ref, sf_scr, sb_scr):
    i = pl.program_id(0)

    _load_states(cfg, i, s0_ref, sf_scr, sb_scr)

    _gla_direction(qk_f[...], v_f[...], g_f[...], stack_ref.at[0], mask_ref.at[0], sf_scr, of_ref)
    _gla_direction(qk_b[...], v_b[...], g_b[...], stack_ref.at[1], mask_ref.at[1], sb_scr, ob_ref)

    _store_states(cfg, i, st_ref, sf_scr, sb_scr)


def _gla(cfg, proj, gates, s0, slot, consts):
    stack, masks = consts
    rev = cfg.rev_tile
    n = SEQ_TILE
    st_shape = (cfg.n_ctx, 2, A_HEADS // 2, 128, A_DV)
    s0_spec, st_spec = _state_specs(cfg, slot, st_shape[1:])
    return pl.pallas_call(
        functools.partial(_gla_kernel, cfg),
        grid=(cfg.n_tiles,),
        in_specs=[
            pl.BlockSpec((n, 2 * A_QK), lambda i: (i, 0)),
            pl.BlockSpec((n, A_WIDTH), lambda i: (i, 1)),
            pl.BlockSpec((n, A_QK), lambda i: (i, 0)),
            pl.BlockSpec((n, 2 * A_QK), lambda i: (rev(i), 0)),
            pl.BlockSpec((n, A_WIDTH), lambda i: (rev(i), 1)),
            pl.BlockSpec((n, A_QK), lambda i: (rev(i), 1)),
            s0_spec,
            pl.BlockSpec(stack.shape, lambda i: (0, 0, 0)),
            pl.BlockSpec(masks.shape, lambda i: (0, 0, 0, 0)),
        ],
        out_specs=[
            pl.BlockSpec((n, A_WIDTH), lambda i: (i, 0)),
            pl.BlockSpec((n, A_WIDTH), lambda i: (rev(i), 0)),
            st_spec,
        ],
        out_shape=[
            jax.ShapeDtypeStruct((cfg.t_all, A_WIDTH), F32),
            jax.ShapeDtypeStruct((cfg.t_all, A_WIDTH), F32),
            jax.ShapeDtypeStruct(st_shape, F32),
        ],
        scratch_shapes=[pltpu.VMEM(st_shape[2:], F32), pltpu.VMEM(st_shape[2:], F32)],
        compiler_params=_cparams(("arbitrary",)),
        name="gla_scan",
    )(proj, proj, gates, proj, proj, gates, s0, stack, masks)


def _rms_heads(o, gain, n_heads, width):
    parts = []
    for h in range(n_heads):
        oh = o[:, width * h:width * (h + 1)]
        parts.append(oh * lax.rsqrt(jnp.mean(oh * oh, axis=-1, keepdims=True) + EPS) * gain)
    return parts


def _out_even_kernel(cfg, of_ref, ob_ref, r_ref, pz_ref, pzp_ref, pzn_ref, x_ref, g1_ref,
                     norm_ref, bproj_ref, bscale_ref, wout_ref, lng_ref, lnb_ref, o_ref, ext_scr):
    n = SEQ_TILE
    i = pl.program_id(0)
    o = of_ref[...] + ob_ref[...]
    r = r_ref[...]
    y = jnp.zeros((n, D_MODEL), F32)
    heads = _rms_heads(o, norm_ref[...], A_HEADS, A_DV)
    for h in range(A_HEADS):
        a = (heads[h] * _silu(r[:, A_DV * h:A_DV * (h + 1)])).astype(BF16)
        y = y + _dot(a, wout_ref[A_DV * h:A_DV * (h + 1), :].astype(BF16))

    pos, length = _seq_edges(cfg, i, n)
    ext_scr[0:HALO, :] = pzp_ref[...]
    ext_scr[HALO:HALO + n, :] = pz_ref[...]
    ext_scr[HALO + n:, :] = pzn_ref[...]
    for gi, win in enumerate(POOL_WINDOWS):
        lo = win // 2
        hi = win - 1 - lo
        cols = slice(B_GW * gi, B_GW * (gi + 1))
        acc = jnp.zeros((n, B_GW), F32)
        for d in range(-lo, hi + 1):
            valid = jnp.logical_and(pos + d >= 0, pos + d < length)
            acc = acc + jnp.where(valid, ext_scr[HALO + d:HALO + d + n, cols], 0.0)
        cnt = (jnp.minimum(pos + hi + 1, length) - jnp.maximum(pos - lo, 0)).astype(F32)
        pooled = (acc / cnt - ext_scr[HALO:HALO + n, cols]).astype(BF16)
        mixed = _dot(pooled, bproj_ref[gi].astype(BF16)) * bscale_ref[:, cols]
        y = y + _dot(mixed.astype(BF16), wout_ref[A_WIDTH + B_GW * gi:A_WIDTH + B_GW * (gi + 1), :].astype(BF16))

    o_ref[...] = _layernorm_rows(ALPHA * x_ref[...] + g1_ref[...] * y, lng_ref[...], lnb_ref[...])


def _out_even(cfg, of, ob, proj, x, mod, layer, norm_g, b_proj, b_scale, w_out, ln_g, ln_b):
    n, e = SEQ_TILE, layer // 2
    pz_prev, pz_next = _halo_specs(cfg, n, B_WIDTH, col_block=3)
    return pl.pallas_call(
        functools.partial(_out_even_kernel, cfg),
        grid=(cfg.n_tiles,),
        in_specs=[
            pl.BlockSpec((n, A_WIDTH), lambda i: (i, 0)),
            pl.BlockSpec((n, A_WIDTH), lambda i: (i, 0)),
            pl.BlockSpec((n, A_WIDTH), lambda i: (i, 2)),
            pl.BlockSpec((n, B_WIDTH), lambda i: (i, 3)),
            pz_prev, pz_next,
            pl.BlockSpec((n, D_MODEL), lambda i: (i, 0)),
            _mod_spec(cfg, layer, 2, n),
            pl.BlockSpec((None, 1, A_DV), lambda i: (e, 0, 0)),
            pl.BlockSpec((None, len(POOL_WINDOWS), B_GW, B_GW), lambda i: (e, 0, 0, 0)),
            pl.BlockSpec((None, 1, B_WIDTH), lambda i: (e, 0, 0)),
            pl.BlockSpec((None, D_MODEL, D_MODEL), lambda i: (e, 0, 0)),
            pl.BlockSpec((None, 1, D_MODEL), lambda i: (layer, 0, 0)),
            pl.BlockSpec((None, 1, D_MODEL), lambda i: (layer, 0, 0)),
        ],
        out_specs=pl.BlockSpec((n, D_MODEL), lambda i: (i, 0)),
        out_shape=jax.ShapeDtypeStruct((cfg.t_all, D_MODEL), F32),
        scratch_shapes=[pltpu.VMEM((n + 2 * HALO, B_WIDTH), F32)],
        compiler_params=_cparams(("arbitrary",)),
        name=f"out_even_l{layer}",
    )(of, ob, proj, proj, proj, proj, x, mod, norm_g, b_proj, b_scale, w_out, ln_g, ln_b)


def _qkv_conv_kernel(cfg, tm, tn, x_ref, xp_ref, xn_ref, sc_ref, sh_ref, w_ref, cw_ref, o_ref, u_scr, y_scr):
    i, j = pl.program_id(0), pl.program_id(1)

    @pl.when(j == 0)
    def _():
        sc, sh = 1.0 + sc_ref[...], sh_ref[...]
        u_scr[0:HALO, :] = (xp_ref[...] * sc + sh).astype(BF16)
        u_scr[HALO:HALO + tm, :] = (x_ref[...] * sc + sh).astype(BF16)
        u_scr[HALO + tm:, :] = (xn_ref[...] * sc + sh).astype(BF16)

    w = w_ref[...].astype(BF16)
    cw = cw_ref[...]
    is_q = j * tn < C_WIDTH
    is_qk = j * tn < 2 * C_WIDTH
    for r in range(tm // ROW_CHUNK):
        rows = slice(ROW_CHUNK * r, ROW_CHUNK * (r + 1))
        y = y_scr.at[r]
        y[...] = _dot(u_scr[ROW_CHUNK * r:ROW_CHUNK * (r + 1) + 2 * HALO, :], w)
        h = _silu(_conv3(y, cw, *cfg.chunk_edges(i * tm + ROW_CHUNK * r)))
        for c in range(tn // C_DK):
            hc = h[:, C_DK * c:C_DK * (c + 1)]
            inv = lax.rsqrt(jnp.sum(hc * hc, axis=-1, keepdims=True) + EPS)
            scale = jnp.where(is_qk, inv * jnp.where(is_q, C_DK ** -0.5, 1.0), 1.0)
            o_ref[rows, C_DK * c:C_DK * (c + 1)] = hc * scale


def _qkv_conv(cfg, x, mod, layer, w_in, conv_w, idx):
    tm, tn = cfg.tm, 256
    n = 3 * C_WIDTH
    xp, xn = _halo_specs(cfg, tm, D_MODEL)
    return pl.pallas_call(
        functools.partial(_qkv_conv_kernel, cfg, tm, tn),
        grid=(cfg.t_all // tm, n // tn),
        in_specs=[
            pl.BlockSpec((tm, D_MODEL), lambda i, j: (i, 0)),
            xp, xn,
            _mod_spec(cfg, layer, 1, tm),
            _mod_spec(cfg, layer, 0, tm),
            pl.BlockSpec((None, D_MODEL, tn), lambda i, j: (idx, 0, j)),
            pl.BlockSpec((None, 3, tn), lambda i, j: (idx, 0, j)),
        ],
        out_specs=pl.BlockSpec((tm, tn), lambda i, j: (i, j)),
        out_shape=jax.ShapeDtypeStruct((cfg.t_all, n), F32),
        scratch_shapes=[pltpu.VMEM((tm + 2 * HALO, D_MODEL), BF16),
                        pltpu.VMEM((tm // ROW_CHUNK, ROW_CHUNK + 2 * HALO, tn), F32)],
        compiler_params=_cparams(("arbitrary", "arbitrary")),
        name=f"qkv_conv_l{layer}",
    )(x, x, x, mod, mod, w_in, conv_w)


DN_BLOCK = 64
DN_NBLK = SEQ_TILE // DN_BLOCK


def _tri_ones():
    n = SEQ_TILE
    idx = np.arange(n)
    lower = (idx[None, :] <= idx[:, None]).astype(np.float32)
    return jnp.asarray(np.stack([lower, lower.T]), BF16)


def _unit_tri_inverses(packs):
    n, w = DN_BLOCK, SEQ_TILE
    row = lax.broadcasted_iota(jnp.int32, (n, w), 0)
    col = jnp.bitwise_and(lax.broadcasted_iota(jnp.int32, (n, w), 1), n - 1)
    eye = (row == col).astype(F32)
    band = lax.broadcasted_iota(jnp.int32, (w, w), 0) // n
    keep = (band == lax.broadcasted_iota(jnp.int32, (w, w), 1) // n).astype(BF16)

    def block_diag(x):
        return jnp.concatenate([x.astype(BF16)] * DN_NBLK, axis=0) * keep

    ts = [eye - jnp.where((row >> 1) == (col >> 1), a, 0.0) for a in packs]
    for l in range(1, int(math.log2(n))):
        couple = jnp.logical_and((row >> (l + 1)) == (col >> (l + 1)), (row >> l) != (col >> l))
        ws = [_dot(jnp.where(couple, a, 0.0).astype(BF16), block_diag(t)) for a, t in zip(packs, ts)]
        ts = [t - _dot(t.astype(BF16), block_diag(wv)) for t, wv in zip(ts, ws)]
    return ts


def _delta_tile(dirs):
    n = SEQ_TILE
    row = lax.broadcasted_iota(jnp.int32, (n, n), 0)
    col = lax.broadcasted_iota(jnp.int32, (n, n), 1)
    probs = []
    for d, (q_ref, k_ref, v_ref, g, beta, tri_ref, s_scr, o_ref) in enumerate(dirs):
        incl = (col <= row) if d == 0 else (col >= row)
        strict = (col < row) if d == 0 else (col > row)
        last = n - 1 if d == 0 else 0
        g_hi, g_lo = _split_bf16(g)
        b_col = _dot(tri_ref[...], g_hi) + _dot(tri_ref[...], g_lo)
        b_row = b_col.T
        for h in range(C_HEADS):
            c = d * C_HEADS + h
            cols = slice(C_DK * h, C_DK * (h + 1))
            q, k, v = q_ref[:, cols], k_ref[:, cols], v_ref[:, cols]
            bc = b_col[:, c:c + 1]
            br = b_row[c:c + 1, :]
            b_last = b_col[last:last + 1, c:c + 1]
            ld = jnp.exp(jnp.where(incl, bc - br, -jnp.inf))
            bt = beta[:, 2 * C_HEADS + c:2 * C_HEADS + c + 1]
            kb = k * bt
            kbf = k.astype(BF16)
            e_col = jnp.exp(bc)
            probs.append(dict(
                d=d, h=h, cols=cols, s_scr=s_scr, o_ref=o_ref,
                a_mat=jnp.where(strict, _dot_nt(kb.astype(BF16), kbf) * ld, 0.0),
                aqk=(_dot_nt(q.astype(BF16), kbf) * ld).astype(BF16),
                rhs=jnp.concatenate([v * bt, kb * e_col], axis=1),
                qd=(q * e_col).astype(BF16),
                kd_t=(k * jnp.exp(b_last - bc)).T.astype(BF16),
                dec=jnp.exp(b_last)))

    blk = lambda b: slice(DN_BLOCK * b, DN_BLOCK * (b + 1))
    on_diag = (row // DN_BLOCK) == (col // DN_BLOCK)
    packs = []
    for p in probs:
        dm = jnp.where(on_diag, p["a_mat"], 0.0)
        packs.append(sum(dm[blk(b), :] for b in range(1, DN_NBLK)) + dm[blk(0), :])
    t_inv = _unit_tri_inverses(packs)

    xs = [[None] * DN_NBLK for _ in probs]
    for step in range(DN_NBLK):
        for pi, p in enumerate(probs):
            b = step if p["d"] == 0 else DN_NBLK - 1 - step
            resid = p["rhs"][blk(b), :]
            for bb in range(DN_NBLK):
                if xs[pi][bb] is not None:
                    resid = resid - _dot(p["a_mat"][blk(b), blk(bb)].astype(BF16), xs[pi][bb].astype(BF16))
            p["resid"] = resid
        for pi, p in enumerate(probs):
            b = step if p["d"] == 0 else DN_NBLK - 1 - step
            xs[pi][b] = _dot(t_inv[pi][:, blk(b)].astype(BF16), p["resid"].astype(BF16))

    for pi, p in enumerate(probs):
        x = jnp.concatenate(xs[pi], axis=0)
        p["u"], p["w"] = x[:, :C_DK], x[:, C_DK:].astype(BF16)
        p["s"] = p["s_scr"][p["h"]]
        p["sb"] = p["s"].astype(BF16)
    for p in probs:
        p["vb"] = (p["u"] - _dot(p["w"], p["sb"])).astype(BF16)
    for p in probs:
        p["o_ref"][:, p["cols"]] = _dot(p["qd"], p["sb"]) + _dot(p["aqk"], p["vb"])
    for p in probs:
        p["s_scr"][p["h"]] = p["s"] * p["dec"] + _dot(p["kd_t"], p["vb"])


def _delta_kernel(cfg, qf, kf, vf, abf, qb, kb, vb, abb, s0_ref, alog_ref, dtb_ref, tri_ref,
                  of_ref, ob_ref, st_ref, sf_scr, sb_scr):
    i = pl.program_id(0)

    _load_states(cfg, i, s0_ref, sf_scr, sb_scr)

    lane = lax.broadcasted_iota(jnp.int32, (1, 128), 1)
    neg_a = jnp.where(lane < 2 * C_HEADS, -jnp.exp(alog_ref[...]), 0.0)

    def gate(ab):
        a_in = ab + dtb_ref[...]
        softplus = jnp.maximum(a_in, 0.0) + jnp.log1p(jnp.exp(-jnp.abs(a_in)))
        return neg_a * softplus, jax.nn.sigmoid(ab)

    g_f, beta_f = gate(abf[...])
    g_b, beta_b = gate(abb[...])
    _delta_tile([(qf, kf, vf, g_f, beta_f, tri_ref.at[0], sf_scr, of_ref),
                 (qb, kb, vb, g_b, beta_b, tri_ref.at[1], sb_scr, ob_ref)])

    _store_states(cfg, i, st_ref, sf_scr, sb_scr)


def _delta(cfg, qkv, zab, s0, slot, a_log, dt_bias, tri):
    rev = cfg.rev_tile
    n = SEQ_TILE
    st_shape = (cfg.n_ctx, 2, C_HEADS, C_DK, C_DK)
    s0_spec, st_spec = _state_specs(cfg, slot, st_shape[1:])
    ab_col = C_WIDTH // 128
    in_specs = []
    for tile_of in (lambda i: i, rev):
        for cb in range(3):
            in_specs.append(pl.BlockSpec((n, C_WIDTH), functools.partial(lambda i, f, c: (f(i), c), f=tile_of, c=cb)))
        in_specs.append(pl.BlockSpec((n, 128), functools.partial(lambda i, f: (f(i), ab_col), f=tile_of)))
    in_specs += [
        s0_spec,
        pl.BlockSpec((1, 128), lambda i: (0, 0)),
        pl.BlockSpec((1, 128), lambda i: (0, 0)),
        pl.BlockSpec(tri.shape, lambda i: (0, 0, 0)),
    ]
    return pl.pallas_call(
        functools.partial(_delta_kernel, cfg),
        grid=(cfg.n_tiles,),
        in_specs=in_specs,
        out_specs=[
            pl.BlockSpec((n, C_WIDTH), lambda i: (i, 0)),
            pl.BlockSpec((n, C_WIDTH), lambda i: (rev(i), 0)),
            st_spec,
        ],
        out_shape=[
            jax.ShapeDtypeStruct((cfg.t_all, C_WIDTH), F32),
            jax.ShapeDtypeStruct((cfg.t_all, C_WIDTH), F32),
            jax.ShapeDtypeStruct(st_shape, F32),
        ],
        scratch_shapes=[pltpu.VMEM(st_shape[2:], F32), pltpu.VMEM(st_shape[2:], F32)],
        compiler_params=_cparams(("arbitrary",)),
        name="delta_scan",
    )(qkv, qkv, qkv, zab, qkv, qkv, qkv, zab, s0, a_log, dt_bias, tri)


def _out_odd_kernel(of_ref, ob_ref, z_ref, x_ref, g1_ref, norm_ref, wout_ref, lng_ref, lnb_ref, o_ref):
    n = SEQ_TILE
    o = of_ref[...] + ob_ref[...]
    z = z_ref[...]
    y = jnp.zeros((n, D_MODEL), F32)
    heads = _rms_heads(o, norm_ref[...], C_HEADS, C_DK)
    for h in range(C_HEADS):
        a = (heads[h] * _silu(z[:, C_DK * h:C_DK * (h + 1)])).astype(BF16)
        y = y + _dot(a, wout_ref[C_DK * h:C_DK * (h + 1), :].astype(BF16))
    o_ref[...] = _layernorm_rows(ALPHA * x_ref[...] + g1_ref[...] * y, lng_ref[...], lnb_ref[...])


def _out_odd(cfg, of, ob, zab, x, mod, layer, norm_g, w_out, ln_g, ln_b):
    n, e = SEQ_TILE, layer // 2
    return pl.pallas_call(
        _out_odd_kernel,
        grid=(cfg.n_tiles,),
        in_specs=[
            pl.BlockSpec((n, C_WIDTH), lambda i: (i, 0)),
            pl.BlockSpec((n, C_WIDTH), lambda i: (i, 0)),
            pl.BlockSpec((n, C_WIDTH), lambda i: (i, 0)),
            pl.BlockSpec((n, D_MODEL), lambda i: (i, 0)),
            _mod_spec(cfg, layer, 2, n),
            pl.BlockSpec((None, 1, C_DK), lambda i: (e, 0, 0)),
            pl.BlockSpec((None, D_MODEL, D_MODEL), lambda i: (e, 0, 0)),
            pl.BlockSpec((None, 1, D_MODEL), lambda i: (layer, 0, 0)),
            pl.BlockSpec((None, 1, D_MODEL), lambda i: (layer, 0, 0)),
        ],
        out_specs=pl.BlockSpec((n, D_MODEL), lambda i: (i, 0)),
        out_shape=jax.ShapeDtypeStruct((cfg.t_all, D_MODEL), F32),
        compiler_params=_cparams(("arbitrary",)),
        name=f"out_odd_l{layer}",
    )(of, ob, zab, x, mod, norm_g, w_out, ln_g, ln_b)


FFN_TF = 256


def _ffn_kernel(cfg, tm, split_out, x_ref, xp_ref, xn_ref, sc_ref, sh_ref, g2_ref, wa_ref, wg_ref, ca_ref, cg_ref,
                wd_ref, lng_ref, lnb_ref, *rest):
    out_refs, (u_scr, ya_scr, yg_scr, acc_scr) = rest[:-4], rest[-4:]
    i, j = pl.program_id(0), pl.program_id(1)

    @pl.when(j == 0)
    def _():
        sc, sh = 1.0 + sc_ref[...], sh_ref[...]
        u_scr[0:HALO, :] = (xp_ref[...] * sc + sh).astype(BF16)
        u_scr[HALO:HALO + tm, :] = (x_ref[...] * sc + sh).astype(BF16)
        u_scr[HALO + tm:, :] = (xn_ref[...] * sc + sh).astype(BF16)
        acc_scr[...] = jnp.zeros_like(acc_scr)

    wa, wg, wd = wa_ref[...].astype(BF16), wg_ref[...].astype(BF16), wd_ref[...].astype(BF16)
    ca, cg = ca_ref[...], cg_ref[...]
    nchunk = tm // ROW_CHUNK

    def up(r):
        u = u_scr[ROW_CHUNK * r:ROW_CHUNK * (r + 1) + 2 * HALO, :]
        ya_scr[r] = _dot(u, wa)
        yg_scr[r] = _dot(u, wg)

    up(0)
    for r in range(nchunk):
        if r + 1 < nchunk:
            up(r + 1)
        edges = cfg.chunk_edges(i * tm + ROW_CHUNK * r)
        act = (_silu(_conv3(yg_scr.at[r], cg, *edges)) * _conv3(ya_scr.at[r], ca, *edges)).astype(BF16)
        acc_scr[ROW_CHUNK * r:ROW_CHUNK * (r + 1), :] += _dot(act, wd)

    @pl.when(j == pl.num_programs(1) - 1)
    def _():
        y = _layernorm_rows(ALPHA * x_ref[...] + g2_ref[...] * acc_scr[...], lng_ref[...], lnb_ref[...])
        if split_out:
            is_ctx = i * tm < cfg.t_ctx

            @pl.when(is_ctx)
            def _():
                out_refs[0][...] = y

            @pl.when(jnp.logical_not(is_ctx))
            def _():
                out_refs[1][...] = y
        else:
            out_refs[0][...] = y


def _ffn(cfg, x, mod, layer, w_up, conv_w, w_down, ln_g, ln_b, split_out=False):
    tm, tf = cfg.tm, FFN_TF
    nj = D_FF // tf
    n_ctx_tiles = cfg.t_ctx // tm
    xp, xn = _halo_specs(cfg, tm, D_MODEL)
    nchunk = tm // ROW_CHUNK
    if split_out:
        out_specs = [
            pl.BlockSpec((tm, D_MODEL), lambda i, j: (jnp.minimum(i, n_ctx_tiles - 1), 0)),
            pl.BlockSpec((tm, D_MODEL), lambda i, j: (jnp.maximum(i - n_ctx_tiles, 0), 0)),
        ]
        out_shape = [jax.ShapeDtypeStruct((cfg.t_ctx, D_MODEL), F32),
                     jax.ShapeDtypeStruct((cfg.t_all - cfg.t_ctx, D_MODEL), F32)]
    else:
        out_specs = pl.BlockSpec((tm, D_MODEL), lambda i, j: (i, 0))
        out_shape = jax.ShapeDtypeStruct((cfg.t_all, D_MODEL), F32)
    return pl.pallas_call(
        functools.partial(_ffn_kernel, cfg, tm, split_out),
        grid=(cfg.t_all // tm, nj),
        in_specs=[
            pl.BlockSpec((tm, D_MODEL), lambda i, j: (i, 0)),
            xp, xn,
            _mod_spec(cfg, layer, 4, tm),
            _mod_spec(cfg, layer, 3, tm),
            _mod_spec(cfg, layer, 5, tm),
            pl.BlockSpec((None, D_MODEL, tf), lambda i, j: (layer, 0, j)),
            pl.BlockSpec((None, D_MODEL, tf), lambda i, j: (layer, 0, nj + j)),
            pl.BlockSpec((None, 3, tf), lambda i, j: (layer, 0, j)),
            pl.BlockSpec((None, 3, tf), lambda i, j: (layer, 0, nj + j)),
            pl.BlockSpec((None, tf, D_MODEL), lambda i, j: (layer, j, 0)),
            pl.BlockSpec((None, 1, D_MODEL), lambda i, j: (layer, 0, 0)),
            pl.BlockSpec((None, 1, D_MODEL), lambda i, j: (layer, 0, 0)),
        ],
        out_specs=out_specs,
        out_shape=out_shape,
        scratch_shapes=[
            pltpu.VMEM((tm + 2 * HALO, D_MODEL), BF16),
            pltpu.VMEM((nchunk, ROW_CHUNK + 2 * HALO, tf), F32),
            pltpu.VMEM((nchunk, ROW_CHUNK + 2 * HALO, tf), F32),
            pltpu.VMEM((tm, D_MODEL), F32),
        ],
        compiler_params=_cparams(("arbitrary", "arbitrary")),
        name=f"ffn_l{layer}",
    )(x, x, x, mod, mod, mod, w_up, w_up, conv_w, conv_w, w_down, ln_g, ln_b)


def _grid_pos_embed(n_tokens):
    rows = n_tokens // GRID_W
    r = jnp.broadcast_to(jnp.arange(rows, dtype=F32)[:, None], (rows, GRID_W)).reshape(-1)
    col = jnp.broadcast_to(jnp.arange(GRID_W, dtype=F32)[None, :], (rows, GRID_W)).reshape(-1)
    quarter = D_MODEL // 4
    freq = jnp.exp(-math.log(10000.0) * jnp.arange(quarter, dtype=F32) / quarter)
    ra, ca = r[:, None] * freq, col[:, None] * freq
    return jnp.concatenate([jnp.sin(ra), jnp.cos(ra), jnp.sin(ca), jnp.cos(ca)], -1)


def kernel(x_prompt, x_sample, state_gla, state_dn, c, c_ctx, w_mod, b_mod, ln1_g, ln1_b, ln2_g, ln2_b,
           a_w_in, a_w_gate, a_b_gate, a_norm, b_proj, b_scale, a_w_out,
           c_w_in, c_conv, c_a_log, c_dt_bias, c_norm, c_w_out, f_w_up, f_conv, f_w_down):
    n_ctx, seq, _ = x_prompt.shape
    n_smp, smp_len, _ = x_sample.shape
    assert seq == SEQ_TILE and n_smp + 1 <= MOD_ROWS
    cfg = _Cfg(n_ctx, n_smp, smp_len)

    x = _embed(cfg, x_prompt.reshape(-1, D_MODEL), x_sample.reshape(-1, D_MODEL), _grid_pos_embed(smp_len))
    cond = jnp.concatenate([c_ctx[None, :], c, jnp.zeros((MOD_ROWS - 1 - n_smp, D_MODEL), F32)], axis=0)
    mod = _modulation(cond, w_mod, b_mod)

    gla_consts = _gla_constants()
    tri = _tri_ones()
    row3 = lambda t: t.reshape(t.shape[0], 1, t.shape[1])
    ln1g, ln1b, ln2g, ln2b = row3(ln1_g), row3(ln1_b), row3(ln2_g), row3(ln2_b)
    s0_gla = state_gla.astype(F32).reshape(n_smp, -1, 2, A_HEADS // 2, 128, A_DV)
    s0_dn = state_dn.astype(F32)
    gla_states, dn_states = [], []
    for layer in range(DEPTH):
        if layer % 2 == 0:
            e = layer // 2
            w_in = a_w_in[e]
            w_main = jnp.concatenate([w_in[:, :2 * A_QK + 2 * A_WIDTH], w_in[:, -B_WIDTH:]], axis=1)
            lr0 = 2 * A_QK + 2 * A_WIDTH
            w_lr = jnp.pad(w_in[:, lr0:lr0 + 2 * A_RANK], ((0, 0), (0, 128 - 2 * A_RANK)))
            w_gate_bd = jnp.zeros((128, 2 * A_QK), F32)
            w_gate_bd = w_gate_bd.at[:A_RANK, :A_QK].set(a_w_gate[e, 0]).at[A_RANK:2 * A_RANK, A_QK:].set(a_w_gate[e, 1])
            proj = _proj(cfg, x, mod, layer, w_main, 512)
            gates = _gates(cfg, x, mod, layer, w_lr, w_gate_bd, a_b_gate[e].reshape(1, 2 * A_QK))
            of, ob, st = _gla(cfg, proj, gates, s0_gla, e, gla_consts)
            gla_states.append(st.reshape(n_ctx, 2, A_HEADS, A_DK, A_DV))
            x = _out_even(cfg, of, ob, proj, x, mod, layer, row3(a_norm), b_proj, row3(b_scale), a_w_out, ln1g, ln1b)
        else:
            o_ = layer // 2
            w_zab = jnp.pad(c_w_in[o_][:, 3 * C_WIDTH:], ((0, 0), (0, 128 - 4 * C_HEADS)))
            qkv = _qkv_conv(cfg, x, mod, layer, c_w_in, c_conv, o_)
            zab = _proj(cfg, x, mod, layer, w_zab, 384)
            pad16 = lambda t: jnp.pad(t.reshape(1, -1), ((0, 0), (0, 128 - 2 * C_HEADS)))
            of, ob, st = _delta(cfg, qkv, zab, s0_dn, o_, pad16(c_a_log[o_]), pad16(c_dt_bias[o_]), tri)
            dn_states.append(st)
            x = _out_odd(cfg, of, ob, zab, x, mod, layer, row3(c_norm), c_w_out, ln1g, ln1b)
        x = _ffn(cfg, x, mod, layer, f_w_up, f_conv, f_w_down, ln2g, ln2b, split_out=layer == DEPTH - 1)

    y_prompt = x[0].reshape(n_ctx, seq, D_MODEL)
    y_sample = x[1].reshape(n_smp, smp_len, D_MODEL)
    return (y_prompt, y_sample, jnp.stack(gla_states, axis=1).astype(x_prompt.dtype),
            jnp.stack(dn_states, axis=1).astype(x_prompt.dtype))
```

```python
import functools
import math

import numpy as np
import jax
import jax.numpy as jnp
from jax import lax
from jax.experimental import pallas as pl
from jax.experimental.pallas import tpu as pltpu

F32 = jnp.float32
BF16 = jnp.bfloat16

D_MODEL = 1024
DEPTH = 4
GRID_W = 64
A_HEADS = 4
A_DK = 64
A_DV = 128
A_QK = A_HEADS * A_DK
A_WIDTH = A_HEADS * A_DV
A_RANK = 16
A_GATE_NORM = 16.0
B_WIDTH = 512
POOL_WINDOWS = (2, 4, 8, 16)
B_GW = 128
C_HEADS = 8
C_DK = 128
C_WIDTH = 1024
D_FF = 2816
N_MOD = 6
ALPHA = (2 * DEPTH) ** 0.25
EPS = 1e-6

SEQ_TILE = 256
HALO = 8
ROW_CHUNK = 256
assert ROW_CHUNK == SEQ_TILE
MOD_ROWS = 8
V7X_VMEM_LIMIT = 56 * 1024 * 1024


def _cparams(sem):
    return pltpu.CompilerParams(dimension_semantics=sem, vmem_limit_bytes=V7X_VMEM_LIMIT)


class _Cfg:
    def __init__(self, n_ctx, n_smp, smp_len):
        assert smp_len % SEQ_TILE == 0 and smp_len & (smp_len - 1) == 0
        self.n_ctx, self.n_smp, self.smp_len = n_ctx, n_smp, smp_len
        self.t_ctx = n_ctx * SEQ_TILE
        self.t_all = self.t_ctx + n_smp * smp_len
        self.tps = smp_len // SEQ_TILE
        self.n_tiles = self.t_all // SEQ_TILE
        self.n_seq = n_ctx + n_smp
        tm = 1024
        while self.t_ctx % tm or smp_len % tm:
            tm //= 2
        self.tm = tm

    def cond_of_row(self, g):
        return jnp.where(g < self.t_ctx, 0, 1 + (g - self.t_ctx) // self.smp_len)

    def seq_of_tile(self, i):
        return jnp.where(i < self.n_ctx, i, self.n_ctx + (i - self.n_ctx) // self.tps)

    def rev_tile(self, i):
        m = (i - self.n_ctx) % self.tps
        return jnp.where(i < self.n_ctx, i, i - m + (self.tps - 1 - m))

    def chunk_edges(self, row0):
        is_ctx = row0 < self.t_ctx
        length = jnp.where(is_ctx, SEQ_TILE, self.smp_len)
        pos0 = jnp.bitwise_and(jnp.where(is_ctx, row0, row0 - self.t_ctx), length - 1)
        return pos0 == 0, pos0 + ROW_CHUNK == length

    def first_of_seq(self, i):
        return jnp.logical_or(i < self.n_ctx, (i - self.n_ctx) % self.tps == 0)

    def last_of_seq(self, i):
        return jnp.logical_or(i < self.n_ctx, (i - self.n_ctx) % self.tps == self.tps - 1)


def _mod_map(cfg, layer, which, rows):
    def index_map(i, *_):
        return ((layer * MOD_ROWS + cfg.cond_of_row(i * rows)) * N_MOD + which, 0, 0)
    return index_map


def _mod_spec(cfg, layer, which, rows):
    return pl.BlockSpec((None, 1, D_MODEL), _mod_map(cfg, layer, which, rows))


def _halo_specs(cfg, rows, width, col_block=0):
    per = rows // HALO
    last = cfg.t_all // HALO - 1
    prev = pl.BlockSpec((HALO, width), lambda i, *_: (jnp.maximum(i * per - 1, 0), col_block))
    nxt = pl.BlockSpec((HALO, width), lambda i, *_: (jnp.minimum((i + 1) * per, last), col_block))
    return prev, nxt


def _seq_edges(cfg, tile, rows):
    g = tile * rows + lax.broadcasted_iota(jnp.int32, (rows, 1), 0)
    is_ctx = g < cfg.t_ctx
    length = jnp.where(is_ctx, SEQ_TILE, cfg.smp_len)
    pos = jnp.bitwise_and(jnp.where(is_ctx, g, g - cfg.t_ctx), length - 1)
    return pos, length


def _silu(x):
    return x * jax.nn.sigmoid(x)


def _dot(a, b):
    return jnp.dot(a, b, preferred_element_type=F32)


def _dot_nt(a, b):
    return lax.dot_general(a, b, (((1,), (1,)), ((), ())), preferred_element_type=F32)


def _split_bf16(x):
    hi = x.astype(BF16)
    lo = (x - hi.astype(F32)).astype(BF16)
    return hi, lo


def _conv3(y_ref, cw, starts_seq, ends_seq):
    n = ROW_CHUNK
    y_ref[HALO - 1:HALO, :] = jnp.where(starts_seq, 0.0, y_ref[HALO - 1:HALO, :])
    y_ref[HALO + n:HALO + n + 1, :] = jnp.where(ends_seq, 0.0, y_ref[HALO + n:HALO + n + 1, :])
    return (y_ref[HALO - 1:HALO - 1 + n, :] * cw[0:1, :]
            + y_ref[HALO:HALO + n, :] * cw[1:2, :]
            + y_ref[HALO + 1:HALO + 1 + n, :] * cw[2:3, :])


def _layernorm_rows(x, g, b):
    mu = jnp.mean(x, axis=-1, keepdims=True)
    xc = x - mu
    var = jnp.mean(xc * xc, axis=-1, keepdims=True)
    return xc * lax.rsqrt(var + EPS) * g + b


def _state_specs(cfg, slot, per_seq):
    zeros = (0,) * len(per_seq)
    n_ctx = cfg.n_ctx
    in_spec = pl.BlockSpec((None, None) + per_seq,
                           lambda i: (jnp.maximum(cfg.seq_of_tile(i) - n_ctx, 0), slot) + zeros)
    out_spec = pl.BlockSpec((None,) + per_seq, lambda i: (jnp.minimum(cfg.seq_of_tile(i), n_ctx - 1),) + zeros)
    return in_spec, out_spec


def _load_states(cfg, i, s0_ref, sf_scr, sb_scr):
    @pl.when(cfg.first_of_seq(i))
    def _():
        is_ctx = i < cfg.n_ctx
        sf_scr[...] = jnp.where(is_ctx, 0.0, s0_ref[0])
        sb_scr[...] = jnp.where(is_ctx, 0.0, s0_ref[1])


def _store_states(cfg, i, st_ref, sf_scr, sb_scr):
    @pl.when(i < cfg.n_ctx)
    def _():
        st_ref[0] = sf_scr[...]
        st_ref[1] = sb_scr[...]


def _embed_kernel(cfg, xp_ref, xs_ref, pe_ref, o_ref):
    i = pl.program_id(0)

    @pl.when(i < cfg.n_ctx)
    def _():
        o_ref[...] = xp_ref[...]

    @pl.when(i >= cfg.n_ctx)
    def _():
        o_ref[...] = xs_ref[...] + pe_ref[...]


def _embed(cfg, xp, xs, pe):
    n_ctx, tps = cfg.n_ctx, cfg.tps
    return pl.pallas_call(
        functools.partial(_embed_kernel, cfg),
        grid=(cfg.n_tiles,),
        in_specs=[
            pl.BlockSpec((SEQ_TILE, D_MODEL), lambda i: (jnp.minimum(i, n_ctx - 1), 0)),
            pl.BlockSpec((SEQ_TILE, D_MODEL), lambda i: (jnp.maximum(i - n_ctx, 0), 0)),
            pl.BlockSpec((SEQ_TILE, D_MODEL), lambda i: (jnp.maximum(i - n_ctx, 0) % tps, 0)),
        ],
        out_specs=pl.BlockSpec((SEQ_TILE, D_MODEL), lambda i: (i, 0)),
        out_shape=jax.ShapeDtypeStruct((cfg.t_all, D_MODEL), F32),
        compiler_params=_cparams(("arbitrary",)),
        name="embed",
    )(xp, xs, pe)


CAST_ROWS = 256


def _cast_blocks_kernel(tn, nblk, col0, w_ref, o_ref):
    for c in range(nblk):
        o_ref[c] = w_ref[:, col0 + tn * c:col0 + tn * (c + 1)].astype(BF16)


def _cast_blocks(w, tn, ncols, col0=0):
    n_l, k, n = w.shape
    nblk = ncols // tn
    return pl.pallas_call(
        functools.partial(_cast_blocks_kernel, tn, nblk, col0),
        grid=(n_l, k // CAST_ROWS),
        in_specs=[pl.BlockSpec((None, CAST_ROWS, n), lambda l, r: (l, r, 0))],
        out_specs=pl.BlockSpec((None, nblk, CAST_ROWS, tn), lambda l, r: (l, 0, r, 0)),
        out_shape=jax.ShapeDtypeStruct((n_l, nblk, k, tn), BF16),
        compiler_params=_cparams(("arbitrary", "arbitrary")),
        name=f"cast_blocks_{n}",
    )(w)


def _mod_kernel(c_ref, w_ref, b_ref, o_ref):
    sc = _silu(c_ref[...]).astype(BF16)
    o_ref[...] = _dot(sc, w_ref[...].astype(BF16)) + b_ref[...]


def _modulation(cond, w_mod, b_mod):
    out = pl.pallas_call(
        _mod_kernel,
        grid=(DEPTH, N_MOD),
        in_specs=[
            pl.BlockSpec((MOD_ROWS, D_MODEL), lambda l, j: (0, 0)),
            pl.BlockSpec((None, D_MODEL, D_MODEL), lambda l, j: (l, 0, j)),
            pl.BlockSpec((None, 1, D_MODEL), lambda l, j: (l, 0, j)),
        ],
        out_specs=pl.BlockSpec((None, MOD_ROWS, D_MODEL), lambda l, j: (l, 0, j)),
        out_shape=jax.ShapeDtypeStruct((DEPTH, MOD_ROWS, N_MOD * D_MODEL), F32),
        compiler_params=_cparams(("arbitrary", "arbitrary")),
        name="modulation",
    )(cond, w_mod, b_mod.reshape(DEPTH, 1, N_MOD * D_MODEL))
    return out.reshape(DEPTH * MOD_ROWS * N_MOD, 1, D_MODEL)


def _proj_kernel(x_ref, sc_ref, sh_ref, w_ref, o_ref, u_scr):
    @pl.when(pl.program_id(1) == 0)
    def _():
        u_scr[...] = (x_ref[...] * (1.0 + sc_ref[...]) + sh_ref[...]).astype(BF16)

    o_ref[...] = _dot(u_scr[...], w_ref[...].astype(BF16))


def _proj(cfg, x, mod, layer, w, tn):
    n = w.shape[1]
    tm = cfg.tm
    return pl.pallas_call(
        _proj_kernel,
        grid=(cfg.t_all // tm, n // tn),
        in_specs=[
            pl.BlockSpec((tm, D_MODEL), lambda i, j: (i, 0)),
            _mod_spec(cfg, layer, 1, tm),
            _mod_spec(cfg, layer, 0, tm),
            pl.BlockSpec((D_MODEL, tn), lambda i, j: (0, j)),
        ],
        out_specs=pl.BlockSpec((tm, tn), lambda i, j: (i, j)),
        out_shape=jax.ShapeDtypeStruct((cfg.t_all, n), F32),
        scratch_shapes=[pltpu.VMEM((tm, D_MODEL), BF16)],
        compiler_params=_cparams(("arbitrary", "arbitrary")),
        name=f"proj_l{layer}",
    )(x, mod, mod, w)


def _gate_kernel(x_ref, sc_ref, sh_ref, wlr_ref, wg_ref, bg_ref, o_ref):
    u = (x_ref[...] * (1.0 + sc_ref[...]) + sh_ref[...]).astype(BF16)
    lr = _dot(u, wlr_ref[...].astype(BF16))
    z = _dot(lr.astype(BF16), wg_ref[...].astype(BF16)) + bg_ref[...]
    log_sig = jnp.minimum(z, 0.0) - jnp.log1p(jnp.exp(-jnp.abs(z)))
    o_ref[...] = log_sig / A_GATE_NORM


def _gates(cfg, x, mod, layer, w_lr, w_gate_bd, b_gate):
    tm = cfg.tm
    return pl.pallas_call(
        _gate_kernel,
        grid=(cfg.t_all // tm,),
        in_specs=[
            pl.BlockSpec((tm, D_MODEL), lambda i: (i, 0)),
            _mod_spec(cfg, layer, 1, tm),
            _mod_spec(cfg, layer, 0, tm),
            pl.BlockSpec((D_MODEL, 128), lambda i: (0, 0)),
            pl.BlockSpec((128, 2 * A_QK), lambda i: (0, 0)),
            pl.BlockSpec((1, 2 * A_QK), lambda i: (0, 0)),
        ],
        out_specs=pl.BlockSpec((tm, 2 * A_QK), lambda i: (i, 0)),
        out_shape=jax.ShapeDtypeStruct((cfg.t_all, 2 * A_QK), F32),
        compiler_params=_cparams(("arbitrary",)),
        name=f"gates_l{layer}",
    )(x, mod, mod, w_lr, w_gate_bd, b_gate)


GLA_LEVELS = 8
GLA_STACK = GLA_LEVELS + 2


def _gla_constants():
    n = SEQ_TILE
    idx = np.arange(n)
    i, m = idx[:, None], idx[None, :]
    stack = np.zeros((2, GLA_STACK, n, n), np.float32)
    masks = np.zeros((2, GLA_LEVELS + 1, n, n), np.float32)
    stack[0, 0] = m <= i
    stack[0, 1] = m > i
    stack[1, 0] = m >= i
    stack[1, 1] = m < i
    for l in range(GLA_LEVELS):
        s = 1 << l
        blk = idx // (2 * s)
        upper = (idx // s) % 2 == 1
        piv_f = blk * 2 * s + s - 1
        piv_b = blk * 2 * s + s
        up, pf, pb = upper[:, None], piv_f[:, None], piv_b[:, None]
        stack[0, 2 + l] = np.where(up, (m > pf) & (m <= i), (m > i) & (m <= pf))
        stack[1, 2 + l] = np.where(up, (m >= pb) & (m < i), (m >= i) & (m < pb))
        same = blk[:, None] == blk[None, :]
        masks[0, l] = same & upper[:, None] & ~upper[None, :]
        masks[1, l] = same & ~upper[:, None] & upper[None, :]
    masks[:, GLA_LEVELS] = np.eye(n)
    return (jnp.asarray(stack.reshape(2, GLA_STACK * n, n), BF16), jnp.asarray(masks, F32))


def _gla_direction(qk, v, g, stack_ref, mask_ref, s_scr, o_ref):
    n = SEQ_TILE
    q = qk[:, :A_QK] * (A_DK ** -0.5)
    k = qk[:, A_QK:]
    g_hi, g_lo = _split_bf16(g)
    sums = _dot(stack_ref[...], g_hi) + _dot(stack_ref[...], g_lo)
    e_in = jnp.exp(sums[0:n])
    e_out = jnp.exp(sums[n:2 * n])
    lane = lax.broadcasted_iota(jnp.int32, (n, 128), 1)
    low_half = lane < A_DK

    att = [jnp.zeros((n, n), F32) for _ in range(A_HEADS)]
    for l in range(GLA_LEVELS + 1):
        if l < GLA_LEVELS:
            e = jnp.exp(sums[(2 + l) * n:(3 + l) * n])
            qs, ks = q * e, k * e
        else:
            qs, ks = q, k
        m = mask_ref[l]
        for p in range(A_HEADS // 2):
            qp = qs[:, 128 * p:128 * (p + 1)]
            kp = ks[:, 128 * p:128 * (p + 1)].astype(BF16)
            for hh in range(2):
                qh = jnp.where(low_half if hh == 0 else ~low_half, qp, 0.0).astype(BF16)
                att[2 * p + hh] = att[2 * p + hh] + m * _dot_nt(qh, kp)

    qd = q * e_in
    kd = k * e_out
    kd_t = kd.T.astype(BF16)
    g_t = g.T
    gt_hi, gt_lo = _split_bf16(g_t)
    ones = jnp.ones((n, 128), BF16)
    dec = jnp.exp(_dot(gt_hi, ones) + _dot(gt_lo, ones))
    for h in range(A_HEADS):
        p, hh = h // 2, h % 2
        v_h = v[:, A_DV * h:A_DV * (h + 1)].astype(BF16)
        s_pair = s_scr[p]
        qp = qd[:, 128 * p:128 * (p + 1)]
        qh = jnp.where(low_half if hh == 0 else ~low_half, qp, 0.0).astype(BF16)
        o_h = _dot(att[h].astype(BF16), v_h) + _dot(qh, s_pair.astype(BF16))
        o_ref[:, A_DV * h:A_DV * (h + 1)] = o_h
    for h in range(A_HEADS):
        p, hh = h // 2, h % 2
        v_h = v[:, A_DV * h:A_DV * (h + 1)].astype(BF16)
        rows = slice(A_DK * hh, A_DK * (hh + 1))
        s_old = s_scr[p, rows, :]
        s_scr[p, rows, :] = s_old * dec[A_DK * h:A_DK * (h + 1), :] + _dot(kd_t[A_DK * h:A_DK * (h + 1), :], v_h)


def _gla_kernel(cfg, qk_f, v_f, g_f, qk_b, v_b, g_b, s0_ref, stack_ref, mask_ref,
                of_ref, ob_ref, st_ref, sf_scr, sb_scr):
    i = pl.program_id(0)

    _load_states(cfg, i, s0_ref, sf_scr, sb_scr)

    _gla_direction(qk_f[...], v_f[...], g_f[...], stack_ref.at[0], mask_ref.at[0], sf_scr, of_ref)
    _gla_direction(qk_b[...], v_b[...], g_b[...], stack_ref.at[1], mask_ref.at[1], sb_scr, ob_ref)

    _store_states(cfg, i, st_ref, sf_scr, sb_scr)


def _gla(cfg, proj, gates, s0, slot, consts):
    stack, masks = consts
    rev = cfg.rev_tile
    n = SEQ_TILE
    st_shape = (cfg.n_ctx, 2, A_HEADS // 2, 128, A_DV)
    s0_spec, st_spec = _state_specs(cfg, slot, st_shape[1:])
    return pl.pallas_call(
        functools.partial(_gla_kernel, cfg),
        grid=(cfg.n_tiles,),
        in_specs=[
            pl.BlockSpec((n, 2 * A_QK), lambda i: (i, 0)),
            pl.BlockSpec((n, A_WIDTH), lambda i: (i, 1)),
            pl.BlockSpec((n, A_QK), lambda i: (i, 0)),
            pl.BlockSpec((n, 2 * A_QK), lambda i: (rev(i), 0)),
            pl.BlockSpec((n, A_WIDTH), lambda i: (rev(i), 1)),
            pl.BlockSpec((n, A_QK), lambda i: (rev(i), 1)),
            s0_spec,
            pl.BlockSpec(stack.shape, lambda i: (0, 0, 0)),
            pl.BlockSpec(masks.shape, lambda i: (0, 0, 0, 0)),
        ],
        out_specs=[
            pl.BlockSpec((n, A_WIDTH), lambda i: (i, 0)),
            pl.BlockSpec((n, A_WIDTH), lambda i: (rev(i), 0)),
            st_spec,
        ],
        out_shape=[
            jax.ShapeDtypeStruct((cfg.t_all, A_WIDTH), F32),
            jax.ShapeDtypeStruct((cfg.t_all, A_WIDTH), F32),
            jax.ShapeDtypeStruct(st_shape, F32),
        ],
        scratch_shapes=[pltpu.VMEM(st_shape[2:], F32), pltpu.VMEM(st_shape[2:], F32)],
        compiler_params=_cparams(("arbitrary",)),
        name="gla_scan",
    )(proj, proj, gates, proj, proj, gates, s0, stack, masks)


def _rms_heads(o, gain, n_heads, width):
    parts = []
    for h in range(n_heads):
        oh = o[:, width * h:width * (h + 1)]
        parts.append(oh * lax.rsqrt(jnp.mean(oh * oh, axis=-1, keepdims=True) + EPS) * gain)
    return parts


def _out_even_kernel(cfg, of_ref, ob_ref, r_ref, pz_ref, pzp_ref, pzn_ref, x_ref, g1_ref,
                     norm_ref, bproj_ref, bscale_ref, wout_ref, lng_ref, lnb_ref, o_ref, ext_scr, w_scr):
    n = SEQ_TILE
    i = pl.program_id(0)

    @pl.when(i == 0)
    def _():
        w_scr[...] = wout_ref[...].astype(BF16)

    o = of_ref[...] + ob_ref[...]
    r = r_ref[...]
    heads = _rms_heads(o, norm_ref[...], A_HEADS, A_DV)
    parts = [(heads[h] * _silu(r[:, A_DV * h:A_DV * (h + 1)])).astype(BF16) for h in range(A_HEADS)]

    pos, length = _seq_edges(cfg, i, n)
    starts_seq, ends_seq = cfg.chunk_edges(i * n)
    ext_scr[0:HALO, :] = jnp.where(starts_seq, 0.0, pzp_ref[...])
    ext_scr[HALO:HALO + n, :] = pz_ref[...]
    ext_scr[HALO + n:, :] = jnp.where(ends_seq, 0.0, pzn_ref[...])
    for gi, win in enumerate(POOL_WINDOWS):
        lo = win // 2
        hi = win - 1 - lo
        cols = slice(B_GW * gi, B_GW * (gi + 1))
        acc = ext_scr[HALO - lo:HALO - lo + n, cols]
        for d in range(-lo + 1, hi + 1):
            acc = acc + ext_scr[HALO + d:HALO + d + n, cols]
        cnt = (jnp.minimum(pos + hi + 1, length) - jnp.maximum(pos - lo, 0)).astype(F32)
        pooled = (acc / cnt - ext_scr[HALO:HALO + n, cols]).astype(BF16)
        mixed = _dot(pooled, bproj_ref[gi].astype(BF16)) * bscale_ref[:, cols]
        parts.append(mixed.astype(BF16))

    y = _dot(jnp.concatenate(parts, axis=1), w_scr[...])
    o_ref[...] = _layernorm_rows(ALPHA * x_ref[...] + g1_ref[...] * y, lng_ref[...], lnb_ref[...])


def _out_even(cfg, of, ob, proj, x, mod, layer, norm_g, b_proj, b_scale, w_out, ln_g, ln_b):
    n, e = SEQ_TILE, layer // 2
    pz_prev, pz_next = _halo_specs(cfg, n, B_WIDTH, col_block=3)
    return pl.pallas_call(
        functools.partial(_out_even_kernel, cfg),
        grid=(cfg.n_tiles,),
        in_specs=[
            pl.BlockSpec((n, A_WIDTH), lambda i: (i, 0)),
            pl.BlockSpec((n, A_WIDTH), lambda i: (i, 0)),
            pl.BlockSpec((n, A_WIDTH), lambda i: (i, 2)),
            pl.BlockSpec((n, B_WIDTH), lambda i: (i, 3)),
            pz_prev, pz_next,
            pl.BlockSpec((n, D_MODEL), lambda i: (i, 0)),
            _mod_spec(cfg, layer, 2, n),
            pl.BlockSpec((None, 1, A_DV), lambda i: (e, 0, 0)),
            pl.BlockSpec((None, len(POOL_WINDOWS), B_GW, B_GW), lambda i: (e, 0, 0, 0)),
            pl.BlockSpec((None, 1, B_WIDTH), lambda i: (e, 0, 0)),
            pl.BlockSpec((None, D_MODEL, D_MODEL), lambda i: (e, 0, 0)),
            pl.BlockSpec((None, 1, D_MODEL), lambda i: (layer, 0, 0)),
            pl.BlockSpec((None, 1, D_MODEL), lambda i: (layer, 0, 0)),
        ],
        out_specs=pl.BlockSpec((n, D_MODEL), lambda i: (i, 0)),
        out_shape=jax.ShapeDtypeStruct((cfg.t_all, D_MODEL), F32),
        scratch_shapes=[pltpu.VMEM((n + 2 * HALO, B_WIDTH), F32), pltpu.VMEM((D_MODEL, D_MODEL), BF16)],
        compiler_params=_cparams(("arbitrary",)),
        name=f"out_even_l{layer}",
    )(of, ob, proj, proj, proj, proj, x, mod, norm_g, b_proj, b_scale, w_out, ln_g, ln_b)


def _qkv_conv_kernel(cfg, tm, tn, x_ref, xp_ref, xn_ref, sc_ref, sh_ref, w_ref, cw_ref, o_ref, u_scr, y_scr):
    i, j = pl.program_id(0), pl.program_id(1)

    @pl.when(j == 0)
    def _():
        sc, sh = 1.0 + sc_ref[...], sh_ref[...]
        u_scr[0:HALO, :] = (xp_ref[...] * sc + sh).astype(BF16)
        u_scr[HALO:HALO + tm, :] = (x_ref[...] * sc + sh).astype(BF16)
        u_scr[HALO + tm:, :] = (xn_ref[...] * sc + sh).astype(BF16)

    w = w_ref[...]
    cw = cw_ref[...]
    is_q = j * tn < C_WIDTH
    is_qk = j * tn < 2 * C_WIDTH
    for r in range(tm // ROW_CHUNK):
        rows = slice(ROW_CHUNK * r, ROW_CHUNK * (r + 1))
        y = y_scr.at[r]
        y[...] = _dot(u_scr[ROW_CHUNK * r:ROW_CHUNK * (r + 1) + 2 * HALO, :], w)
        h = _silu(_conv3(y, cw, *cfg.chunk_edges(i * tm + ROW_CHUNK * r)))
        for c in range(tn // C_DK):
            hc = h[:, C_DK * c:C_DK * (c + 1)]
            inv = lax.rsqrt(jnp.sum(hc * hc, axis=-1, keepdims=True) + EPS)
            scale = jnp.where(is_qk, inv * jnp.where(is_q, C_DK ** -0.5, 1.0), 1.0)
            o_ref[rows, C_DK * c:C_DK * (c + 1)] = hc * scale


def _qkv_conv(cfg, x, mod, layer, w_in, conv_w, idx):
    tm, tn = cfg.tm, 256
    n = 3 * C_WIDTH
    xp, xn = _halo_specs(cfg, tm, D_MODEL)
    return pl.pallas_call(
        functools.partial(_qkv_conv_kernel, cfg, tm, tn),
        grid=(cfg.t_all // tm, n // tn),
        in_specs=[
            pl.BlockSpec((tm, D_MODEL), lambda i, j: (i, 0)),
            xp, xn,
            _mod_spec(cfg, layer, 1, tm),
            _mod_spec(cfg, layer, 0, tm),
            pl.BlockSpec((None, None, D_MODEL, tn), lambda i, j: (idx, j, 0, 0)),
            pl.BlockSpec((None, 3, tn), lambda i, j: (idx, 0, j)),
        ],
        out_specs=pl.BlockSpec((tm, tn), lambda i, j: (i, j)),
        out_shape=jax.ShapeDtypeStruct((cfg.t_all, n), F32),
        scratch_shapes=[pltpu.VMEM((tm + 2 * HALO, D_MODEL), BF16),
                        pltpu.VMEM((tm // ROW_CHUNK, ROW_CHUNK + 2 * HALO, tn), F32)],
        compiler_params=_cparams(("arbitrary", "arbitrary")),
        name=f"qkv_conv_l{layer}",
    )(x, x, x, mod, mod, w_in, conv_w)


DN_BLOCK = 64
DN_NBLK = SEQ_TILE // DN_BLOCK


def _tri_ones():
    n = SEQ_TILE
    idx = np.arange(n)
    lower = (idx[None, :] <= idx[:, None]).astype(np.float32)
    return jnp.asarray(np.stack([lower, lower.T]), BF16)


def _unit_tri_inverses(packs):
    n, w = DN_BLOCK, SEQ_TILE
    row = lax.broadcasted_iota(jnp.int32, (n, w), 0)
    col = jnp.bitwise_and(lax.broadcasted_iota(jnp.int32, (n, w), 1), n - 1)
    eye = (row == col).astype(F32)
    band = lax.broadcasted_iota(jnp.int32, (w, w), 0) // n
    keep = (band == lax.broadcasted_iota(jnp.int32, (w, w), 1) // n).astype(BF16)

    def block_diag(x):
        return jnp.concatenate([x.astype(BF16)] * DN_NBLK, axis=0) * keep

    ts = [eye - jnp.where((row >> 1) == (col >> 1), a, 0.0) for a in packs]
    for l in range(1, int(math.log2(n))):
        couple = jnp.logical_and((row >> (l + 1)) == (col >> (l + 1)), (row >> l) != (col >> l))
        ws = [_dot(jnp.where(couple, a, 0.0).astype(BF16), block_diag(t)) for a, t in zip(packs, ts)]
        ts = [t - _dot(t.astype(BF16), block_diag(wv)) for t, wv in zip(ts, ws)]
    return ts


def _delta_tile(dirs):
    n = SEQ_TILE
    row = lax.broadcasted_iota(jnp.int32, (n, n), 0)
    col = lax.broadcasted_iota(jnp.int32, (n, n), 1)
    probs = []
    for d, (q_ref, k_ref, v_ref, g, beta, tri_ref, s_scr, o_ref) in enumerate(dirs):
        incl = (col <= row) if d == 0 else (col >= row)
        strict = (col < row) if d == 0 else (col > row)
        last = n - 1 if d == 0 else 0
        g_hi, g_lo = _split_bf16(g)
        b_col = _dot(tri_ref[...], g_hi) + _dot(tri_ref[...], g_lo)
        b_row = b_col.T
        for h in range(C_HEADS):
            c = d * C_HEADS + h
            cols = slice(C_DK * h, C_DK * (h + 1))
            q, k, v = q_ref[:, cols], k_ref[:, cols], v_ref[:, cols]
            bc = b_col[:, c:c + 1]
            br = b_row[c:c + 1, :]
            b_last = b_col[last:last + 1, c:c + 1]
            ld = jnp.exp(jnp.where(incl, bc - br, -jnp.inf))
            bt = beta[:, 2 * C_HEADS + c:2 * C_HEADS + c + 1]
            kb = k * bt
            kbf = k.astype(BF16)
            e_col = jnp.exp(bc)
            probs.append(dict(
                d=d, h=h, cols=cols, s_scr=s_scr, o_ref=o_ref,
                a_mat=jnp.where(strict, _dot_nt(kb.astype(BF16), kbf) * ld, 0.0),
                aqk=(_dot_nt(q.astype(BF16), kbf) * ld).astype(BF16),
                rhs=jnp.concatenate([v * bt, kb * e_col], axis=1),
                qd=(q * e_col).astype(BF16),
                kd_t=(k * jnp.exp(b_last - bc)).T.astype(BF16),
                dec=jnp.exp(b_last)))

    blk = lambda b: slice(DN_BLOCK * b, DN_BLOCK * (b + 1))
    on_diag = (row // DN_BLOCK) == (col // DN_BLOCK)
    packs = []
    for p in probs:
        dm = jnp.where(on_diag, p["a_mat"], 0.0)
        packs.append(sum(dm[blk(b), :] for b in range(1, DN_NBLK)) + dm[blk(0), :])
    t_inv = _unit_tri_inverses(packs)

    xs = [[None] * DN_NBLK for _ in probs]
    for step in range(DN_NBLK):
        for pi, p in enumerate(probs):
            b = step if p["d"] == 0 else DN_NBLK - 1 - step
            resid = p["rhs"][blk(b), :]
            for bb in range(DN_NBLK):
                if xs[pi][bb] is not None:
                    resid = resid - _dot(p["a_mat"][blk(b), blk(bb)].astype(BF16), xs[pi][bb].astype(BF16))
            p["resid"] = resid
        for pi, p in enumerate(probs):
            b = step if p["d"] == 0 else DN_NBLK - 1 - step
            xs[pi][b] = _dot(t_inv[pi][:, blk(b)].astype(BF16), p["resid"].astype(BF16))

    for pi, p in enumerate(probs):
        x = jnp.concatenate(xs[pi], axis=0)
        p["u"], p["w"] = x[:, :C_DK], x[:, C_DK:].astype(BF16)
        p["s"] = p["s_scr"][p["h"]]
        p["sb"] = p["s"].astype(BF16)
    for p in probs:
        p["vb"] = (p["u"] - _dot(p["w"], p["sb"])).astype(BF16)
    for p in probs:
        p["o_ref"][:, p["cols"]] = _dot(p["qd"], p["sb"]) + _dot(p["aqk"], p["vb"])
    for p in probs:
        p["s_scr"][p["h"]] = p["s"] * p["dec"] + _dot(p["kd_t"], p["vb"])


def _delta_kernel(cfg, qf, kf, vf, abf, qb, kb, vb, abb, s0_ref, alog_ref, dtb_ref, tri_ref,
                  of_ref, ob_ref, st_ref, sf_scr, sb_scr):
    i = pl.program_id(0)

    _load_states(cfg, i, s0_ref, sf_scr, sb_scr)

    lane = lax.broadcasted_iota(jnp.int32, (1, 128), 1)
    neg_a = jnp.where(lane < 2 * C_HEADS, -jnp.exp(alog_ref[...]), 0.0)

    def gate(ab):
        a_in = ab + dtb_ref[...]
        softplus = jnp.maximum(a_in, 0.0) + jnp.log1p(jnp.exp(-jnp.abs(a_in)))
        return neg_a * softplus, jax.nn.sigmoid(ab)

    g_f, beta_f = gate(abf[...])
    g_b, beta_b = gate(abb[...])
    _delta_tile([(qf, kf, vf, g_f, beta_f, tri_ref.at[0], sf_scr, of_ref),
                 (qb, kb, vb, g_b, beta_b, tri_ref.at[1], sb_scr, ob_ref)])

    _store_states(cfg, i, st_ref, sf_scr, sb_scr)


def _delta(cfg, qkv, zab, s0, slot, a_log, dt_bias, tri):
    rev = cfg.rev_tile
    n = SEQ_TILE
    st_shape = (cfg.n_ctx, 2, C_HEADS, C_DK, C_DK)
    s0_spec, st_spec = _state_specs(cfg, slot, st_shape[1:])
    ab_col = C_WIDTH // 128
    in_specs = []
    for tile_of in (lambda i: i, rev):
        for cb in range(3):
            in_specs.append(pl.BlockSpec((n, C_WIDTH), functools.partial(lambda i, f, c: (f(i), c), f=tile_of, c=cb)))
        in_specs.append(pl.BlockSpec((n, 128), functools.partial(lambda i, f: (f(i), ab_col), f=tile_of)))
    in_specs += [
        s0_spec,
        pl.BlockSpec((1, 128), lambda i: (0, 0)),
        pl.BlockSpec((1, 128), lambda i: (0, 0)),
        pl.BlockSpec(tri.shape, lambda i: (0, 0, 0)),
    ]
    return pl.pallas_call(
        functools.partial(_delta_kernel, cfg),
        grid=(cfg.n_tiles,),
        in_specs=in_specs,
        out_specs=[
            pl.BlockSpec((n, C_WIDTH), lambda i: (i, 0)),
            pl.BlockSpec((n, C_WIDTH), lambda i: (rev(i), 0)),
            st_spec,
        ],
        out_shape=[
            jax.ShapeDtypeStruct((cfg.t_all, C_WIDTH), F32),
            jax.ShapeDtypeStruct((cfg.t_all, C_WIDTH), F32),
            jax.ShapeDtypeStruct(st_shape, F32),
        ],
        scratch_shapes=[pltpu.VMEM(st_shape[2:], F32), pltpu.VMEM(st_shape[2:], F32)],
        compiler_params=_cparams(("arbitrary",)),
        name="delta_scan",
    )(qkv, qkv, qkv, zab, qkv, qkv, qkv, zab, s0, a_log, dt_bias, tri)


def _out_odd_kernel(of_ref, ob_ref, z_ref, x_ref, g1_ref, norm_ref, wout_ref, lng_ref, lnb_ref, o_ref, w_scr):
    @pl.when(pl.program_id(0) == 0)
    def _():
        w_scr[...] = wout_ref[...].astype(BF16)

    o = of_ref[...] + ob_ref[...]
    z = z_ref[...]
    heads = _rms_heads(o, norm_ref[...], C_HEADS, C_DK)
    parts = [(heads[h] * _silu(z[:, C_DK * h:C_DK * (h + 1)])).astype(BF16) for h in range(C_HEADS)]
    y = _dot(jnp.concatenate(parts, axis=1), w_scr[...])
    o_ref[...] = _layernorm_rows(ALPHA * x_ref[...] + g1_ref[...] * y, lng_ref[...], lnb_ref[...])


def _out_odd(cfg, of, ob, zab, x, mod, layer, norm_g, w_out, ln_g, ln_b):
    n, e = SEQ_TILE, layer // 2
    return pl.pallas_call(
        _out_odd_kernel,
        grid=(cfg.n_tiles,),
        in_specs=[
            pl.BlockSpec((n, C_WIDTH), lambda i: (i, 0)),
            pl.BlockSpec((n, C_WIDTH), lambda i: (i, 0)),
            pl.BlockSpec((n, C_WIDTH), lambda i: (i, 0)),
            pl.BlockSpec((n, D_MODEL), lambda i: (i, 0)),
            _mod_spec(cfg, layer, 2, n),
            pl.BlockSpec((None, 1, C_DK), lambda i: (e, 0, 0)),
            pl.BlockSpec((None, D_MODEL, D_MODEL), lambda i: (e, 0, 0)),
            pl.BlockSpec((None, 1, D_MODEL), lambda i: (layer, 0, 0)),
            pl.BlockSpec((None, 1, D_MODEL), lambda i: (layer, 0, 0)),
        ],
        out_specs=pl.BlockSpec((n, D_MODEL), lambda i: (i, 0)),
        out_shape=jax.ShapeDtypeStruct((cfg.t_all, D_MODEL), F32),
        scratch_shapes=[pltpu.VMEM((D_MODEL, D_MODEL), BF16)],
        compiler_params=_cparams(("arbitrary",)),
        name=f"out_odd_l{layer}",
    )(of, ob, zab, x, mod, norm_g, w_out, ln_g, ln_b)


FFN_TF = 256


def _ffn_kernel(cfg, tm, split_out, x_ref, xp_ref, xn_ref, sc_ref, sh_ref, g2_ref, wa_ref, wg_ref, ca_ref, cg_ref,
                wd_ref, lng_ref, lnb_ref, *rest):
    out_refs, (u_scr, ya_scr, yg_scr, acc_scr) = rest[:-4], rest[-4:]
    i, j = pl.program_id(0), pl.program_id(1)

    @pl.when(j == 0)
    def _():
        sc, sh = 1.0 + sc_ref[...], sh_ref[...]
        u_scr[0:HALO, :] = (xp_ref[...] * sc + sh).astype(BF16)
        u_scr[HALO:HALO + tm, :] = (x_ref[...] * sc + sh).astype(BF16)
        u_scr[HALO + tm:, :] = (xn_ref[...] * sc + sh).astype(BF16)
        acc_scr[...] = jnp.zeros_like(acc_scr)

    wa, wg, wd = wa_ref[...], wg_ref[...], wd_ref[...]
    ca, cg = ca_ref[...], cg_ref[...]
    nchunk = tm // ROW_CHUNK

    def up(r):
        u = u_scr[ROW_CHUNK * r:ROW_CHUNK * (r + 1) + 2 * HALO, :]
        ya_scr[r] = _dot(u, wa)
        yg_scr[r] = _dot(u, wg)

    up(0)
    for r in range(nchunk):
        if r + 1 < nchunk:
            up(r + 1)
        edges = cfg.chunk_edges(i * tm + ROW_CHUNK * r)
        act = (_silu(_conv3(yg_scr.at[r], cg, *edges)) * _conv3(ya_scr.at[r], ca, *edges)).astype(BF16)
        acc_scr[ROW_CHUNK * r:ROW_CHUNK * (r + 1), :] += _dot(act, wd)

    @pl.when(j == pl.num_programs(1) - 1)
    def _():
        y = _layernorm_rows(ALPHA * x_ref[...] + g2_ref[...] * acc_scr[...], lng_ref[...], lnb_ref[...])
        if split_out:
            is_ctx = i * tm < cfg.t_ctx

            @pl.when(is_ctx)
            def _():
                out_refs[0][...] = y

            @pl.when(jnp.logical_not(is_ctx))
            def _():
                out_refs[1][...] = y
        else:
            out_refs[0][...] = y


def _ffn(cfg, x, mod, layer, w_up, conv_w, w_down, ln_g, ln_b, split_out=False):
    tm, tf = cfg.tm, FFN_TF
    nj = D_FF // tf
    n_ctx_tiles = cfg.t_ctx // tm
    xp, xn = _halo_specs(cfg, tm, D_MODEL)
    nchunk = tm // ROW_CHUNK
    if split_out:
        out_specs = [
            pl.BlockSpec((tm, D_MODEL), lambda i, j: (jnp.minimum(i, n_ctx_tiles - 1), 0)),
            pl.BlockSpec((tm, D_MODEL), lambda i, j: (jnp.maximum(i - n_ctx_tiles, 0), 0)),
        ]
        out_shape = [jax.ShapeDtypeStruct((cfg.t_ctx, D_MODEL), F32),
                     jax.ShapeDtypeStruct((cfg.t_all - cfg.t_ctx, D_MODEL), F32)]
    else:
        out_specs = pl.BlockSpec((tm, D_MODEL), lambda i, j: (i, 0))
        out_shape = jax.ShapeDtypeStruct((cfg.t_all, D_MODEL), F32)
    return pl.pallas_call(
        functools.partial(_ffn_kernel, cfg, tm, split_out),
        grid=(cfg.t_all // tm, nj),
        in_specs=[
            pl.BlockSpec((tm, D_MODEL), lambda i, j: (i, 0)),
            xp, xn,
            _mod_spec(cfg, layer, 4, tm),
            _mod_spec(cfg, layer, 3, tm),
            _mod_spec(cfg, layer, 5, tm),
            pl.BlockSpec((None, None, D_MODEL, tf), lambda i, j: (layer, j, 0, 0)),
            pl.BlockSpec((None, None, D_MODEL, tf), lambda i, j: (layer, nj + j, 0, 0)),
            pl.BlockSpec((None, 3, tf), lambda i, j: (layer, 0, j)),
            pl.BlockSpec((None, 3, tf), lambda i, j: (layer, 0, nj + j)),
            pl.BlockSpec((None, None, tf, D_MODEL), lambda i, j: (layer, 0, j, 0)),
            pl.BlockSpec((None, 1, D_MODEL), lambda i, j: (layer, 0, 0)),
            pl.BlockSpec((None, 1, D_MODEL), lambda i, j: (layer, 0, 0)),
        ],
        out_specs=out_specs,
        out_shape=out_shape,
        scratch_shapes=[
            pltpu.VMEM((tm + 2 * HALO, D_MODEL), BF16),
            pltpu.VMEM((nchunk, ROW_CHUNK + 2 * HALO, tf), F32),
            pltpu.VMEM((nchunk, ROW_CHUNK + 2 * HALO, tf), F32),
            pltpu.VMEM((tm, D_MODEL), F32),
        ],
        compiler_params=_cparams(("arbitrary", "arbitrary")),
        name=f"ffn_l{layer}",
    )(x, x, x, mod, mod, mod, w_up, w_up, conv_w, conv_w, w_down, ln_g, ln_b)


def _grid_pos_embed(n_tokens):
    rows = n_tokens // GRID_W
    r = jnp.broadcast_to(jnp.arange(rows, dtype=F32)[:, None], (rows, GRID_W)).reshape(-1)
    col = jnp.broadcast_to(jnp.arange(GRID_W, dtype=F32)[None, :], (rows, GRID_W)).reshape(-1)
    quarter = D_MODEL // 4
    freq = jnp.exp(-math.log(10000.0) * jnp.arange(quarter, dtype=F32) / quarter)
    ra, ca = r[:, None] * freq, col[:, None] * freq
    return jnp.concatenate([jnp.sin(ra), jnp.cos(ra), jnp.sin(ca), jnp.cos(ca)], -1)


def kernel(x_prompt, x_sample, state_gla, state_dn, c, c_ctx, w_mod, b_mod, ln1_g, ln1_b, ln2_g, ln2_b,
           a_w_in, a_w_gate, a_b_gate, a_norm, b_proj, b_scale, a_w_out,
           c_w_in, c_conv, c_a_log, c_dt_bias, c_norm, c_w_out, f_w_up, f_conv, f_w_down):
    n_ctx, seq, _ = x_prompt.shape
    n_smp, smp_len, _ = x_sample.shape
    assert seq == SEQ_TILE and n_smp + 1 <= MOD_ROWS
    cfg = _Cfg(n_ctx, n_smp, smp_len)

    x = _embed(cfg, x_prompt.reshape(-1, D_MODEL), x_sample.reshape(-1, D_MODEL), _grid_pos_embed(smp_len))
    cond = jnp.concatenate([c_ctx[None, :], c, jnp.zeros((MOD_ROWS - 1 - n_smp, D_MODEL), F32)], axis=0)
    mod = _modulation(cond, w_mod, b_mod)

    gla_consts = _gla_constants()
    tri = _tri_ones()
    row3 = lambda t: t.reshape(t.shape[0], 1, t.shape[1])
    ln1g, ln1b, ln2g, ln2b = row3(ln1_g), row3(ln1_b), row3(ln2_g), row3(ln2_b)
    s0_gla = state_gla.astype(F32).reshape(n_smp, -1, 2, A_HEADS // 2, 128, A_DV)
    s0_dn = state_dn.astype(F32)
    w_up_blk = _cast_blocks(f_w_up, FFN_TF, 2 * D_FF)
    w_down_blk = _cast_blocks(f_w_down, D_MODEL, D_MODEL)
    w_qkv_blk = _cast_blocks(c_w_in, 256, 3 * C_WIDTH)
    gla_states, dn_states = [], []
    for layer in range(DEPTH):
        if layer % 2 == 0:
            e = layer // 2
            w_in = a_w_in[e]
            w_main = jnp.concatenate([w_in[:, :2 * A_QK + 2 * A_WIDTH], w_in[:, -B_WIDTH:]], axis=1)
            lr0 = 2 * A_QK + 2 * A_WIDTH
            w_lr = jnp.pad(w_in[:, lr0:lr0 + 2 * A_RANK], ((0, 0), (0, 128 - 2 * A_RANK)))
            w_gate_bd = jnp.zeros((128, 2 * A_QK), F32)
            w_gate_bd = w_gate_bd.at[:A_RANK, :A_QK].set(a_w_gate[e, 0]).at[A_RANK:2 * A_RANK, A_QK:].set(a_w_gate[e, 1])
            proj = _proj(cfg, x, mod, layer, w_main, 512)
            gates = _gates(cfg, x, mod, layer, w_lr, w_gate_bd, a_b_gate[e].reshape(1, 2 * A_QK))
            of, ob, st = _gla(cfg, proj, gates, s0_gla, e, gla_consts)
            gla_states.append(st.reshape(n_ctx, 2, A_HEADS, A_DK, A_DV))
            x = _out_even(cfg, of, ob, proj, x, mod, layer, row3(a_norm), b_proj, row3(b_scale), a_w_out, ln1g, ln1b)
        else:
            o_ = layer // 2
            w_zab = jnp.pad(c_w_in[o_][:, 3 * C_WIDTH:], ((0, 0), (0, 128 - 4 * C_HEADS)))
            qkv = _qkv_conv(cfg, x, mod, layer, w_qkv_blk, c_conv, o_)
            zab = _proj(cfg, x, mod, layer, w_zab, 384)
            pad16 = lambda t: jnp.pad(t.reshape(1, -1), ((0, 0), (0, 128 - 2 * C_HEADS)))
            of, ob, st = _delta(cfg, qkv, zab, s0_dn, o_, pad16(c_a_log[o_]), pad16(c_dt_bias[o_]), tri)
            dn_states.append(st)
            x = _out_odd(cfg, of, ob, zab, x, mod, layer, row3(c_norm), c_w_out, ln1g, ln1b)
        x = _ffn(cfg, x, mod, layer, w_up_blk, f_conv, w_down_blk, ln2g, ln2b, split_out=layer == DEPTH - 1)

    y_prompt = x[0].reshape(n_ctx, seq, D_MODEL)
    y_sample = x[1].reshape(n_smp, smp_len, D_MODEL)
    return (y_prompt, y_sample, jnp.stack(gla_states, axis=1).astype(x_prompt.dtype),
            jnp.stack(dn_states, axis=1).astype(x_prompt.dtype))
```

```python
import functools
import math

import numpy as np
import jax
import jax.numpy as jnp
from jax import lax
from jax.experimental import pallas as pl
from jax.experimental.pallas import tpu as pltpu

F32 = jnp.float32
BF16 = jnp.bfloat16

D_MODEL = 1024
DEPTH = 4
GRID_W = 64
A_HEADS = 4
A_DK = 64
A_DV = 128
A_QK = A_HEADS * A_DK
A_WIDTH = A_HEADS * A_DV
A_RANK = 16
A_GATE_NORM = 16.0
B_WIDTH = 512
POOL_WINDOWS = (2, 4, 8, 16)
B_GW = 128
C_HEADS = 8
C_DK = 128
C_WIDTH = 1024
D_FF = 2816
N_MOD = 6
ALPHA = (2 * DEPTH) ** 0.25
EPS = 1e-6

SEQ_TILE = 256
HALO = 8
ROW_CHUNK = 256
assert ROW_CHUNK == SEQ_TILE
MOD_ROWS = 8
V7X_VMEM_LIMIT = 56 * 1024 * 1024


def _cparams(sem):
    return pltpu.CompilerParams(dimension_semantics=sem, vmem_limit_bytes=V7X_VMEM_LIMIT)


class _Cfg:
    def __init__(self, n_ctx, n_smp, smp_len):
        assert smp_len % SEQ_TILE == 0 and smp_len & (smp_len - 1) == 0
        self.n_ctx, self.n_smp, self.smp_len = n_ctx, n_smp, smp_len
        self.t_ctx = n_ctx * SEQ_TILE
        self.t_all = self.t_ctx + n_smp * smp_len
        self.tps = smp_len // SEQ_TILE
        self.n_tiles = self.t_all // SEQ_TILE
        self.n_seq = n_ctx + n_smp
        tm = 1024
        while self.t_ctx % tm or smp_len % tm:
            tm //= 2
        self.tm = tm

    def cond_of_row(self, g):
        return jnp.where(g < self.t_ctx, 0, 1 + (g - self.t_ctx) // self.smp_len)

    def seq_of_tile(self, i):
        return jnp.where(i < self.n_ctx, i, self.n_ctx + (i - self.n_ctx) // self.tps)

    def rev_tile(self, i):
        m = (i - self.n_ctx) % self.tps
        return jnp.where(i < self.n_ctx, i, i - m + (self.tps - 1 - m))

    def chunk_edges(self, row0):
        is_ctx = row0 < self.t_ctx
        length = jnp.where(is_ctx, SEQ_TILE, self.smp_len)
        pos0 = jnp.bitwise_and(jnp.where(is_ctx, row0, row0 - self.t_ctx), length - 1)
        return pos0 == 0, pos0 + ROW_CHUNK == length

    def first_of_seq(self, i):
        return jnp.logical_or(i < self.n_ctx, (i - self.n_ctx) % self.tps == 0)

    def last_of_seq(self, i):
        return jnp.logical_or(i < self.n_ctx, (i - self.n_ctx) % self.tps == self.tps - 1)


def _mod_map(cfg, layer, which, rows):
    def index_map(i, *_):
        return ((layer * MOD_ROWS + cfg.cond_of_row(i * rows)) * N_MOD + which, 0, 0)
    return index_map


def _mod_spec(cfg, layer, which, rows):
    return pl.BlockSpec((None, 1, D_MODEL), _mod_map(cfg, layer, which, rows))


def _halo_specs(cfg, rows, width, col_block=0):
    per = rows // HALO
    last = cfg.t_all // HALO - 1
    prev = pl.BlockSpec((HALO, width), lambda i, *_: (jnp.maximum(i * per - 1, 0), col_block))
    nxt = pl.BlockSpec((HALO, width), lambda i, *_: (jnp.minimum((i + 1) * per, last), col_block))
    return prev, nxt


def _seq_edges(cfg, tile, rows):
    g = tile * rows + lax.broadcasted_iota(jnp.int32, (rows, 1), 0)
    is_ctx = g < cfg.t_ctx
    length = jnp.where(is_ctx, SEQ_TILE, cfg.smp_len)
    pos = jnp.bitwise_and(jnp.where(is_ctx, g, g - cfg.t_ctx), length - 1)
    return pos, length


def _silu(x):
    return x * jax.nn.sigmoid(x)


def _dot(a, b):
    return jnp.dot(a, b, preferred_element_type=F32)


def _dot_nt(a, b):
    return lax.dot_general(a, b, (((1,), (1,)), ((), ())), preferred_element_type=F32)


def _split_bf16(x):
    hi = x.astype(BF16)
    lo = (x - hi.astype(F32)).astype(BF16)
    return hi, lo


def _conv3(y_ref, cw, starts_seq, ends_seq):
    n = ROW_CHUNK
    y_ref[HALO - 1:HALO, :] = jnp.where(starts_seq, 0.0, y_ref[HALO - 1:HALO, :])
    y_ref[HALO + n:HALO + n + 1, :] = jnp.where(ends_seq, 0.0, y_ref[HALO + n:HALO + n + 1, :])
    return (y_ref[HALO - 1:HALO - 1 + n, :] * cw[0:1, :]
            + y_ref[HALO:HALO + n, :] * cw[1:2, :]
            + y_ref[HALO + 1:HALO + 1 + n, :] * cw[2:3, :])


def _layernorm_rows(x, g, b):
    mu = jnp.mean(x, axis=-1, keepdims=True)
    xc = x - mu
    var = jnp.mean(xc * xc, axis=-1, keepdims=True)
    return xc * lax.rsqrt(var + EPS) * g + b


def _state_specs(cfg, slot, per_seq):
    zeros = (0,) * len(per_seq)
    n_ctx = cfg.n_ctx
    in_spec = pl.BlockSpec((None, None) + per_seq,
                           lambda i: (jnp.maximum(cfg.seq_of_tile(i) - n_ctx, 0), slot) + zeros)
    out_spec = pl.BlockSpec((None,) + per_seq, lambda i: (jnp.minimum(cfg.seq_of_tile(i), n_ctx - 1),) + zeros)
    return in_spec, out_spec


def _load_states(cfg, i, s0_ref, sf_scr, sb_scr):
    @pl.when(cfg.first_of_seq(i))
    def _():
        is_ctx = i < cfg.n_ctx
        sf_scr[...] = jnp.where(is_ctx, 0.0, s0_ref[0])
        sb_scr[...] = jnp.where(is_ctx, 0.0, s0_ref[1])


def _store_states(cfg, i, st_ref, sf_scr, sb_scr):
    @pl.when(i < cfg.n_ctx)
    def _():
        st_ref[0] = sf_scr[...]
        st_ref[1] = sb_scr[...]


def _embed_kernel(cfg, xp_ref, xs_ref, pe_ref, o_ref):
    i = pl.program_id(0)

    @pl.when(i < cfg.n_ctx)
    def _():
        o_ref[...] = xp_ref[...]

    @pl.when(i >= cfg.n_ctx)
    def _():
        o_ref[...] = xs_ref[...] + pe_ref[...]


def _embed(cfg, xp, xs, pe):
    n_ctx, tps = cfg.n_ctx, cfg.tps
    return pl.pallas_call(
        functools.partial(_embed_kernel, cfg),
        grid=(cfg.n_tiles,),
        in_specs=[
            pl.BlockSpec((SEQ_TILE, D_MODEL), lambda i: (jnp.minimum(i, n_ctx - 1), 0)),
            pl.BlockSpec((SEQ_TILE, D_MODEL), lambda i: (jnp.maximum(i - n_ctx, 0), 0)),
            pl.BlockSpec((SEQ_TILE, D_MODEL), lambda i: (jnp.maximum(i - n_ctx, 0) % tps, 0)),
        ],
        out_specs=pl.BlockSpec((SEQ_TILE, D_MODEL), lambda i: (i, 0)),
        out_shape=jax.ShapeDtypeStruct((cfg.t_all, D_MODEL), F32),
        compiler_params=_cparams(("arbitrary",)),
        name="embed",
    )(xp, xs, pe)


CAST_ROWS = 256


def _cast_blocks_kernel(tn, nblk, col0, w_ref, o_ref):
    for c in range(nblk):
        o_ref[c] = w_ref[:, col0 + tn * c:col0 + tn * (c + 1)].astype(BF16)


def _cast_blocks(w, tn, ncols, col0=0):
    n_l, k, n = w.shape
    nblk = ncols // tn
    return pl.pallas_call(
        functools.partial(_cast_blocks_kernel, tn, nblk, col0),
        grid=(n_l, k // CAST_ROWS),
        in_specs=[pl.BlockSpec((None, CAST_ROWS, n), lambda l, r: (l, r, 0))],
        out_specs=pl.BlockSpec((None, nblk, CAST_ROWS, tn), lambda l, r: (l, 0, r, 0)),
        out_shape=jax.ShapeDtypeStruct((n_l, nblk, k, tn), BF16),
        compiler_params=_cparams(("arbitrary", "arbitrary")),
        name=f"cast_blocks_{n}",
    )(w)


def _mod_kernel(c_ref, w_ref, b_ref, o_ref):
    sc = _silu(c_ref[...]).astype(BF16)
    o_ref[...] = _dot(sc, w_ref[...].astype(BF16)) + b_ref[...]


def _modulation(cond, w_mod, b_mod):
    out = pl.pallas_call(
        _mod_kernel,
        grid=(DEPTH, N_MOD),
        in_specs=[
            pl.BlockSpec((MOD_ROWS, D_MODEL), lambda l, j: (0, 0)),
            pl.BlockSpec((None, D_MODEL, D_MODEL), lambda l, j: (l, 0, j)),
            pl.BlockSpec((None, 1, D_MODEL), lambda l, j: (l, 0, j)),
        ],
        out_specs=pl.BlockSpec((None, MOD_ROWS, D_MODEL), lambda l, j: (l, 0, j)),
        out_shape=jax.ShapeDtypeStruct((DEPTH, MOD_ROWS, N_MOD * D_MODEL), F32),
        compiler_params=_cparams(("arbitrary", "arbitrary")),
        name="modulation",
    )(cond, w_mod, b_mod.reshape(DEPTH, 1, N_MOD * D_MODEL))
    return out.reshape(DEPTH * MOD_ROWS * N_MOD, 1, D_MODEL)


def _proj_kernel(x_ref, sc_ref, sh_ref, w_ref, o_ref, u_scr):
    @pl.when(pl.program_id(1) == 0)
    def _():
        u_scr[...] = (x_ref[...] * (1.0 + sc_ref[...]) + sh_ref[...]).astype(BF16)

    o_ref[...] = _dot(u_scr[...], w_ref[...].astype(BF16))


def _proj(cfg, x, mod, layer, w, tn):
    n = w.shape[1]
    tm = cfg.tm
    return pl.pallas_call(
        _proj_kernel,
        grid=(cfg.t_all // tm, n // tn),
        in_specs=[
            pl.BlockSpec((tm, D_MODEL), lambda i, j: (i, 0)),
            _mod_spec(cfg, layer, 1, tm),
            _mod_spec(cfg, layer, 0, tm),
            pl.BlockSpec((D_MODEL, tn), lambda i, j: (0, j)),
        ],
        out_specs=pl.BlockSpec((tm, tn), lambda i, j: (i, j)),
        out_shape=jax.ShapeDtypeStruct((cfg.t_all, n), F32),
        scratch_shapes=[pltpu.VMEM((tm, D_MODEL), BF16)],
        compiler_params=_cparams(("arbitrary", "arbitrary")),
        name=f"proj_l{layer}",
    )(x, mod, mod, w)


def _gate_kernel(x_ref, sc_ref, sh_ref, wlr_ref, wg_ref, bg_ref, o_ref):
    u = (x_ref[...] * (1.0 + sc_ref[...]) + sh_ref[...]).astype(BF16)
    lr = _dot(u, wlr_ref[...].astype(BF16))
    z = _dot(lr.astype(BF16), wg_ref[...].astype(BF16)) + bg_ref[...]
    log_sig = jnp.minimum(z, 0.0) - jnp.log1p(jnp.exp(-jnp.abs(z)))
    o_ref[...] = log_sig / A_GATE_NORM


def _gates(cfg, x, mod, layer, w_lr, w_gate_bd, b_gate):
    tm = cfg.tm
    return pl.pallas_call(
        _gate_kernel,
        grid=(cfg.t_all // tm,),
        in_specs=[
            pl.BlockSpec((tm, D_MODEL), lambda i: (i, 0)),
            _mod_spec(cfg, layer, 1, tm),
            _mod_spec(cfg, layer, 0, tm),
            pl.BlockSpec((D_MODEL, 128), lambda i: (0, 0)),
            pl.BlockSpec((128, 2 * A_QK), lambda i: (0, 0)),
            pl.BlockSpec((1, 2 * A_QK), lambda i: (0, 0)),
        ],
        out_specs=pl.BlockSpec((tm, 2 * A_QK), lambda i: (i, 0)),
        out_shape=jax.ShapeDtypeStruct((cfg.t_all, 2 * A_QK), F32),
        compiler_params=_cparams(("arbitrary",)),
        name=f"gates_l{layer}",
    )(x, mod, mod, w_lr, w_gate_bd, b_gate)


GLA_LEVELS = 8
GLA_STACK = GLA_LEVELS + 2


def _gla_constants():
    n = SEQ_TILE
    idx = np.arange(n)
    i, m = idx[:, None], idx[None, :]
    stack = np.zeros((2, GLA_STACK, n, n), np.float32)
    masks = np.zeros((2, GLA_LEVELS + 1, n, n), np.float32)
    stack[0, 0] = m <= i
    stack[0, 1] = m > i
    stack[1, 0] = m >= i
    stack[1, 1] = m < i
    for l in range(GLA_LEVELS):
        s = 1 << l
        blk = idx // (2 * s)
        upper = (idx // s) % 2 == 1
        piv_f = blk * 2 * s + s - 1
        piv_b = blk * 2 * s + s
        up, pf, pb = upper[:, None], piv_f[:, None], piv_b[:, None]
        stack[0, 2 + l] = np.where(up, (m > pf) & (m <= i), (m > i) & (m <= pf))
        stack[1, 2 + l] = np.where(up, (m >= pb) & (m < i), (m >= i) & (m < pb))
        same = blk[:, None] == blk[None, :]
        masks[0, l] = same & upper[:, None] & ~upper[None, :]
        masks[1, l] = same & ~upper[:, None] & upper[None, :]
    masks[:, GLA_LEVELS] = np.eye(n)
    return (jnp.asarray(stack.reshape(2, GLA_STACK * n, n), BF16), jnp.asarray(masks, F32))


def _gla_direction(qk, v, g, stack_ref, mask_ref, s_scr, o_ref):
    n = SEQ_TILE
    q = qk[:, :A_QK] * (A_DK ** -0.5)
    k = qk[:, A_QK:]
    g_hi, g_lo = _split_bf16(g)
    sums = _dot(stack_ref[...], g_hi) + _dot(stack_ref[...], g_lo)
    e_in = jnp.exp(sums[0:n])
    e_out = jnp.exp(sums[n:2 * n])
    lane = lax.broadcasted_iota(jnp.int32, (n, 128), 1)
    low_half = lane < A_DK

    att = [jnp.zeros((n, n), F32) for _ in range(A_HEADS)]
    for l in range(GLA_LEVELS + 1):
        if l < GLA_LEVELS:
            e = jnp.exp(sums[(2 + l) * n:(3 + l) * n])
            qs, ks = q * e, k * e
        else:
            qs, ks = q, k
        m = mask_ref[l]
        for p in range(A_HEADS // 2):
            qp = qs[:, 128 * p:128 * (p + 1)]
            kp = ks[:, 128 * p:128 * (p + 1)].astype(BF16)
            for hh in range(2):
                qh = jnp.where(low_half if hh == 0 else ~low_half, qp, 0.0).astype(BF16)
                att[2 * p + hh] = att[2 * p + hh] + m * _dot_nt(qh, kp)

    qd = q * e_in
    kd = k * e_out
    kd_t = kd.T.astype(BF16)
    g_t = g.T
    gt_hi, gt_lo = _split_bf16(g_t)
    ones = jnp.ones((n, 128), BF16)
    dec = jnp.exp(_dot(gt_hi, ones) + _dot(gt_lo, ones))
    for h in range(A_HEADS):
        p, hh = h // 2, h % 2
        v_h = v[:, A_DV * h:A_DV * (h + 1)].astype(BF16)
        s_pair = s_scr[p]
        qp = qd[:, 128 * p:128 * (p + 1)]
        qh = jnp.where(low_half if hh == 0 else ~low_half, qp, 0.0).astype(BF16)
        o_h = _dot(att[h].astype(BF16), v_h) + _dot(qh, s_pair.astype(BF16))
        o_ref[:, A_DV * h:A_DV * (h + 1)] = o_h
    for h in range(A_HEADS):
        p, hh = h // 2, h % 2
        v_h = v[:, A_DV * h:A_DV * (h + 1)].astype(BF16)
        rows = slice(A_DK * hh, A_DK * (hh + 1))
        s_old = s_scr[p, rows, :]
        s_scr[p, rows, :] = s_old * dec[A_DK * h:A_DK * (h + 1), :] + _dot(kd_t[A_DK * h:A_DK * (h + 1), :], v_h)


def _gla_kernel(cfg, qk_f, v_f, g_f, qk_b, v_b, g_b, s0_ref, stack_ref, mask_ref,
                of_ref, ob_ref, st_ref, sf_scr, sb_scr):
    i = pl.program_id(0)

    _load_states(cfg, i, s0_ref, sf_scr, sb_scr)

    _gla_direction(qk_f[...], v_f[...], g_f[...], stack_ref.at[0], mask_ref.at[0], sf_scr, of_ref)
    _gla_direction(qk_b[...], v_b[...], g_b[...], stack_ref.at[1], mask_ref.at[1], sb_scr, ob_ref)

    _store_states(cfg, i, st_ref, sf_scr, sb_scr)


def _gla(cfg, proj, gates, s0, slot, consts):
    stack, masks = consts
    rev = cfg.rev_tile
    n = SEQ_TILE
    st_shape = (cfg.n_ctx, 2, A_HEADS // 2, 128, A_DV)
    s0_spec, st_spec = _state_specs(cfg, slot, st_shape[1:])
    return pl.pallas_call(
        functools.partial(_gla_kernel, cfg),
        grid=(cfg.n_tiles,),
        in_specs=[
            pl.BlockSpec((n, 2 * A_QK), lambda i: (i, 0)),
            pl.BlockSpec((n, A_WIDTH), lambda i: (i, 1)),
            pl.BlockSpec((n, A_QK), lambda i: (i, 0)),
            pl.BlockSpec((n, 2 * A_QK), lambda i: (rev(i), 0)),
            pl.BlockSpec((n, A_WIDTH), lambda i: (rev(i), 1)),
            pl.BlockSpec((n, A_QK), lambda i: (rev(i), 1)),
            s0_spec,
            pl.BlockSpec(stack.shape, lambda i: (0, 0, 0)),
            pl.BlockSpec(masks.shape, lambda i: (0, 0, 0, 0)),
        ],
        out_specs=[
            pl.BlockSpec((n, A_WIDTH), lambda i: (i, 0)),
            pl.BlockSpec((n, A_WIDTH), lambda i: (rev(i), 0)),
            st_spec,
        ],
        out_shape=[
            jax.ShapeDtypeStruct((cfg.t_all, A_WIDTH), F32),
            jax.ShapeDtypeStruct((cfg.t_all, A_WIDTH), F32),
            jax.ShapeDtypeStruct(st_shape, F32),
        ],
        scratch_shapes=[pltpu.VMEM(st_shape[2:], F32), pltpu.VMEM(st_shape[2:], F32)],
        compiler_params=_cparams(("arbitrary",)),
        name="gla_scan",
    )(proj, proj, gates, proj, proj, gates, s0, stack, masks)


def _rms_heads(o, gain, n_heads, width):
    parts = []
    for h in range(n_heads):
        oh = o[:, width * h:width * (h + 1)]
        parts.append(oh * lax.rsqrt(jnp.mean(oh * oh, axis=-1, keepdims=True) + EPS) * gain)
    return parts


def _out_even_kernel(cfg, of_ref, ob_ref, r_ref, pz_ref, pzp_ref, pzn_ref, x_ref, g1_ref,
                     norm_ref, bproj_ref, bscale_ref, wout_ref, lng_ref, lnb_ref, o_ref, ext_scr, w_scr):
    n = SEQ_TILE
    i = pl.program_id(0)

    @pl.when(i == 0)
    def _():
        w_scr[...] = wout_ref[...].astype(BF16)

    o = of_ref[...] + ob_ref[...]
    r = r_ref[...]
    heads = _rms_heads(o, norm_ref[...], A_HEADS, A_DV)
    parts = [(heads[h] * _silu(r[:, A_DV * h:A_DV * (h + 1)])).astype(BF16) for h in range(A_HEADS)]

    pos, length = _seq_edges(cfg, i, n)
    starts_seq, ends_seq = cfg.chunk_edges(i * n)
    ext_scr[0:HALO, :] = jnp.where(starts_seq, 0.0, pzp_ref[...])
    ext_scr[HALO:HALO + n, :] = pz_ref[...]
    ext_scr[HALO + n:, :] = jnp.where(ends_seq, 0.0, pzn_ref[...])
    for gi, win in enumerate(POOL_WINDOWS):
        lo = win // 2
        hi = win - 1 - lo
        cols = slice(B_GW * gi, B_GW * (gi + 1))
        acc = ext_scr[HALO - lo:HALO - lo + n, cols]
        for d in range(-lo + 1, hi + 1):
            acc = acc + ext_scr[HALO + d:HALO + d + n, cols]
        cnt = (jnp.minimum(pos + hi + 1, length) - jnp.maximum(pos - lo, 0)).astype(F32)
        pooled = (acc / cnt - ext_scr[HALO:HALO + n, cols]).astype(BF16)
        mixed = _dot(pooled, bproj_ref[gi].astype(BF16)) * bscale_ref[:, cols]
        parts.append(mixed.astype(BF16))

    y = _dot(jnp.concatenate(parts, axis=1), w_scr[...])
    o_ref[...] = _layernorm_rows(ALPHA * x_ref[...] + g1_ref[...] * y, lng_ref[...], lnb_ref[...])


def _out_even(cfg, of, ob, proj, x, mod, layer, norm_g, b_proj, b_scale, w_out, ln_g, ln_b):
    n, e = SEQ_TILE, layer // 2
    pz_prev, pz_next = _halo_specs(cfg, n, B_WIDTH, col_block=3)
    return pl.pallas_call(
        functools.partial(_out_even_kernel, cfg),
        grid=(cfg.n_tiles,),
        in_specs=[
            pl.BlockSpec((n, A_WIDTH), lambda i: (i, 0)),
            pl.BlockSpec((n, A_WIDTH), lambda i: (i, 0)),
            pl.BlockSpec((n, A_WIDTH), lambda i: (i, 2)),
            pl.BlockSpec((n, B_WIDTH), lambda i: (i, 3)),
            pz_prev, pz_next,
            pl.BlockSpec((n, D_MODEL), lambda i: (i, 0)),
            _mod_spec(cfg, layer, 2, n),
            pl.BlockSpec((None, 1, A_DV), lambda i: (e, 0, 0)),
            pl.BlockSpec((None, len(POOL_WINDOWS), B_GW, B_GW), lambda i: (e, 0, 0, 0)),
            pl.BlockSpec((None, 1, B_WIDTH), lambda i: (e, 0, 0)),
            pl.BlockSpec((None, D_MODEL, D_MODEL), lambda i: (e, 0, 0)),
            pl.BlockSpec((None, 1, D_MODEL), lambda i: (layer, 0, 0)),
            pl.BlockSpec((None, 1, D_MODEL), lambda i: (layer, 0, 0)),
        ],
        out_specs=pl.BlockSpec((n, D_MODEL), lambda i: (i, 0)),
        out_shape=jax.ShapeDtypeStruct((cfg.t_all, D_MODEL), F32),
        scratch_shapes=[pltpu.VMEM((n + 2 * HALO, B_WIDTH), F32), pltpu.VMEM((D_MODEL, D_MODEL), BF16)],
        compiler_params=_cparams(("arbitrary",)),
        name=f"out_even_l{layer}",
    )(of, ob, proj, proj, proj, proj, x, mod, norm_g, b_proj, b_scale, w_out, ln_g, ln_b)


def _qkv_conv_kernel(cfg, tm, tn, x_ref, xp_ref, xn_ref, sc_ref, sh_ref, w_ref, cw_ref, o_ref, u_scr, y_scr):
    i, j = pl.program_id(0), pl.program_id(1)

    @pl.when(j == 0)
    def _():
        sc, sh = 1.0 + sc_ref[...], sh_ref[...]
        u_scr[0:HALO, :] = (xp_ref[...] * sc + sh).astype(BF16)
        u_scr[HALO:HALO + tm, :] = (x_ref[...] * sc + sh).astype(BF16)
        u_scr[HALO + tm:, :] = (xn_ref[...] * sc + sh).astype(BF16)

    w = w_ref[...].astype(BF16)
    cw = cw_ref[...]
    is_q = j == 0
    is_qk = j < 2
    nchunk = tm // ROW_CHUNK

    def up(r):
        y_scr[r] = _dot(u_scr[ROW_CHUNK * r:ROW_CHUNK * (r + 1) + 2 * HALO, :], w)

    up(0)
    for r in range(nchunk):
        if r + 1 < nchunk:
            up(r + 1)
        rows = slice(ROW_CHUNK * r, ROW_CHUNK * (r + 1))
        h = _silu(_conv3(y_scr.at[r], cw, *cfg.chunk_edges(i * tm + ROW_CHUNK * r)))
        for c in range(tn // C_DK):
            hc = h[:, C_DK * c:C_DK * (c + 1)]
            inv = lax.rsqrt(jnp.sum(hc * hc, axis=-1, keepdims=True) + EPS)
            scale = jnp.where(is_qk, inv * jnp.where(is_q, C_DK ** -0.5, 1.0), 1.0)
            o_ref[rows, C_DK * c:C_DK * (c + 1)] = hc * scale


def _qkv_conv(cfg, x, mod, layer, w_in, conv_w, idx):
    tm, tn = cfg.tm, C_WIDTH
    n = 3 * C_WIDTH
    xp, xn = _halo_specs(cfg, tm, D_MODEL)
    return pl.pallas_call(
        functools.partial(_qkv_conv_kernel, cfg, tm, tn),
        grid=(cfg.t_all // tm, n // tn),
        in_specs=[
            pl.BlockSpec((tm, D_MODEL), lambda i, j: (i, 0)),
            xp, xn,
            _mod_spec(cfg, layer, 1, tm),
            _mod_spec(cfg, layer, 0, tm),
            pl.BlockSpec((None, D_MODEL, tn), lambda i, j: (idx, 0, j)),
            pl.BlockSpec((None, 3, tn), lambda i, j: (idx, 0, j)),
        ],
        out_specs=pl.BlockSpec((tm, tn), lambda i, j: (i, j)),
        out_shape=jax.ShapeDtypeStruct((cfg.t_all, n), F32),
        scratch_shapes=[pltpu.VMEM((tm + 2 * HALO, D_MODEL), BF16),
                        pltpu.VMEM((tm // ROW_CHUNK, ROW_CHUNK + 2 * HALO, tn), F32)],
        compiler_params=_cparams(("arbitrary", "arbitrary")),
        name=f"qkv_conv_l{layer}",
    )(x, x, x, mod, mod, w_in, conv_w)


DN_BLOCK = 64
DN_NBLK = SEQ_TILE // DN_BLOCK


def _tri_ones():
    n = SEQ_TILE
    idx = np.arange(n)
    lower = (idx[None, :] <= idx[:, None]).astype(np.float32)
    return jnp.asarray(np.stack([lower, lower.T]), BF16)


def _unit_tri_inverses(packs):
    n, w = DN_BLOCK, SEQ_TILE
    row = lax.broadcasted_iota(jnp.int32, (n, w), 0)
    col = jnp.bitwise_and(lax.broadcasted_iota(jnp.int32, (n, w), 1), n - 1)
    eye = (row == col).astype(F32)
    band = lax.broadcasted_iota(jnp.int32, (w, w), 0) // n
    keep = (band == lax.broadcasted_iota(jnp.int32, (w, w), 1) // n).astype(BF16)

    def block_diag(x):
        return jnp.concatenate([x.astype(BF16)] * DN_NBLK, axis=0) * keep

    ts = [eye - jnp.where((row >> 1) == (col >> 1), a, 0.0) for a in packs]
    for l in range(1, int(math.log2(n))):
        couple = jnp.logical_and((row >> (l + 1)) == (col >> (l + 1)), (row >> l) != (col >> l))
        ws = [_dot(jnp.where(couple, a, 0.0).astype(BF16), block_diag(t)) for a, t in zip(packs, ts)]
        ts = [t - _dot(t.astype(BF16), block_diag(wv)) for t, wv in zip(ts, ws)]
    return ts


def _delta_tile(dirs):
    n = SEQ_TILE
    row = lax.broadcasted_iota(jnp.int32, (n, n), 0)
    col = lax.broadcasted_iota(jnp.int32, (n, n), 1)
    probs = []
    for d, (q_ref, k_ref, v_ref, g, beta, tri_ref, s_scr, o_ref) in enumerate(dirs):
        incl = (col <= row) if d == 0 else (col >= row)
        strict = (col < row) if d == 0 else (col > row)
        last = n - 1 if d == 0 else 0
        g_hi, g_lo = _split_bf16(g)
        b_col = _dot(tri_ref[...], g_hi) + _dot(tri_ref[...], g_lo)
        b_row = b_col.T
        for h in range(C_HEADS):
            c = d * C_HEADS + h
            cols = slice(C_DK * h, C_DK * (h + 1))
            q, k, v = q_ref[:, cols], k_ref[:, cols], v_ref[:, cols]
            bc = b_col[:, c:c + 1]
            br = b_row[c:c + 1, :]
            b_last = b_col[last:last + 1, c:c + 1]
            ld = jnp.exp(jnp.where(incl, bc - br, -jnp.inf))
            bt = beta[:, 2 * C_HEADS + c:2 * C_HEADS + c + 1]
            kb = k * bt
            kbf = k.astype(BF16)
            e_col = jnp.exp(bc)
            probs.append(dict(
                d=d, h=h, cols=cols, s_scr=s_scr, o_ref=o_ref,
                a_mat=jnp.where(strict, _dot_nt(kb.astype(BF16), kbf) * ld, 0.0),
                aqk=(_dot_nt(q.astype(BF16), kbf) * ld).astype(BF16),
                rhs=jnp.concatenate([v * bt, kb * e_col], axis=1),
                qd=(q * e_col).astype(BF16),
                kd_t=(k * jnp.exp(b_last - bc)).T.astype(BF16),
                dec=jnp.exp(b_last)))

    blk = lambda b: slice(DN_BLOCK * b, DN_BLOCK * (b + 1))
    on_diag = (row // DN_BLOCK) == (col // DN_BLOCK)
    packs = []
    for p in probs:
        dm = jnp.where(on_diag, p["a_mat"], 0.0)
        packs.append(sum(dm[blk(b), :] for b in range(1, DN_NBLK)) + dm[blk(0), :])
    t_inv = _unit_tri_inverses(packs)

    xs = [[None] * DN_NBLK for _ in probs]
    for step in range(DN_NBLK):
        for pi, p in enumerate(probs):
            b = step if p["d"] == 0 else DN_NBLK - 1 - step
            resid = p["rhs"][blk(b), :]
            for bb in range(DN_NBLK):
                if xs[pi][bb] is not None:
                    resid = resid - _dot(p["a_mat"][blk(b), blk(bb)].astype(BF16), xs[pi][bb].astype(BF16))
            p["resid"] = resid
        for pi, p in enumerate(probs):
            b = step if p["d"] == 0 else DN_NBLK - 1 - step
            xs[pi][b] = _dot(t_inv[pi][:, blk(b)].astype(BF16), p["resid"].astype(BF16))

    for pi, p in enumerate(probs):
        x = jnp.concatenate(xs[pi], axis=0)
        p["u"], p["w"] = x[:, :C_DK], x[:, C_DK:].astype(BF16)
        p["s"] = p["s_scr"][p["h"]]
        p["sb"] = p["s"].astype(BF16)
    for p in probs:
        p["vb"] = (p["u"] - _dot(p["w"], p["sb"])).astype(BF16)
    for p in probs:
        p["o_ref"][:, p["cols"]] = _dot(p["qd"], p["sb"]) + _dot(p["aqk"], p["vb"])
    for p in probs:
        p["s_scr"][p["h"]] = p["s"] * p["dec"] + _dot(p["kd_t"], p["vb"])


def _delta_kernel(cfg, qf, kf, vf, abf, qb, kb, vb, abb, s0_ref, alog_ref, dtb_ref, tri_ref,
                  of_ref, ob_ref, st_ref, sf_scr, sb_scr):
    i = pl.program_id(0)

    _load_states(cfg, i, s0_ref, sf_scr, sb_scr)

    lane = lax.broadcasted_iota(jnp.int32, (1, 128), 1)
    neg_a = jnp.where(lane < 2 * C_HEADS, -jnp.exp(alog_ref[...]), 0.0)

    def gate(ab):
        a_in = ab + dtb_ref[...]
        softplus = jnp.maximum(a_in, 0.0) + jnp.log1p(jnp.exp(-jnp.abs(a_in)))
        return neg_a * softplus, jax.nn.sigmoid(ab)

    g_f, beta_f = gate(abf[...])
    g_b, beta_b = gate(abb[...])
    _delta_tile([(qf, kf, vf, g_f, beta_f, tri_ref.at[0], sf_scr, of_ref),
                 (qb, kb, vb, g_b, beta_b, tri_ref.at[1], sb_scr, ob_ref)])

    _store_states(cfg, i, st_ref, sf_scr, sb_scr)


def _delta(cfg, qkv, zab, s0, slot, a_log, dt_bias, tri):
    rev = cfg.rev_tile
    n = SEQ_TILE
    st_shape = (cfg.n_ctx, 2, C_HEADS, C_DK, C_DK)
    s0_spec, st_spec = _state_specs(cfg, slot, st_shape[1:])
    ab_col = C_WIDTH // 128
    in_specs = []
    for tile_of in (lambda i: i, rev):
        for cb in range(3):
            in_specs.append(pl.BlockSpec((n, C_WIDTH), functools.partial(lambda i, f, c: (f(i), c), f=tile_of, c=cb)))
        in_specs.append(pl.BlockSpec((n, 128), functools.partial(lambda i, f: (f(i), ab_col), f=tile_of)))
    in_specs += [
        s0_spec,
        pl.BlockSpec((1, 128), lambda i: (0, 0)),
        pl.BlockSpec((1, 128), lambda i: (0, 0)),
        pl.BlockSpec(tri.shape, lambda i: (0, 0, 0)),
    ]
    return pl.pallas_call(
        functools.partial(_delta_kernel, cfg),
        grid=(cfg.n_tiles,),
        in_specs=in_specs,
        out_specs=[
            pl.BlockSpec((n, C_WIDTH), lambda i: (i, 0)),
            pl.BlockSpec((n, C_WIDTH), lambda i: (rev(i), 0)),
            st_spec,
        ],
        out_shape=[
            jax.ShapeDtypeStruct((cfg.t_all, C_WIDTH), F32),
            jax.ShapeDtypeStruct((cfg.t_all, C_WIDTH), F32),
            jax.ShapeDtypeStruct(st_shape, F32),
        ],
        scratch_shapes=[pltpu.VMEM(st_shape[2:], F32), pltpu.VMEM(st_shape[2:], F32)],
        compiler_params=_cparams(("arbitrary",)),
        name="delta_scan",
    )(qkv, qkv, qkv, zab, qkv, qkv, qkv, zab, s0, a_log, dt_bias, tri)


def _out_odd_kernel(of_ref, ob_ref, z_ref, x_ref, g1_ref, norm_ref, wout_ref, lng_ref, lnb_ref, o_ref, w_scr):
    @pl.when(pl.program_id(0) == 0)
    def _():
        w_scr[...] = wout_ref[...].astype(BF16)

    o = of_ref[...] + ob_ref[...]
    z = z_ref[...]
    heads = _rms_heads(o, norm_ref[...], C_HEADS, C_DK)
    parts = [(heads[h] * _silu(z[:, C_DK * h:C_DK * (h + 1)])).astype(BF16) for h in range(C_HEADS)]
    y = _dot(jnp.concatenate(parts, axis=1), w_scr[...])
    o_ref[...] = _layernorm_rows(ALPHA * x_ref[...] + g1_ref[...] * y, lng_ref[...], lnb_ref[...])


def _out_odd(cfg, of, ob, zab, x, mod, layer, norm_g, w_out, ln_g, ln_b):
    n, e = SEQ_TILE, layer // 2
    return pl.pallas_call(
        _out_odd_kernel,
        grid=(cfg.n_tiles,),
        in_specs=[
            pl.BlockSpec((n, C_WIDTH), lambda i: (i, 0)),
            pl.BlockSpec((n, C_WIDTH), lambda i: (i, 0)),
            pl.BlockSpec((n, C_WIDTH), lambda i: (i, 0)),
            pl.BlockSpec((n, D_MODEL), lambda i: (i, 0)),
            _mod_spec(cfg, layer, 2, n),
            pl.BlockSpec((None, 1, C_DK), lambda i: (e, 0, 0)),
            pl.BlockSpec((None, D_MODEL, D_MODEL), lambda i: (e, 0, 0)),
            pl.BlockSpec((None, 1, D_MODEL), lambda i: (layer, 0, 0)),
            pl.BlockSpec((None, 1, D_MODEL), lambda i: (layer, 0, 0)),
        ],
        out_specs=pl.BlockSpec((n, D_MODEL), lambda i: (i, 0)),
        out_shape=jax.ShapeDtypeStruct((cfg.t_all, D_MODEL), F32),
        scratch_shapes=[pltpu.VMEM((D_MODEL, D_MODEL), BF16)],
        compiler_params=_cparams(("arbitrary",)),
        name=f"out_odd_l{layer}",
    )(of, ob, zab, x, mod, norm_g, w_out, ln_g, ln_b)


FFN_TF = 256
FFN_NBLK = D_FF // FFN_TF
FFN_TM = 1024


def _ffn_kernel(cfg, tm, split_out, x_ref, xp_ref, xn_ref, sc_ref, sh_ref, g2_ref, wup_ref, cw_ref,
                wd_ref, lng_ref, lnb_ref, *rest):
    out_refs, (u_scr, ya_scr, yg_scr, acc_scr) = rest[:-4], rest[-4:]
    i = pl.program_id(0)
    sc, sh = 1.0 + sc_ref[...], sh_ref[...]
    u_scr[0:HALO, :] = (xp_ref[...] * sc + sh).astype(BF16)
    u_scr[HALO:HALO + tm, :] = (x_ref[...] * sc + sh).astype(BF16)
    u_scr[HALO + tm:, :] = (xn_ref[...] * sc + sh).astype(BF16)
    acc_scr[...] = jnp.zeros_like(acc_scr)
    nchunk = tm // ROW_CHUNK
    edges = [cfg.chunk_edges(i * tm + ROW_CHUNK * r) for r in range(nchunk)]

    def ff_block(jb, carry):
        wa, wg, wd = wup_ref[jb], wup_ref[FFN_NBLK + jb], wd_ref[jb]
        ca, cg = cw_ref[jb], cw_ref[FFN_NBLK + jb]

        def up(r):
            u = u_scr[ROW_CHUNK * r:ROW_CHUNK * (r + 1) + 2 * HALO, :]
            ya_scr[r] = _dot(u, wa)
            yg_scr[r] = _dot(u, wg)

        up(0)
        for r in range(nchunk):
            if r + 1 < nchunk:
                up(r + 1)
            act = (_silu(_conv3(yg_scr.at[r], cg, *edges[r])) * _conv3(ya_scr.at[r], ca, *edges[r])).astype(BF16)
            acc_scr[ROW_CHUNK * r:ROW_CHUNK * (r + 1), :] += _dot(act, wd)
        return carry

    lax.fori_loop(0, FFN_NBLK, ff_block, 0)

    y = _layernorm_rows(ALPHA * x_ref[...] + g2_ref[...] * acc_scr[...], lng_ref[...], lnb_ref[...])
    if split_out:
        is_ctx = i * tm < cfg.t_ctx

        @pl.when(is_ctx)
        def _():
            out_refs[0][...] = y

        @pl.when(jnp.logical_not(is_ctx))
        def _():
            out_refs[1][...] = y
    else:
        out_refs[0][...] = y


def _ffn(cfg, x, mod, layer, w_up, conv_w, w_down, ln_g, ln_b, split_out=False):
    tm = min(cfg.tm, FFN_TM)
    n_ctx_tiles = cfg.t_ctx // tm
    xp, xn = _halo_specs(cfg, tm, D_MODEL)
    nchunk = tm // ROW_CHUNK
    resident = pl.Buffered(1)
    if split_out:
        out_specs = [
            pl.BlockSpec((tm, D_MODEL), lambda i: (jnp.minimum(i, n_ctx_tiles - 1), 0)),
            pl.BlockSpec((tm, D_MODEL), lambda i: (jnp.maximum(i - n_ctx_tiles, 0), 0)),
        ]
        out_shape = [jax.ShapeDtypeStruct((cfg.t_ctx, D_MODEL), F32),
                     jax.ShapeDtypeStruct((cfg.t_all - cfg.t_ctx, D_MODEL), F32)]
    else:
        out_specs = pl.BlockSpec((tm, D_MODEL), lambda i: (i, 0))
        out_shape = jax.ShapeDtypeStruct((cfg.t_all, D_MODEL), F32)
    return pl.pallas_call(
        functools.partial(_ffn_kernel, cfg, tm, split_out),
        grid=(cfg.t_all // tm,),
        in_specs=[
            pl.BlockSpec((tm, D_MODEL), lambda i: (i, 0)),
            xp, xn,
            _mod_spec(cfg, layer, 4, tm),
            _mod_spec(cfg, layer, 3, tm),
            _mod_spec(cfg, layer, 5, tm),
            pl.BlockSpec((None, 2 * FFN_NBLK, D_MODEL, FFN_TF), lambda i: (layer, 0, 0, 0), pipeline_mode=resident),
            pl.BlockSpec((None, 2 * FFN_NBLK, 3, FFN_TF), lambda i: (layer, 0, 0, 0)),
            pl.BlockSpec((None, FFN_NBLK, FFN_TF, D_MODEL), lambda i: (layer, 0, 0, 0), pipeline_mode=resident),
            pl.BlockSpec((None, 1, D_MODEL), lambda i: (layer, 0, 0)),
            pl.BlockSpec((None, 1, D_MODEL), lambda i: (layer, 0, 0)),
        ],
        out_specs=out_specs,
        out_shape=out_shape,
        scratch_shapes=[
            pltpu.VMEM((tm + 2 * HALO, D_MODEL), BF16),
            pltpu.VMEM((nchunk, ROW_CHUNK + 2 * HALO, FFN_TF), F32),
            pltpu.VMEM((nchunk, ROW_CHUNK + 2 * HALO, FFN_TF), F32),
            pltpu.VMEM((tm, D_MODEL), F32),
        ],
        compiler_params=_cparams(("arbitrary",)),
        name=f"ffn_l{layer}",
    )(x, x, x, mod, mod, mod, w_up, conv_w, w_down, ln_g, ln_b)


def _grid_pos_embed(n_tokens):
    rows = n_tokens // GRID_W
    r = jnp.broadcast_to(jnp.arange(rows, dtype=F32)[:, None], (rows, GRID_W)).reshape(-1)
    col = jnp.broadcast_to(jnp.arange(GRID_W, dtype=F32)[None, :], (rows, GRID_W)).reshape(-1)
    quarter = D_MODEL // 4
    freq = jnp.exp(-math.log(10000.0) * jnp.arange(quarter, dtype=F32) / quarter)
    ra, ca = r[:, None] * freq, col[:, None] * freq
    return jnp.concatenate([jnp.sin(ra), jnp.cos(ra), jnp.sin(ca), jnp.cos(ca)], -1)


def kernel(x_prompt, x_sample, state_gla, state_dn, c, c_ctx, w_mod, b_mod, ln1_g, ln1_b, ln2_g, ln2_b,
           a_w_in, a_w_gate, a_b_gate, a_norm, b_proj, b_scale, a_w_out,
           c_w_in, c_conv, c_a_log, c_dt_bias, c_norm, c_w_out, f_w_up, f_conv, f_w_down):
    n_ctx, seq, _ = x_prompt.shape
    n_smp, smp_len, _ = x_sample.shape
    assert seq == SEQ_TILE and n_smp + 1 <= MOD_ROWS
    cfg = _Cfg(n_ctx, n_smp, smp_len)

    x = _embed(cfg, x_prompt.reshape(-1, D_MODEL), x_sample.reshape(-1, D_MODEL), _grid_pos_embed(smp_len))
    cond = jnp.concatenate([c_ctx[None, :], c, jnp.zeros((MOD_ROWS - 1 - n_smp, D_MODEL), F32)], axis=0)
    mod = _modulation(cond, w_mod, b_mod)

    gla_consts = _gla_constants()
    tri = _tri_ones()
    row3 = lambda t: t.reshape(t.shape[0], 1, t.shape[1])
    ln1g, ln1b, ln2g, ln2b = row3(ln1_g), row3(ln1_b), row3(ln2_g), row3(ln2_b)
    s0_gla = state_gla.astype(F32).reshape(n_smp, -1, 2, A_HEADS // 2, 128, A_DV)
    s0_dn = state_dn.astype(F32)
    w_up_blk = _cast_blocks(f_w_up, FFN_TF, 2 * D_FF)
    w_down_blk = _cast_blocks(f_w_down, D_MODEL, D_MODEL).reshape(DEPTH, FFN_NBLK, FFN_TF, D_MODEL)
    ffn_conv = f_conv.reshape(DEPTH, 3, 2 * FFN_NBLK, FFN_TF).transpose(0, 2, 1, 3)
    gla_states, dn_states = [], []
    for layer in range(DEPTH):
        if layer % 2 == 0:
            e = layer // 2
            w_in = a_w_in[e]
            w_main = jnp.concatenate([w_in[:, :2 * A_QK + 2 * A_WIDTH], w_in[:, -B_WIDTH:]], axis=1)
            lr0 = 2 * A_QK + 2 * A_WIDTH
            w_lr = jnp.pad(w_in[:, lr0:lr0 + 2 * A_RANK], ((0, 0), (0, 128 - 2 * A_RANK)))
            w_gate_bd = jnp.zeros((128, 2 * A_QK), F32)
            w_gate_bd = w_gate_bd.at[:A_RANK, :A_QK].set(a_w_gate[e, 0]).at[A_RANK:2 * A_RANK, A_QK:].set(a_w_gate[e, 1])
            proj = _proj(cfg, x, mod, layer, w_main, 512)
            gates = _gates(cfg, x, mod, layer, w_lr, w_gate_bd, a_b_gate[e].reshape(1, 2 * A_QK))
            of, ob, st = _gla(cfg, proj, gates, s0_gla, e, gla_consts)
            gla_states.append(st.reshape(n_ctx, 2, A_HEADS, A_DK, A_DV))
            x = _out_even(cfg, of, ob, proj, x, mod, layer, row3(a_norm), b_proj, row3(b_scale), a_w_out, ln1g, ln1b)
        else:
            o_ = layer // 2
            w_zab = jnp.pad(c_w_in[o_][:, 3 * C_WIDTH:], ((0, 0), (0, 128 - 4 * C_HEADS)))
            qkv = _qkv_conv(cfg, x, mod, layer, c_w_in, c_conv, o_)
            zab = _proj(cfg, x, mod, layer, w_zab, 384)
            pad16 = lambda t: jnp.pad(t.reshape(1, -1), ((0, 0), (0, 128 - 2 * C_HEADS)))
            of, ob, st = _delta(cfg, qkv, zab, s0_dn, o_, pad16(c_a_log[o_]), pad16(c_dt_bias[o_]), tri)
            dn_states.append(st)
            x = _out_odd(cfg, of, ob, zab, x, mod, layer, row3(c_norm), c_w_out, ln1g, ln1b)
        x = _ffn(cfg, x, mod, layer, w_up_blk, ffn_conv, w_down_blk, ln2g, ln2b, split_out=layer == DEPTH - 1)

    y_prompt = x[0].reshape(n_ctx, seq, D_MODEL)
    y_sample = x[1].reshape(n_smp, smp_len, D_MODEL)
    return (y_prompt, y_sample, jnp.stack(gla_states, axis=1).astype(x_prompt.dtype),
            jnp.stack(dn_states, axis=1).astype(x_prompt.dtype))
```

```python
import functools
import math

import numpy as np
import jax
import jax.numpy as jnp
from jax import lax
from jax.experimental import pallas as pl
from jax.experimental.pallas import tpu as pltpu

F32 = jnp.float32
BF16 = jnp.bfloat16

D_MODEL = 1024
DEPTH = 4
GRID_W = 64
A_HEADS = 4
A_DK = 64
A_DV = 128
A_QK = A_HEADS * A_DK
A_WIDTH = A_HEADS * A_DV
A_RANK = 16
A_GATE_NORM = 16.0
B_WIDTH = 512
POOL_WINDOWS = (2, 4, 8, 16)
B_GW = 128
C_HEADS = 8
C_DK = 128
C_WIDTH = 1024
D_FF = 2816
N_MOD = 6
ALPHA = (2 * DEPTH) ** 0.25
EPS = 1e-6

SEQ_TILE = 256
HALO = 8
ROW_CHUNK = 256
assert ROW_CHUNK == SEQ_TILE
MOD_ROWS = 8
V7X_VMEM_LIMIT = 56 * 1024 * 1024


def _cparams(sem):
    return pltpu.CompilerParams(dimension_semantics=sem, vmem_limit_bytes=V7X_VMEM_LIMIT)


class _Cfg:
    def __init__(self, n_ctx, n_smp, smp_len):
        assert smp_len % SEQ_TILE == 0 and smp_len & (smp_len - 1) == 0
        self.n_ctx, self.n_smp, self.smp_len = n_ctx, n_smp, smp_len
        self.t_ctx = n_ctx * SEQ_TILE
        self.t_all = self.t_ctx + n_smp * smp_len
        self.tps = smp_len // SEQ_TILE
        self.n_tiles = self.t_all // SEQ_TILE
        self.n_seq = n_ctx + n_smp
        tm = 1024
        while self.t_ctx % tm or smp_len % tm:
            tm //= 2
        self.tm = tm

    def cond_of_row(self, g):
        return jnp.where(g < self.t_ctx, 0, 1 + (g - self.t_ctx) // self.smp_len)

    def seq_of_tile(self, i):
        return jnp.where(i < self.n_ctx, i, self.n_ctx + (i - self.n_ctx) // self.tps)

    def rev_tile(self, i):
        m = (i - self.n_ctx) % self.tps
        return jnp.where(i < self.n_ctx, i, i - m + (self.tps - 1 - m))

    def chunk_edges(self, row0):
        is_ctx = row0 < self.t_ctx
        length = jnp.where(is_ctx, SEQ_TILE, self.smp_len)
        pos0 = jnp.bitwise_and(jnp.where(is_ctx, row0, row0 - self.t_ctx), length - 1)
        return pos0 == 0, pos0 + ROW_CHUNK == length

    def first_of_seq(self, i):
        return jnp.logical_or(i < self.n_ctx, (i - self.n_ctx) % self.tps == 0)

    def last_of_seq(self, i):
        return jnp.logical_or(i < self.n_ctx, (i - self.n_ctx) % self.tps == self.tps - 1)


def _mod_map(cfg, layer, which, rows):
    def index_map(i, *_):
        return ((layer * MOD_ROWS + cfg.cond_of_row(i * rows)) * N_MOD + which, 0, 0)
    return index_map


def _mod_spec(cfg, layer, which, rows):
    return pl.BlockSpec((None, 1, D_MODEL), _mod_map(cfg, layer, which, rows))


def _halo_specs(cfg, rows, width, col_block=0):
    per = rows // HALO
    last = cfg.t_all // HALO - 1
    prev = pl.BlockSpec((HALO, width), lambda i, *_: (jnp.maximum(i * per - 1, 0), col_block))
    nxt = pl.BlockSpec((HALO, width), lambda i, *_: (jnp.minimum((i + 1) * per, last), col_block))
    return prev, nxt


def _seq_edges(cfg, tile, rows):
    g = tile * rows + lax.broadcasted_iota(jnp.int32, (rows, 1), 0)
    is_ctx = g < cfg.t_ctx
    length = jnp.where(is_ctx, SEQ_TILE, cfg.smp_len)
    pos = jnp.bitwise_and(jnp.where(is_ctx, g, g - cfg.t_ctx), length - 1)
    return pos, length


def _silu(x):
    return x * jax.nn.sigmoid(x)


def _dot(a, b):
    return jnp.dot(a, b, preferred_element_type=F32)


def _dot_nt(a, b):
    return lax.dot_general(a, b, (((1,), (1,)), ((), ())), preferred_element_type=F32)


def _split_bf16(x):
    hi = x.astype(BF16)
    lo = (x - hi.astype(F32)).astype(BF16)
    return hi, lo


def _conv3(y_ref, cw, starts_seq, ends_seq):
    n = ROW_CHUNK
    y_ref[HALO - 1:HALO, :] = jnp.where(starts_seq, 0.0, y_ref[HALO - 1:HALO, :])
    y_ref[HALO + n:HALO + n + 1, :] = jnp.where(ends_seq, 0.0, y_ref[HALO + n:HALO + n + 1, :])
    return (y_ref[HALO - 1:HALO - 1 + n, :] * cw[0:1, :]
            + y_ref[HALO:HALO + n, :] * cw[1:2, :]
            + y_ref[HALO + 1:HALO + 1 + n, :] * cw[2:3, :])


def _layernorm_rows(x, g, b):
    mu = jnp.mean(x, axis=-1, keepdims=True)
    xc = x - mu
    var = jnp.mean(xc * xc, axis=-1, keepdims=True)
    return xc * lax.rsqrt(var + EPS) * g + b


def _state_specs(cfg, slot, per_seq):
    zeros = (0,) * len(per_seq)
    n_ctx = cfg.n_ctx
    in_spec = pl.BlockSpec((None, None) + per_seq,
                           lambda i: (jnp.maximum(cfg.seq_of_tile(i) - n_ctx, 0), slot) + zeros)
    out_spec = pl.BlockSpec((None,) + per_seq, lambda i: (jnp.minimum(cfg.seq_of_tile(i), n_ctx - 1),) + zeros)
    return in_spec, out_spec


def _load_states(cfg, i, s0_ref, sf_scr, sb_scr):
    @pl.when(cfg.first_of_seq(i))
    def _():
        is_ctx = i < cfg.n_ctx
        sf_scr[...] = jnp.where(is_ctx, 0.0, s0_ref[0])
        sb_scr[...] = jnp.where(is_ctx, 0.0, s0_ref[1])


def _store_states(cfg, i, st_ref, sf_scr, sb_scr):
    @pl.when(i < cfg.n_ctx)
    def _():
        st_ref[0] = sf_scr[...]
        st_ref[1] = sb_scr[...]


def _embed_kernel(n_ctx_tiles, xp_ref, xs_ref, pe_ref, o_ref):
    i = pl.program_id(0)

    @pl.when(i < n_ctx_tiles)
    def _():
        o_ref[...] = xp_ref[...]

    @pl.when(i >= n_ctx_tiles)
    def _():
        o_ref[...] = xs_ref[...] + pe_ref[...]


def _embed(cfg, xp, xs, pe):
    tm = cfg.tm
    n_ctx_tiles, tps = cfg.t_ctx // tm, cfg.smp_len // tm
    return pl.pallas_call(
        functools.partial(_embed_kernel, n_ctx_tiles),
        grid=(cfg.t_all // tm,),
        in_specs=[
            pl.BlockSpec((tm, D_MODEL), lambda i: (jnp.minimum(i, n_ctx_tiles - 1), 0)),
            pl.BlockSpec((tm, D_MODEL), lambda i: (jnp.maximum(i - n_ctx_tiles, 0), 0)),
            pl.BlockSpec((tm, D_MODEL), lambda i: (jnp.maximum(i - n_ctx_tiles, 0) % tps, 0)),
        ],
        out_specs=pl.BlockSpec((tm, D_MODEL), lambda i: (i, 0)),
        out_shape=jax.ShapeDtypeStruct((cfg.t_all, D_MODEL), F32),
        compiler_params=_cparams(("arbitrary",)),
        name="embed",
    )(xp, xs, pe)


CAST_ROWS = 256


def _cast_blocks_kernel(tn, nblk, col0, w_ref, o_ref):
    for c in range(nblk):
        o_ref[c] = w_ref[:, col0 + tn * c:col0 + tn * (c + 1)].astype(BF16)


def _cast_blocks(w, tn, ncols, col0=0):
    n_l, k, n = w.shape
    nblk = ncols // tn
    return pl.pallas_call(
        functools.partial(_cast_blocks_kernel, tn, nblk, col0),
        grid=(n_l, k // CAST_ROWS),
        in_specs=[pl.BlockSpec((None, CAST_ROWS, n), lambda l, r: (l, r, 0))],
        out_specs=pl.BlockSpec((None, nblk, CAST_ROWS, tn), lambda l, r: (l, 0, r, 0)),
        out_shape=jax.ShapeDtypeStruct((n_l, nblk, k, tn), BF16),
        compiler_params=_cparams(("arbitrary", "arbitrary")),
        name=f"cast_blocks_{n}",
    )(w)


def _mod_kernel(c_ref, w_ref, b_ref, o_ref):
    sc = _silu(c_ref[...]).astype(BF16)
    o_ref[...] = _dot(sc, w_ref[...].astype(BF16)) + b_ref[...]


def _modulation(cond, w_mod, b_mod):
    out = pl.pallas_call(
        _mod_kernel,
        grid=(DEPTH, N_MOD),
        in_specs=[
            pl.BlockSpec((MOD_ROWS, D_MODEL), lambda l, j: (0, 0)),
            pl.BlockSpec((None, D_MODEL, D_MODEL), lambda l, j: (l, 0, j)),
            pl.BlockSpec((None, 1, D_MODEL), lambda l, j: (l, 0, j)),
        ],
        out_specs=pl.BlockSpec((None, MOD_ROWS, D_MODEL), lambda l, j: (l, 0, j)),
        out_shape=jax.ShapeDtypeStruct((DEPTH, MOD_ROWS, N_MOD * D_MODEL), F32),
        compiler_params=_cparams(("arbitrary", "arbitrary")),
        name="modulation",
    )(cond, w_mod, b_mod.reshape(DEPTH, 1, N_MOD * D_MODEL))
    return out.reshape(DEPTH * MOD_ROWS * N_MOD, 1, D_MODEL)


def _proj_kernel(x_ref, sc_ref, sh_ref, w_ref, o_ref, u_scr):
    @pl.when(pl.program_id(1) == 0)
    def _():
        u_scr[...] = (x_ref[...] * (1.0 + sc_ref[...]) + sh_ref[...]).astype(BF16)

    o_ref[...] = _dot(u_scr[...], w_ref[...].astype(BF16))


def _proj(cfg, x, mod, layer, w, tn):
    n = w.shape[1]
    tm = cfg.tm
    return pl.pallas_call(
        _proj_kernel,
        grid=(cfg.t_all // tm, n // tn),
        in_specs=[
            pl.BlockSpec((tm, D_MODEL), lambda i, j: (i, 0)),
            _mod_spec(cfg, layer, 1, tm),
            _mod_spec(cfg, layer, 0, tm),
            pl.BlockSpec((D_MODEL, tn), lambda i, j: (0, j)),
        ],
        out_specs=pl.BlockSpec((tm, tn), lambda i, j: (i, j)),
        out_shape=jax.ShapeDtypeStruct((cfg.t_all, n), F32),
        scratch_shapes=[pltpu.VMEM((tm, D_MODEL), BF16)],
        compiler_params=_cparams(("arbitrary", "arbitrary")),
        name=f"proj_l{layer}",
    )(x, mod, mod, w)


def _gate_kernel(x_ref, sc_ref, sh_ref, wlr_ref, wg_ref, bg_ref, o_ref):
    u = (x_ref[...] * (1.0 + sc_ref[...]) + sh_ref[...]).astype(BF16)
    lr = _dot(u, wlr_ref[...].astype(BF16))
    z = _dot(lr.astype(BF16), wg_ref[...].astype(BF16)) + bg_ref[...]
    log_sig = jnp.minimum(z, 0.0) - jnp.log1p(jnp.exp(-jnp.abs(z)))
    o_ref[...] = log_sig / A_GATE_NORM


def _gates(cfg, x, mod, layer, w_lr, w_gate_bd, b_gate):
    tm = cfg.tm
    return pl.pallas_call(
        _gate_kernel,
        grid=(cfg.t_all // tm,),
        in_specs=[
            pl.BlockSpec((tm, D_MODEL), lambda i: (i, 0)),
            _mod_spec(cfg, layer, 1, tm),
            _mod_spec(cfg, layer, 0, tm),
            pl.BlockSpec((D_MODEL, 128), lambda i: (0, 0)),
            pl.BlockSpec((128, 2 * A_QK), lambda i: (0, 0)),
            pl.BlockSpec((1, 2 * A_QK), lambda i: (0, 0)),
        ],
        out_specs=pl.BlockSpec((tm, 2 * A_QK), lambda i: (i, 0)),
        out_shape=jax.ShapeDtypeStruct((cfg.t_all, 2 * A_QK), F32),
        compiler_params=_cparams(("arbitrary",)),
        name=f"gates_l{layer}",
    )(x, mod, mod, w_lr, w_gate_bd, b_gate)


GLA_LEVELS = 8
GLA_STACK = GLA_LEVELS + 2


def _gla_constants():
    n = SEQ_TILE
    idx = np.arange(n)
    i, m = idx[:, None], idx[None, :]
    stack = np.zeros((2, GLA_STACK, n, n), np.float32)
    masks = np.zeros((2, GLA_LEVELS + 1, n, n), np.float32)
    stack[0, 0] = m <= i
    stack[0, 1] = m > i
    stack[1, 0] = m >= i
    stack[1, 1] = m < i
    for l in range(GLA_LEVELS):
        s = 1 << l
        blk = idx // (2 * s)
        upper = (idx // s) % 2 == 1
        piv_f = blk * 2 * s + s - 1
        piv_b = blk * 2 * s + s
        up, pf, pb = upper[:, None], piv_f[:, None], piv_b[:, None]
        stack[0, 2 + l] = np.where(up, (m > pf) & (m <= i), (m > i) & (m <= pf))
        stack[1, 2 + l] = np.where(up, (m >= pb) & (m < i), (m >= i) & (m < pb))
        same = blk[:, None] == blk[None, :]
        masks[0, l] = same & upper[:, None] & ~upper[None, :]
        masks[1, l] = same & ~upper[:, None] & upper[None, :]
    masks[:, GLA_LEVELS] = np.eye(n)
    return (jnp.asarray(stack.reshape(2, GLA_STACK * n, n), BF16), jnp.asarray(masks, F32))


def _gla_direction(qk, v, g, stack_ref, mask_ref, s_scr, o_ref):
    n = SEQ_TILE
    q = qk[:, :A_QK] * (A_DK ** -0.5)
    k = qk[:, A_QK:]
    g_hi, g_lo = _split_bf16(g)
    sums = _dot(stack_ref[...], g_hi) + _dot(stack_ref[...], g_lo)
    e_in = jnp.exp(sums[0:n])
    e_out = jnp.exp(sums[n:2 * n])
    lane = lax.broadcasted_iota(jnp.int32, (n, 128), 1)
    low_half = lane < A_DK

    att = [jnp.zeros((n, n), F32) for _ in range(A_HEADS)]
    for l in range(GLA_LEVELS + 1):
        if l < GLA_LEVELS:
            e = jnp.exp(sums[(2 + l) * n:(3 + l) * n])
            qs, ks = q * e, k * e
        else:
            qs, ks = q, k
        m = mask_ref[l]
        for p in range(A_HEADS // 2):
            qp = qs[:, 128 * p:128 * (p + 1)]
            kp = ks[:, 128 * p:128 * (p + 1)].astype(BF16)
            for hh in range(2):
                qh = jnp.where(low_half if hh == 0 else ~low_half, qp, 0.0).astype(BF16)
                att[2 * p + hh] = att[2 * p + hh] + m * _dot_nt(qh, kp)

    qd = q * e_in
    kd = k * e_out
    kd_t = kd.T.astype(BF16)
    g_t = g.T
    gt_hi, gt_lo = _split_bf16(g_t)
    ones = jnp.ones((n, 128), BF16)
    dec = jnp.exp(_dot(gt_hi, ones) + _dot(gt_lo, ones))
    for h in range(A_HEADS):
        p, hh = h // 2, h % 2
        v_h = v[:, A_DV * h:A_DV * (h + 1)].astype(BF16)
        s_pair = s_scr[p]
        qp = qd[:, 128 * p:128 * (p + 1)]
        qh = jnp.where(low_half if hh == 0 else ~low_half, qp, 0.0).astype(BF16)
        o_h = _dot(att[h].astype(BF16), v_h) + _dot(qh, s_pair.astype(BF16))
        o_ref[:, A_DV * h:A_DV * (h + 1)] = o_h
    for h in range(A_HEADS):
        p, hh = h // 2, h % 2
        v_h = v[:, A_DV * h:A_DV * (h + 1)].astype(BF16)
        rows = slice(A_DK * hh, A_DK * (hh + 1))
        s_old = s_scr[p, rows, :]
        s_scr[p, rows, :] = s_old * dec[A_DK * h:A_DK * (h + 1), :] + _dot(kd_t[A_DK * h:A_DK * (h + 1), :], v_h)


def _gla_kernel(cfg, qk_f, v_f, g_f, qk_b, v_b, g_b, s0_ref, stack_ref, mask_ref,
                of_ref, ob_ref, st_ref, sf_scr, sb_scr):
    i = pl.program_id(0)

    _load_states(cfg, i, s0_ref, sf_scr, sb_scr)

    _gla_direction(qk_f[...], v_f[...], g_f[...], stack_ref.at[0], mask_ref.at[0], sf_scr, of_ref)
    _gla_direction(qk_b[...], v_b[...], g_b[...], stack_ref.at[1], mask_ref.at[1], sb_scr, ob_ref)

    _store_states(cfg, i, st_ref, sf_scr, sb_scr)


def _gla(cfg, proj, gates, s0, slot, consts):
    stack, masks = consts
    rev = cfg.rev_tile
    n = SEQ_TILE
    st_shape = (cfg.n_ctx, 2, A_HEADS // 2, 128, A_DV)
    s0_spec, st_spec = _state_specs(cfg, slot, st_shape[1:])
    return pl.pallas_call(
        functools.partial(_gla_kernel, cfg),
        grid=(cfg.n_tiles,),
        in_specs=[
            pl.BlockSpec((n, 2 * A_QK), lambda i: (i, 0)),
            pl.BlockSpec((n, A_WIDTH), lambda i: (i, 1)),
            pl.BlockSpec((n, A_QK), lambda i: (i, 0)),
            pl.BlockSpec((n, 2 * A_QK), lambda i: (rev(i), 0)),
            pl.BlockSpec((n, A_WIDTH), lambda i: (rev(i), 1)),
            pl.BlockSpec((n, A_QK), lambda i: (rev(i), 1)),
            s0_spec,
            pl.BlockSpec(stack.shape, lambda i: (0, 0, 0)),
            pl.BlockSpec(masks.shape, lambda i: (0, 0, 0, 0)),
        ],
        out_specs=[
            pl.BlockSpec((n, A_WIDTH), lambda i: (i, 0)),
            pl.BlockSpec((n, A_WIDTH), lambda i: (rev(i), 0)),
            st_spec,
        ],
        out_shape=[
            jax.ShapeDtypeStruct((cfg.t_all, A_WIDTH), F32),
            jax.ShapeDtypeStruct((cfg.t_all, A_WIDTH), F32),
            jax.ShapeDtypeStruct(st_shape, F32),
        ],
        scratch_shapes=[pltpu.VMEM(st_shape[2:], F32), pltpu.VMEM(st_shape[2:], F32)],
        compiler_params=_cparams(("arbitrary",)),
        name="gla_scan",
    )(proj, proj, gates, proj, proj, gates, s0, stack, masks)


def _rms_heads(o, gain, n_heads, width):
    parts = []
    for h in range(n_heads):
        oh = o[:, width * h:width * (h + 1)]
        parts.append(oh * lax.rsqrt(jnp.mean(oh * oh, axis=-1, keepdims=True) + EPS) * gain)
    return parts


def _out_even_kernel(cfg, of_ref, ob_ref, r_ref, pz_ref, pzp_ref, pzn_ref, x_ref, g1_ref,
                     norm_ref, bproj_ref, bscale_ref, wout_ref, lng_ref, lnb_ref, o_ref, ext_scr, w_scr):
    n = SEQ_TILE
    i = pl.program_id(0)

    @pl.when(i == 0)
    def _():
        w_scr[...] = wout_ref[...].astype(BF16)

    o = of_ref[...] + ob_ref[...]
    r = r_ref[...]
    heads = _rms_heads(o, norm_ref[...], A_HEADS, A_DV)
    parts = [(heads[h] * _silu(r[:, A_DV * h:A_DV * (h + 1)])).astype(BF16) for h in range(A_HEADS)]

    pos, length = _seq_edges(cfg, i, n)
    starts_seq, ends_seq = cfg.chunk_edges(i * n)
    ext_scr[0:HALO, :] = jnp.where(starts_seq, 0.0, pzp_ref[...])
    ext_scr[HALO:HALO + n, :] = pz_ref[...]
    ext_scr[HALO + n:, :] = jnp.where(ends_seq, 0.0, pzn_ref[...])
    for gi, win in enumerate(POOL_WINDOWS):
        lo = win // 2
        hi = win - 1 - lo
        cols = slice(B_GW * gi, B_GW * (gi + 1))
        acc = ext_scr[HALO - lo:HALO - lo + n, cols]
        for d in range(-lo + 1, hi + 1):
            acc = acc + ext_scr[HALO + d:HALO + d + n, cols]
        cnt = (jnp.minimum(pos + hi + 1, length) - jnp.maximum(pos - lo, 0)).astype(F32)
        pooled = (acc / cnt - ext_scr[HALO:HALO + n, cols]).astype(BF16)
        mixed = _dot(pooled, bproj_ref[gi].astype(BF16)) * bscale_ref[:, cols]
        parts.append(mixed.astype(BF16))

    y = _dot(jnp.concatenate(parts, axis=1), w_scr[...])
    o_ref[...] = _layernorm_rows(ALPHA * x_ref[...] + g1_ref[...] * y, lng_ref[...], lnb_ref[...])


def _out_even(cfg, of, ob, proj, x, mod, layer, norm_g, b_proj, b_scale, w_out, ln_g, ln_b):
    n, e = SEQ_TILE, layer // 2
    pz_prev, pz_next = _halo_specs(cfg, n, B_WIDTH, col_block=3)
    return pl.pallas_call(
        functools.partial(_out_even_kernel, cfg),
        grid=(cfg.n_tiles,),
        in_specs=[
            pl.BlockSpec((n, A_WIDTH), lambda i: (i, 0)),
            pl.BlockSpec((n, A_WIDTH), lambda i: (i, 0)),
            pl.BlockSpec((n, A_WIDTH), lambda i: (i, 2)),
            pl.BlockSpec((n, B_WIDTH), lambda i: (i, 3)),
            pz_prev, pz_next,
            pl.BlockSpec((n, D_MODEL), lambda i: (i, 0)),
            _mod_spec(cfg, layer, 2, n),
            pl.BlockSpec((None, 1, A_DV), lambda i: (e, 0, 0)),
            pl.BlockSpec((None, len(POOL_WINDOWS), B_GW, B_GW), lambda i: (e, 0, 0, 0)),
            pl.BlockSpec((None, 1, B_WIDTH), lambda i: (e, 0, 0)),
            pl.BlockSpec((None, D_MODEL, D_MODEL), lambda i: (e, 0, 0)),
            pl.BlockSpec((None, 1, D_MODEL), lambda i: (layer, 0, 0)),
            pl.BlockSpec((None, 1, D_MODEL), lambda i: (layer, 0, 0)),
        ],
        out_specs=pl.BlockSpec((n, D_MODEL), lambda i: (i, 0)),
        out_shape=jax.ShapeDtypeStruct((cfg.t_all, D_MODEL), F32),
        scratch_shapes=[pltpu.VMEM((n + 2 * HALO, B_WIDTH), F32), pltpu.VMEM((D_MODEL, D_MODEL), BF16)],
        compiler_params=_cparams(("arbitrary",)),
        name=f"out_even_l{layer}",
    )(of, ob, proj, proj, proj, proj, x, mod, norm_g, b_proj, b_scale, w_out, ln_g, ln_b)


def _qkv_conv_kernel(cfg, tm, tn, x_ref, xp_ref, xn_ref, sc_ref, sh_ref, w_ref, cw_ref, o_ref, u_scr, y_scr):
    i, j = pl.program_id(0), pl.program_id(1)

    @pl.when(j == 0)
    def _():
        sc, sh = 1.0 + sc_ref[...], sh_ref[...]
        u_scr[0:HALO, :] = (xp_ref[...] * sc + sh).astype(BF16)
        u_scr[HALO:HALO + tm, :] = (x_ref[...] * sc + sh).astype(BF16)
        u_scr[HALO + tm:, :] = (xn_ref[...] * sc + sh).astype(BF16)

    w = w_ref[...].astype(BF16)
    cw = cw_ref[...]
    is_q = j == 0
    is_qk = j < 2
    nchunk = tm // ROW_CHUNK

    def up(r):
        y_scr[r] = _dot(u_scr[ROW_CHUNK * r:ROW_CHUNK * (r + 1) + 2 * HALO, :], w)

    up(0)
    for r in range(nchunk):
        if r + 1 < nchunk:
            up(r + 1)
        rows = slice(ROW_CHUNK * r, ROW_CHUNK * (r + 1))
        h = _silu(_conv3(y_scr.at[r], cw, *cfg.chunk_edges(i * tm + ROW_CHUNK * r)))
        for c in range(tn // C_DK):
            hc = h[:, C_DK * c:C_DK * (c + 1)]
            inv = lax.rsqrt(jnp.sum(hc * hc, axis=-1, keepdims=True) + EPS)
            scale = jnp.where(is_qk, inv * jnp.where(is_q, C_DK ** -0.5, 1.0), 1.0)
            o_ref[rows, C_DK * c:C_DK * (c + 1)] = hc * scale


def _qkv_conv(cfg, x, mod, layer, w_in, conv_w, idx):
    tm, tn = cfg.tm, C_WIDTH
    n = 3 * C_WIDTH
    xp, xn = _halo_specs(cfg, tm, D_MODEL)
    return pl.pallas_call(
        functools.partial(_qkv_conv_kernel, cfg, tm, tn),
        grid=(cfg.t_all // tm, n // tn),
        in_specs=[
            pl.BlockSpec((tm, D_MODEL), lambda i, j: (i, 0)),
            xp, xn,
            _mod_spec(cfg, layer, 1, tm),
            _mod_spec(cfg, layer, 0, tm),
            pl.BlockSpec((None, D_MODEL, tn), lambda i, j: (idx, 0, j)),
            pl.BlockSpec((None, 3, tn), lambda i, j: (idx, 0, j)),
        ],
        out_specs=pl.BlockSpec((tm, tn), lambda i, j: (i, j)),
        out_shape=jax.ShapeDtypeStruct((cfg.t_all, n), F32),
        scratch_shapes=[pltpu.VMEM((tm + 2 * HALO, D_MODEL), BF16),
                        pltpu.VMEM((tm // ROW_CHUNK, ROW_CHUNK + 2 * HALO, tn), F32)],
        compiler_params=_cparams(("arbitrary", "arbitrary")),
        name=f"qkv_conv_l{layer}",
    )(x, x, x, mod, mod, w_in, conv_w)


DN_BLOCK = 64
DN_NBLK = SEQ_TILE // DN_BLOCK


def _tri_ones():
    n = SEQ_TILE
    idx = np.arange(n)
    lower = (idx[None, :] <= idx[:, None]).astype(np.float32)
    return jnp.asarray(np.stack([lower, lower.T]), BF16)


def _unit_tri_inverses(packs):
    n, w = DN_BLOCK, SEQ_TILE
    row = lax.broadcasted_iota(jnp.int32, (n, w), 0)
    col = jnp.bitwise_and(lax.broadcasted_iota(jnp.int32, (n, w), 1), n - 1)
    eye = (row == col).astype(F32)
    band = lax.broadcasted_iota(jnp.int32, (w, w), 0) // n
    keep = (band == lax.broadcasted_iota(jnp.int32, (w, w), 1) // n).astype(BF16)

    def block_diag(x):
        return jnp.concatenate([x.astype(BF16)] * DN_NBLK, axis=0) * keep

    ts = [eye - jnp.where((row >> 1) == (col >> 1), a, 0.0) for a in packs]
    for l in range(1, int(math.log2(n))):
        couple = jnp.logical_and((row >> (l + 1)) == (col >> (l + 1)), (row >> l) != (col >> l))
        ws = [_dot(jnp.where(couple, a, 0.0).astype(BF16), block_diag(t)) for a, t in zip(packs, ts)]
        ts = [t - _dot(t.astype(BF16), block_diag(wv)) for t, wv in zip(ts, ws)]
    return ts


def _delta_tile(dirs):
    n = SEQ_TILE
    row = lax.broadcasted_iota(jnp.int32, (n, n), 0)
    col = lax.broadcasted_iota(jnp.int32, (n, n), 1)
    probs = []
    for d, (q_ref, k_ref, v_ref, g, beta, tri_ref, s_scr, o_ref) in enumerate(dirs):
        incl = (col <= row) if d == 0 else (col >= row)
        strict = (col < row) if d == 0 else (col > row)
        last = n - 1 if d == 0 else 0
        g_hi, g_lo = _split_bf16(g)
        b_col = _dot(tri_ref[...], g_hi) + _dot(tri_ref[...], g_lo)
        b_row = b_col.T
        for h in range(C_HEADS):
            c = d * C_HEADS + h
            cols = slice(C_DK * h, C_DK * (h + 1))
            q, k, v = q_ref[:, cols], k_ref[:, cols], v_ref[:, cols]
            bc = b_col[:, c:c + 1]
            br = b_row[c:c + 1, :]
            b_last = b_col[last:last + 1, c:c + 1]
            ld = jnp.exp(jnp.where(incl, bc - br, -jnp.inf))
            bt = beta[:, 2 * C_HEADS + c:2 * C_HEADS + c + 1]
            kb = k * bt
            kbf = k.astype(BF16)
            e_col = jnp.exp(bc)
            probs.append(dict(
                d=d, h=h, cols=cols, s_scr=s_scr, o_ref=o_ref,
                a_mat=jnp.where(strict, _dot_nt(kb.astype(BF16), kbf) * ld, 0.0),
                aqk=(_dot_nt(q.astype(BF16), kbf) * ld).astype(BF16),
                rhs=jnp.concatenate([v * bt, kb * e_col], axis=1),
                qd=(q * e_col).astype(BF16),
                kd_t=(k * jnp.exp(b_last - bc)).T.astype(BF16),
                dec=jnp.exp(b_last)))

    blk = lambda b: slice(DN_BLOCK * b, DN_BLOCK * (b + 1))
    on_diag = (row // DN_BLOCK) == (col // DN_BLOCK)
    packs = []
    for p in probs:
        dm = jnp.where(on_diag, p["a_mat"], 0.0)
        packs.append(sum(dm[blk(b), :] for b in range(1, DN_NBLK)) + dm[blk(0), :])
    t_inv = _unit_tri_inverses(packs)

    xs = [[None] * DN_NBLK for _ in probs]
    for step in range(DN_NBLK):
        for pi, p in enumerate(probs):
            b = step if p["d"] == 0 else DN_NBLK - 1 - step
            resid = p["rhs"][blk(b), :]
            for bb in range(DN_NBLK):
                if xs[pi][bb] is not None:
                    resid = resid - _dot(p["a_mat"][blk(b), blk(bb)].astype(BF16), xs[pi][bb].astype(BF16))
            p["resid"] = resid
        for pi, p in enumerate(probs):
            b = step if p["d"] == 0 else DN_NBLK - 1 - step
            xs[pi][b] = _dot(t_inv[pi][:, blk(b)].astype(BF16), p["resid"].astype(BF16))

    for pi, p in enumerate(probs):
        x = jnp.concatenate(xs[pi], axis=0)
        p["u"], p["w"] = x[:, :C_DK], x[:, C_DK:].astype(BF16)
        p["s"] = p["s_scr"][p["h"]]
        p["sb"] = p["s"].astype(BF16)
    for p in probs:
        p["vb"] = (p["u"] - _dot(p["w"], p["sb"])).astype(BF16)
    for p in probs:
        p["o_ref"][:, p["cols"]] = _dot(p["qd"], p["sb"]) + _dot(p["aqk"], p["vb"])
    for p in probs:
        p["s_scr"][p["h"]] = p["s"] * p["dec"] + _dot(p["kd_t"], p["vb"])


def _delta_kernel(cfg, qf, kf, vf, abf, qb, kb, vb, abb, s0_ref, alog_ref, dtb_ref, tri_ref,
                  of_ref, ob_ref, st_ref, sf_scr, sb_scr):
    i = pl.program_id(0)

    _load_states(cfg, i, s0_ref, sf_scr, sb_scr)

    lane = lax.broadcasted_iota(jnp.int32, (1, 128), 1)
    neg_a = jnp.where(lane < 2 * C_HEADS, -jnp.exp(alog_ref[...]), 0.0)

    def gate(ab):
        a_in = ab + dtb_ref[...]
        softplus = jnp.maximum(a_in, 0.0) + jnp.log1p(jnp.exp(-jnp.abs(a_in)))
        return neg_a * softplus, jax.nn.sigmoid(ab)

    g_f, beta_f = gate(abf[...])
    g_b, beta_b = gate(abb[...])
    _delta_tile([(qf, kf, vf, g_f, beta_f, tri_ref.at[0], sf_scr, of_ref),
                 (qb, kb, vb, g_b, beta_b, tri_ref.at[1], sb_scr, ob_ref)])

    _store_states(cfg, i, st_ref, sf_scr, sb_scr)


def _delta(cfg, qkv, zab, s0, slot, a_log, dt_bias, tri):
    rev = cfg.rev_tile
    n = SEQ_TILE
    st_shape = (cfg.n_ctx, 2, C_HEADS, C_DK, C_DK)
    s0_spec, st_spec = _state_specs(cfg, slot, st_shape[1:])
    ab_col = C_WIDTH // 128
    in_specs = []
    for tile_of in (lambda i: i, rev):
        for cb in range(3):
            in_specs.append(pl.BlockSpec((n, C_WIDTH), functools.partial(lambda i, f, c: (f(i), c), f=tile_of, c=cb)))
        in_specs.append(pl.BlockSpec((n, 128), functools.partial(lambda i, f: (f(i), ab_col), f=tile_of)))
    in_specs += [
        s0_spec,
        pl.BlockSpec((1, 128), lambda i: (0, 0)),
        pl.BlockSpec((1, 128), lambda i: (0, 0)),
        pl.BlockSpec(tri.shape, lambda i: (0, 0, 0)),
    ]
    return pl.pallas_call(
        functools.partial(_delta_kernel, cfg),
        grid=(cfg.n_tiles,),
        in_specs=in_specs,
        out_specs=[
            pl.BlockSpec((n, C_WIDTH), lambda i: (i, 0)),
            pl.BlockSpec((n, C_WIDTH), lambda i: (rev(i), 0)),
            st_spec,
        ],
        out_shape=[
            jax.ShapeDtypeStruct((cfg.t_all, C_WIDTH), F32),
            jax.ShapeDtypeStruct((cfg.t_all, C_WIDTH), F32),
            jax.ShapeDtypeStruct(st_shape, F32),
        ],
        scratch_shapes=[pltpu.VMEM(st_shape[2:], F32), pltpu.VMEM(st_shape[2:], F32)],
        compiler_params=_cparams(("arbitrary",)),
        name="delta_scan",
    )(qkv, qkv, qkv, zab, qkv, qkv, qkv, zab, s0, a_log, dt_bias, tri)


def _out_odd_kernel(of_ref, ob_ref, z_ref, x_ref, g1_ref, norm_ref, wout_ref, lng_ref, lnb_ref, o_ref, w_scr):
    @pl.when(pl.program_id(0) == 0)
    def _():
        w_scr[...] = wout_ref[...].astype(BF16)

    o = of_ref[...] + ob_ref[...]
    z = z_ref[...]
    heads = _rms_heads(o, norm_ref[...], C_HEADS, C_DK)
    parts = [(heads[h] * _silu(z[:, C_DK * h:C_DK * (h + 1)])).astype(BF16) for h in range(C_HEADS)]
    y = _dot(jnp.concatenate(parts, axis=1), w_scr[...])
    o_ref[...] = _layernorm_rows(ALPHA * x_ref[...] + g1_ref[...] * y, lng_ref[...], lnb_ref[...])


def _out_odd(cfg, of, ob, zab, x, mod, layer, norm_g, w_out, ln_g, ln_b):
    n, e = min(cfg.tm, 2 * SEQ_TILE), layer // 2
    return pl.pallas_call(
        _out_odd_kernel,
        grid=(cfg.t_all // n,),
        in_specs=[
            pl.BlockSpec((n, C_WIDTH), lambda i: (i, 0)),
            pl.BlockSpec((n, C_WIDTH), lambda i: (i, 0)),
            pl.BlockSpec((n, C_WIDTH), lambda i: (i, 0)),
            pl.BlockSpec((n, D_MODEL), lambda i: (i, 0)),
            _mod_spec(cfg, layer, 2, n),
            pl.BlockSpec((None, 1, C_DK), lambda i: (e, 0, 0)),
            pl.BlockSpec((None, D_MODEL, D_MODEL), lambda i: (e, 0, 0)),
            pl.BlockSpec((None, 1, D_MODEL), lambda i: (layer, 0, 0)),
            pl.BlockSpec((None, 1, D_MODEL), lambda i: (layer, 0, 0)),
        ],
        out_specs=pl.BlockSpec((n, D_MODEL), lambda i: (i, 0)),
        out_shape=jax.ShapeDtypeStruct((cfg.t_all, D_MODEL), F32),
        scratch_shapes=[pltpu.VMEM((D_MODEL, D_MODEL), BF16)],
        compiler_params=_cparams(("arbitrary",)),
        name=f"out_odd_l{layer}",
    )(of, ob, zab, x, mod, norm_g, w_out, ln_g, ln_b)


FFN_TF = 256
FFN_NBLK = D_FF // FFN_TF
FFN_TM = 1024


def _ffn_kernel(cfg, tm, split_out, x_ref, xp_ref, xn_ref, sc_ref, sh_ref, g2_ref, wup_ref, cw_ref,
                wd_ref, lng_ref, lnb_ref, *rest):
    out_refs, (u_scr, ya_scr, yg_scr, acc_scr) = rest[:-4], rest[-4:]
    i = pl.program_id(0)
    sc, sh = 1.0 + sc_ref[...], sh_ref[...]
    u_scr[0:HALO, :] = (xp_ref[...] * sc + sh).astype(BF16)
    u_scr[HALO:HALO + tm, :] = (x_ref[...] * sc + sh).astype(BF16)
    u_scr[HALO + tm:, :] = (xn_ref[...] * sc + sh).astype(BF16)
    acc_scr[...] = jnp.zeros_like(acc_scr)
    nchunk = tm // ROW_CHUNK
    edges = [cfg.chunk_edges(i * tm + ROW_CHUNK * r) for r in range(nchunk)]

    def ff_block(jb, carry):
        wa, wg, wd = wup_ref[jb], wup_ref[FFN_NBLK + jb], wd_ref[jb]
        ca, cg = cw_ref[jb], cw_ref[FFN_NBLK + jb]

        def up(r):
            u = u_scr[ROW_CHUNK * r:ROW_CHUNK * (r + 1) + 2 * HALO, :]
            ya_scr[r] = _dot(u, wa)
            yg_scr[r] = _dot(u, wg)

        up(0)
        for r in range(nchunk):
            if r + 1 < nchunk:
                up(r + 1)
            act = (_silu(_conv3(yg_scr.at[r], cg, *edges[r])) * _conv3(ya_scr.at[r], ca, *edges[r])).astype(BF16)
            acc_scr[ROW_CHUNK * r:ROW_CHUNK * (r + 1), :] += _dot(act, wd)
        return carry

    lax.fori_loop(0, FFN_NBLK, ff_block, 0)

    y = _layernorm_rows(ALPHA * x_ref[...] + g2_ref[...] * acc_scr[...], lng_ref[...], lnb_ref[...])
    if split_out:
        is_ctx = i * tm < cfg.t_ctx

        @pl.when(is_ctx)
        def _():
            out_refs[0][...] = y

        @pl.when(jnp.logical_not(is_ctx))
        def _():
            out_refs[1][...] = y
    else:
        out_refs[0][...] = y


def _ffn(cfg, x, mod, layer, w_up, conv_w, w_down, ln_g, ln_b, split_out=False):
    tm = min(cfg.tm, FFN_TM)
    n_ctx_tiles = cfg.t_ctx // tm
    xp, xn = _halo_specs(cfg, tm, D_MODEL)
    nchunk = tm // ROW_CHUNK
    resident = pl.Buffered(1)
    if split_out:
        out_specs = [
            pl.BlockSpec((tm, D_MODEL), lambda i: (jnp.minimum(i, n_ctx_tiles - 1), 0)),
            pl.BlockSpec((tm, D_MODEL), lambda i: (jnp.maximum(i - n_ctx_tiles, 0), 0)),
        ]
        out_shape = [jax.ShapeDtypeStruct((cfg.t_ctx, D_MODEL), F32),
                     jax.ShapeDtypeStruct((cfg.t_all - cfg.t_ctx, D_MODEL), F32)]
    else:
        out_specs = pl.BlockSpec((tm, D_MODEL), lambda i: (i, 0))
        out_shape = jax.ShapeDtypeStruct((cfg.t_all, D_MODEL), F32)
    return pl.pallas_call(
        functools.partial(_ffn_kernel, cfg, tm, split_out),
        grid=(cfg.t_all // tm,),
        in_specs=[
            pl.BlockSpec((tm, D_MODEL), lambda i: (i, 0)),
            xp, xn,
            _mod_spec(cfg, layer, 4, tm),
            _mod_spec(cfg, layer, 3, tm),
            _mod_spec(cfg, layer, 5, tm),
            pl.BlockSpec((None, 2 * FFN_NBLK, D_MODEL, FFN_TF), lambda i: (layer, 0, 0, 0), pipeline_mode=resident),
            pl.BlockSpec((None, 2 * FFN_NBLK, 3, FFN_TF), lambda i: (layer, 0, 0, 0)),
            pl.BlockSpec((None, FFN_NBLK, FFN_TF, D_MODEL), lambda i: (layer, 0, 0, 0), pipeline_mode=resident),
            pl.BlockSpec((None, 1, D_MODEL), lambda i: (layer, 0, 0)),
            pl.BlockSpec((None, 1, D_MODEL), lambda i: (layer, 0, 0)),
        ],
        out_specs=out_specs,
        out_shape=out_shape,
        scratch_shapes=[
            pltpu.VMEM((tm + 2 * HALO, D_MODEL), BF16),
            pltpu.VMEM((nchunk, ROW_CHUNK + 2 * HALO, FFN_TF), F32),
            pltpu.VMEM((nchunk, ROW_CHUNK + 2 * HALO, FFN_TF), F32),
            pltpu.VMEM((tm, D_MODEL), F32),
        ],
        compiler_params=_cparams(("arbitrary",)),
        name=f"ffn_l{layer}",
    )(x, x, x, mod, mod, mod, w_up, conv_w, w_down, ln_g, ln_b)


def _grid_pos_embed(n_tokens):
    rows = n_tokens // GRID_W
    r = jnp.broadcast_to(jnp.arange(rows, dtype=F32)[:, None], (rows, GRID_W)).reshape(-1)
    col = jnp.broadcast_to(jnp.arange(GRID_W, dtype=F32)[None, :], (rows, GRID_W)).reshape(-1)
    quarter = D_MODEL // 4
    freq = jnp.exp(-math.log(10000.0) * jnp.arange(quarter, dtype=F32) / quarter)
    ra, ca = r[:, None] * freq, col[:, None] * freq
    return jnp.concatenate([jnp.sin(ra), jnp.cos(ra), jnp.sin(ca), jnp.cos(ca)], -1)


def kernel(x_prompt, x_sample, state_gla, state_dn, c, c_ctx, w_mod, b_mod, ln1_g, ln1_b, ln2_g, ln2_b,
           a_w_in, a_w_gate, a_b_gate, a_norm, b_proj, b_scale, a_w_out,
           c_w_in, c_conv, c_a_log, c_dt_bias, c_norm, c_w_out, f_w_up, f_conv, f_w_down):
    n_ctx, seq, _ = x_prompt.shape
    n_smp, smp_len, _ = x_sample.shape
    assert seq == SEQ_TILE and n_smp + 1 <= MOD_ROWS
    cfg = _Cfg(n_ctx, n_smp, smp_len)

    x = _embed(cfg, x_prompt.reshape(-1, D_MODEL), x_sample.reshape(-1, D_MODEL), _grid_pos_embed(smp_len))
    cond = jnp.concatenate([c_ctx[None, :], c, jnp.zeros((MOD_ROWS - 1 - n_smp, D_MODEL), F32)], axis=0)
    mod = _modulation(cond, w_mod, b_mod)

    gla_consts = _gla_constants()
    tri = _tri_ones()
    row3 = lambda t: t.reshape(t.shape[0], 1, t.shape[1])
    ln1g, ln1b, ln2g, ln2b = row3(ln1_g), row3(ln1_b), row3(ln2_g), row3(ln2_b)
    s0_gla = state_gla.astype(F32).reshape(n_smp, -1, 2, A_HEADS // 2, 128, A_DV)
    s0_dn = state_dn.astype(F32)
    w_up_blk = _cast_blocks(f_w_up, FFN_TF, 2 * D_FF)
    w_down_blk = _cast_blocks(f_w_down, D_MODEL, D_MODEL).reshape(DEPTH, FFN_NBLK, FFN_TF, D_MODEL)
    ffn_conv = f_conv.reshape(DEPTH, 3, 2 * FFN_NBLK, FFN_TF).transpose(0, 2, 1, 3)
    gla_states, dn_states = [], []
    for layer in range(DEPTH):
        if layer % 2 == 0:
            e = layer // 2
            w_in = a_w_in[e]
            w_main = jnp.concatenate([w_in[:, :2 * A_QK + 2 * A_WIDTH], w_in[:, -B_WIDTH:]], axis=1)
            lr0 = 2 * A_QK + 2 * A_WIDTH
            w_lr = jnp.pad(w_in[:, lr0:lr0 + 2 * A_RANK], ((0, 0), (0, 128 - 2 * A_RANK)))
            w_gate_bd = jnp.zeros((128, 2 * A_QK), F32)
            w_gate_bd = w_gate_bd.at[:A_RANK, :A_QK].set(a_w_gate[e, 0]).at[A_RANK:2 * A_RANK, A_QK:].set(a_w_gate[e, 1])
            proj = _proj(cfg, x, mod, layer, w_main, 1024)
            gates = _gates(cfg, x, mod, layer, w_lr, w_gate_bd, a_b_gate[e].reshape(1, 2 * A_QK))
            of, ob, st = _gla(cfg, proj, gates, s0_gla, e, gla_consts)
            gla_states.append(st.reshape(n_ctx, 2, A_HEADS, A_DK, A_DV))
            x = _out_even(cfg, of, ob, proj, x, mod, layer, row3(a_norm), b_proj, row3(b_scale), a_w_out, ln1g, ln1b)
        else:
            o_ = layer // 2
            w_zab = jnp.pad(c_w_in[o_][:, 3 * C_WIDTH:], ((0, 0), (0, 128 - 4 * C_HEADS)))
            qkv = _qkv_conv(cfg, x, mod, layer, c_w_in, c_conv, o_)
            zab = _proj(cfg, x, mod, layer, w_zab, w_zab.shape[1])
            pad16 = lambda t: jnp.pad(t.reshape(1, -1), ((0, 0), (0, 128 - 2 * C_HEADS)))
            of, ob, st = _delta(cfg, qkv, zab, s0_dn, o_, pad16(c_a_log[o_]), pad16(c_dt_bias[o_]), tri)
            dn_states.append(st)
            x = _out_odd(cfg, of, ob, zab, x, mod, layer, row3(c_norm), c_w_out, ln1g, ln1b)
        x = _ffn(cfg, x, mod, layer, w_up_blk, ffn_conv, w_down_blk, ln2g, ln2b, split_out=layer == DEPTH - 1)

    y_prompt = x[0].reshape(n_ctx, seq, D_MODEL)
    y_sample = x[1].reshape(n_smp, smp_len, D_MODEL)
    return (y_prompt, y_sample, jnp.stack(gla_states, axis=1).astype(x_prompt.dtype),
            jnp.stack(dn_states, axis=1).astype(x_prompt.dtype))
```

```python
import functools
import math

import numpy as np
import jax
import jax.numpy as jnp
from jax import lax
from jax.experimental import pallas as pl
from jax.experimental.pallas import tpu as pltpu

F32 = jnp.float32
BF16 = jnp.bfloat16

D_MODEL = 1024
DEPTH = 4
GRID_W = 64
A_HEADS = 4
A_DK = 64
A_DV = 128
A_QK = A_HEADS * A_DK
A_WIDTH = A_HEADS * A_DV
A_RANK = 16
A_GATE_NORM = 16.0
B_WIDTH = 512
POOL_WINDOWS = (2, 4, 8, 16)
B_GW = 128
C_HEADS = 8
C_DK = 128
C_WIDTH = 1024
D_FF = 2816
N_MOD = 6
ALPHA = (2 * DEPTH) ** 0.25
EPS = 1e-6

SEQ_TILE = 256
HALO = 8
ROW_CHUNK = 256
assert ROW_CHUNK == SEQ_TILE
MOD_ROWS = 8
V7X_VMEM_LIMIT = 56 * 1024 * 1024


def _cparams(sem):
    return pltpu.CompilerParams(dimension_semantics=sem, vmem_limit_bytes=V7X_VMEM_LIMIT)


class _Cfg:
    def __init__(self, n_ctx, n_smp, smp_len):
        assert smp_len % SEQ_TILE == 0 and smp_len & (smp_len - 1) == 0
        self.n_ctx, self.n_smp, self.smp_len = n_ctx, n_smp, smp_len
        self.t_ctx = n_ctx * SEQ_TILE
        self.t_all = self.t_ctx + n_smp * smp_len
        self.tps = smp_len // SEQ_TILE
        self.n_tiles = self.t_all // SEQ_TILE
        self.n_seq = n_ctx + n_smp
        tm = 1024
        while self.t_ctx % tm or smp_len % tm:
            tm //= 2
        self.tm = tm

    def cond_of_row(self, g):
        return jnp.where(g < self.t_ctx, 0, 1 + (g - self.t_ctx) // self.smp_len)

    def seq_of_tile(self, i):
        return jnp.where(i < self.n_ctx, i, self.n_ctx + (i - self.n_ctx) // self.tps)

    def rev_tile(self, i):
        m = (i - self.n_ctx) % self.tps
        return jnp.where(i < self.n_ctx, i, i - m + (self.tps - 1 - m))

    def chunk_edges(self, row0):
        is_ctx = row0 < self.t_ctx
        length = jnp.where(is_ctx, SEQ_TILE, self.smp_len)
        pos0 = jnp.bitwise_and(jnp.where(is_ctx, row0, row0 - self.t_ctx), length - 1)
        return pos0 == 0, pos0 + ROW_CHUNK == length

    def first_of_seq(self, i):
        return jnp.logical_or(i < self.n_ctx, (i - self.n_ctx) % self.tps == 0)

    def last_of_seq(self, i):
        return jnp.logical_or(i < self.n_ctx, (i - self.n_ctx) % self.tps == self.tps - 1)


def _mod_map(cfg, layer, which, rows):
    def index_map(i, *_):
        return ((layer * MOD_ROWS + cfg.cond_of_row(i * rows)) * N_MOD + which, 0, 0)
    return index_map


def _mod_spec(cfg, layer, which, rows):
    return pl.BlockSpec((None, 1, D_MODEL), _mod_map(cfg, layer, which, rows))


def _halo_specs(cfg, rows, width, col_block=0):
    per = rows // HALO
    last = cfg.t_all // HALO - 1
    prev = pl.BlockSpec((HALO, width), lambda i, *_: (jnp.maximum(i * per - 1, 0), col_block))
    nxt = pl.BlockSpec((HALO, width), lambda i, *_: (jnp.minimum((i + 1) * per, last), col_block))
    return prev, nxt


def _seq_edges(cfg, tile, rows):
    g = tile * rows + lax.broadcasted_iota(jnp.int32, (rows, 1), 0)
    is_ctx = g < cfg.t_ctx
    length = jnp.where(is_ctx, SEQ_TILE, cfg.smp_len)
    pos = jnp.bitwise_and(jnp.where(is_ctx, g, g - cfg.t_ctx), length - 1)
    return pos, length


def _silu(x):
    return x * jax.nn.sigmoid(x)


def _dot(a, b):
    return jnp.dot(a, b, preferred_element_type=F32)


def _dot_nt(a, b):
    return lax.dot_general(a, b, (((1,), (1,)), ((), ())), preferred_element_type=F32)


def _split_bf16(x):
    hi = x.astype(BF16)
    lo = (x - hi.astype(F32)).astype(BF16)
    return hi, lo


def _conv3(y_ref, cw, starts_seq, ends_seq):
    n = ROW_CHUNK
    y_ref[HALO - 1:HALO, :] = jnp.where(starts_seq, 0.0, y_ref[HALO - 1:HALO, :])
    y_ref[HALO + n:HALO + n + 1, :] = jnp.where(ends_seq, 0.0, y_ref[HALO + n:HALO + n + 1, :])
    return (y_ref[HALO - 1:HALO - 1 + n, :] * cw[0:1, :]
            + y_ref[HALO:HALO + n, :] * cw[1:2, :]
            + y_ref[HALO + 1:HALO + 1 + n, :] * cw[2:3, :])


def _layernorm_rows(x, g, b):
    mu = jnp.mean(x, axis=-1, keepdims=True)
    xc = x - mu
    var = jnp.mean(xc * xc, axis=-1, keepdims=True)
    return xc * lax.rsqrt(var + EPS) * g + b


def _state_specs(cfg, slot, per_seq):
    zeros = (0,) * len(per_seq)
    n_ctx = cfg.n_ctx
    in_spec = pl.BlockSpec((None, None) + per_seq,
                           lambda i: (jnp.maximum(cfg.seq_of_tile(i) - n_ctx, 0), slot) + zeros)
    out_spec = pl.BlockSpec((None,) + per_seq, lambda i: (jnp.minimum(cfg.seq_of_tile(i), n_ctx - 1),) + zeros)
    return in_spec, out_spec


def _load_states(cfg, i, s0_ref, sf_scr, sb_scr):
    @pl.when(cfg.first_of_seq(i))
    def _():
        is_ctx = i < cfg.n_ctx
        sf_scr[...] = jnp.where(is_ctx, 0.0, s0_ref[0])
        sb_scr[...] = jnp.where(is_ctx, 0.0, s0_ref[1])


def _store_states(cfg, i, st_ref, sf_scr, sb_scr):
    @pl.when(i < cfg.n_ctx)
    def _():
        st_ref[0] = sf_scr[...]
        st_ref[1] = sb_scr[...]


def _embed_kernel(n_ctx_tiles, xp_ref, xs_ref, pe_ref, o_ref):
    i = pl.program_id(0)

    @pl.when(i < n_ctx_tiles)
    def _():
        o_ref[...] = xp_ref[...]

    @pl.when(i >= n_ctx_tiles)
    def _():
        o_ref[...] = xs_ref[...] + pe_ref[...]


def _embed(cfg, xp, xs, pe):
    tm = cfg.tm
    n_ctx_tiles, tps = cfg.t_ctx // tm, cfg.smp_len // tm
    return pl.pallas_call(
        functools.partial(_embed_kernel, n_ctx_tiles),
        grid=(cfg.t_all // tm,),
        in_specs=[
            pl.BlockSpec((tm, D_MODEL), lambda i: (jnp.minimum(i, n_ctx_tiles - 1), 0)),
            pl.BlockSpec((tm, D_MODEL), lambda i: (jnp.maximum(i - n_ctx_tiles, 0), 0)),
            pl.BlockSpec((tm, D_MODEL), lambda i: (jnp.maximum(i - n_ctx_tiles, 0) % tps, 0)),
        ],
        out_specs=pl.BlockSpec((tm, D_MODEL), lambda i: (i, 0)),
        out_shape=jax.ShapeDtypeStruct((cfg.t_all, D_MODEL), F32),
        compiler_params=_cparams(("arbitrary",)),
        name="embed",
    )(xp, xs, pe)


CAST_ROWS = 256


def _cast_blocks_kernel(tn, nblk, col0, w_ref, o_ref):
    for c in range(nblk):
        o_ref[c] = w_ref[:, col0 + tn * c:col0 + tn * (c + 1)].astype(BF16)


def _cast_blocks(w, tn, ncols, col0=0):
    n_l, k, n = w.shape
    nblk = ncols // tn
    return pl.pallas_call(
        functools.partial(_cast_blocks_kernel, tn, nblk, col0),
        grid=(n_l, k // CAST_ROWS),
        in_specs=[pl.BlockSpec((None, CAST_ROWS, n), lambda l, r: (l, r, 0))],
        out_specs=pl.BlockSpec((None, nblk, CAST_ROWS, tn), lambda l, r: (l, 0, r, 0)),
        out_shape=jax.ShapeDtypeStruct((n_l, nblk, k, tn), BF16),
        compiler_params=_cparams(("arbitrary", "arbitrary")),
        name=f"cast_blocks_{n}",
    )(w)


def _mod_kernel(c_ref, w_ref, b_ref, o_ref):
    sc = _silu(c_ref[...]).astype(BF16)
    o_ref[...] = _dot(sc, w_ref[...].astype(BF16)) + b_ref[...]


def _modulation(cond, w_mod, b_mod):
    out = pl.pallas_call(
        _mod_kernel,
        grid=(DEPTH, N_MOD),
        in_specs=[
            pl.BlockSpec((MOD_ROWS, D_MODEL), lambda l, j: (0, 0)),
            pl.BlockSpec((None, D_MODEL, D_MODEL), lambda l, j: (l, 0, j)),
            pl.BlockSpec((None, 1, D_MODEL), lambda l, j: (l, 0, j)),
        ],
        out_specs=pl.BlockSpec((None, MOD_ROWS, D_MODEL), lambda l, j: (l, 0, j)),
        out_shape=jax.ShapeDtypeStruct((DEPTH, MOD_ROWS, N_MOD * D_MODEL), F32),
        compiler_params=_cparams(("arbitrary", "arbitrary")),
        name="modulation",
    )(cond, w_mod, b_mod.reshape(DEPTH, 1, N_MOD * D_MODEL))
    return out.reshape(DEPTH * MOD_ROWS * N_MOD, 1, D_MODEL)


def _proj_kernel(x_ref, sc_ref, sh_ref, w_ref, o_ref, u_scr):
    @pl.when(pl.program_id(1) == 0)
    def _():
        u_scr[...] = (x_ref[...] * (1.0 + sc_ref[...]) + sh_ref[...]).astype(BF16)

    o_ref[...] = _dot(u_scr[...], w_ref[...].astype(BF16))


def _proj(cfg, x, mod, layer, w, tn):
    n = w.shape[1]
    tm = cfg.tm
    return pl.pallas_call(
        _proj_kernel,
        grid=(cfg.t_all // tm, n // tn),
        in_specs=[
            pl.BlockSpec((tm, D_MODEL), lambda i, j: (i, 0)),
            _mod_spec(cfg, layer, 1, tm),
            _mod_spec(cfg, layer, 0, tm),
            pl.BlockSpec((D_MODEL, tn), lambda i, j: (0, j)),
        ],
        out_specs=pl.BlockSpec((tm, tn), lambda i, j: (i, j)),
        out_shape=jax.ShapeDtypeStruct((cfg.t_all, n), F32),
        scratch_shapes=[pltpu.VMEM((tm, D_MODEL), BF16)],
        compiler_params=_cparams(("arbitrary", "arbitrary")),
        name=f"proj_l{layer}",
    )(x, mod, mod, w)


def _proj_gates_kernel(x_ref, sc_ref, sh_ref, w_ref, wlr_ref, wg_ref, bg_ref, o_ref, g_ref, u_scr):
    @pl.when(pl.program_id(1) == 0)
    def _():
        u = (x_ref[...] * (1.0 + sc_ref[...]) + sh_ref[...]).astype(BF16)
        u_scr[...] = u
        lr = _dot(u, wlr_ref[...].astype(BF16))
        z = _dot(lr.astype(BF16), wg_ref[...].astype(BF16)) + bg_ref[...]
        g_ref[...] = (jnp.minimum(z, 0.0) - jnp.log1p(jnp.exp(-jnp.abs(z)))) / A_GATE_NORM

    o_ref[...] = _dot(u_scr[...], w_ref[...].astype(BF16))


def _proj_gates(cfg, x, mod, layer, w, tn, w_lr, w_gate_bd, b_gate):
    n = w.shape[1]
    tm = cfg.tm
    return pl.pallas_call(
        _proj_gates_kernel,
        grid=(cfg.t_all // tm, n // tn),
        in_specs=[
            pl.BlockSpec((tm, D_MODEL), lambda i, j: (i, 0)),
            _mod_spec(cfg, layer, 1, tm),
            _mod_spec(cfg, layer, 0, tm),
            pl.BlockSpec((D_MODEL, tn), lambda i, j: (0, j)),
            pl.BlockSpec((D_MODEL, 128), lambda i, j: (0, 0)),
            pl.BlockSpec((128, 2 * A_QK), lambda i, j: (0, 0)),
            pl.BlockSpec((1, 2 * A_QK), lambda i, j: (0, 0)),
        ],
        out_specs=[
            pl.BlockSpec((tm, tn), lambda i, j: (i, j)),
            pl.BlockSpec((tm, 2 * A_QK), lambda i, j: (i, 0)),
        ],
        out_shape=[
            jax.ShapeDtypeStruct((cfg.t_all, n), F32),
            jax.ShapeDtypeStruct((cfg.t_all, 2 * A_QK), F32),
        ],
        scratch_shapes=[pltpu.VMEM((tm, D_MODEL), BF16)],
        compiler_params=_cparams(("arbitrary", "arbitrary")),
        name=f"proj_gates_l{layer}",
    )(x, mod, mod, w, w_lr, w_gate_bd, b_gate)


GLA_LEVELS = 8
GLA_STACK = GLA_LEVELS + 2


def _gla_constants():
    n = SEQ_TILE
    idx = np.arange(n)
    i, m = idx[:, None], idx[None, :]
    stack = np.zeros((2, GLA_STACK, n, n), np.float32)
    masks = np.zeros((2, GLA_LEVELS + 1, n, n), np.float32)
    stack[0, 0] = m <= i
    stack[0, 1] = m > i
    stack[1, 0] = m >= i
    stack[1, 1] = m < i
    for l in range(GLA_LEVELS):
        s = 1 << l
        blk = idx // (2 * s)
        upper = (idx // s) % 2 == 1
        piv_f = blk * 2 * s + s - 1
        piv_b = blk * 2 * s + s
        up, pf, pb = upper[:, None], piv_f[:, None], piv_b[:, None]
        stack[0, 2 + l] = np.where(up, (m > pf) & (m <= i), (m > i) & (m <= pf))
        stack[1, 2 + l] = np.where(up, (m >= pb) & (m < i), (m >= i) & (m < pb))
        same = blk[:, None] == blk[None, :]
        masks[0, l] = same & upper[:, None] & ~upper[None, :]
        masks[1, l] = same & ~upper[:, None] & upper[None, :]
    masks[:, GLA_LEVELS] = np.eye(n)
    return (jnp.asarray(stack.reshape(2, GLA_STACK * n, n), BF16), jnp.asarray(masks, F32))


def _gla_direction(qk, v, g, stack_ref, mask_ref, s_scr, o_ref):
    n = SEQ_TILE
    q = qk[:, :A_QK] * (A_DK ** -0.5)
    k = qk[:, A_QK:]
    g_hi, g_lo = _split_bf16(g)
    sums = _dot(stack_ref[...], g_hi) + _dot(stack_ref[...], g_lo)
    e_in = jnp.exp(sums[0:n])
    e_out = jnp.exp(sums[n:2 * n])
    lane = lax.broadcasted_iota(jnp.int32, (n, 128), 1)
    low_half = lane < A_DK

    att = [jnp.zeros((n, n), F32) for _ in range(A_HEADS)]
    for l in range(GLA_LEVELS + 1):
        if l < GLA_LEVELS:
            e = jnp.exp(sums[(2 + l) * n:(3 + l) * n])
            qs, ks = q * e, k * e
        else:
            qs, ks = q, k
        m = mask_ref[l]
        for p in range(A_HEADS // 2):
            qp = qs[:, 128 * p:128 * (p + 1)]
            kp = ks[:, 128 * p:128 * (p + 1)].astype(BF16)
            for hh in range(2):
                qh = jnp.where(low_half if hh == 0 else ~low_half, qp, 0.0).astype(BF16)
                att[2 * p + hh] = att[2 * p + hh] + m * _dot_nt(qh, kp)

    qd = q * e_in
    kd = k * e_out
    kd_t = kd.T.astype(BF16)
    g_t = g.T
    gt_hi, gt_lo = _split_bf16(g_t)
    ones = jnp.ones((n, 128), BF16)
    dec = jnp.exp(_dot(gt_hi, ones) + _dot(gt_lo, ones))
    for h in range(A_HEADS):
        p, hh = h // 2, h % 2
        v_h = v[:, A_DV * h:A_DV * (h + 1)].astype(BF16)
        s_pair = s_scr[p]
        qp = qd[:, 128 * p:128 * (p + 1)]
        qh = jnp.where(low_half if hh == 0 else ~low_half, qp, 0.0).astype(BF16)
        o_h = _dot(att[h].astype(BF16), v_h) + _dot(qh, s_pair.astype(BF16))
        o_ref[:, A_DV * h:A_DV * (h + 1)] = o_h
    for h in range(A_HEADS):
        p, hh = h // 2, h % 2
        v_h = v[:, A_DV * h:A_DV * (h + 1)].astype(BF16)
        rows = slice(A_DK * hh, A_DK * (hh + 1))
        s_old = s_scr[p, rows, :]
        s_scr[p, rows, :] = s_old * dec[A_DK * h:A_DK * (h + 1), :] + _dot(kd_t[A_DK * h:A_DK * (h + 1), :], v_h)


def _gla_kernel(cfg, qk_f, v_f, g_f, qk_b, v_b, g_b, s0_ref, stack_ref, mask_ref,
                of_ref, ob_ref, st_ref, sf_scr, sb_scr):
    i = pl.program_id(0)

    _load_states(cfg, i, s0_ref, sf_scr, sb_scr)

    _gla_direction(qk_f[...], v_f[...], g_f[...], stack_ref.at[0], mask_ref.at[0], sf_scr, of_ref)
    _gla_direction(qk_b[...], v_b[...], g_b[...], stack_ref.at[1], mask_ref.at[1], sb_scr, ob_ref)

    _store_states(cfg, i, st_ref, sf_scr, sb_scr)


def _gla(cfg, proj, gates, s0, slot, consts):
    stack, masks = consts
    rev = cfg.rev_tile
    n = SEQ_TILE
    st_shape = (cfg.n_ctx, 2, A_HEADS // 2, 128, A_DV)
    s0_spec, st_spec = _state_specs(cfg, slot, st_shape[1:])
    return pl.pallas_call(
        functools.partial(_gla_kernel, cfg),
        grid=(cfg.n_tiles,),
        in_specs=[
            pl.BlockSpec((n, 2 * A_QK), lambda i: (i, 0)),
            pl.BlockSpec((n, A_WIDTH), lambda i: (i, 1)),
            pl.BlockSpec((n, A_QK), lambda i: (i, 0)),
            pl.BlockSpec((n, 2 * A_QK), lambda i: (rev(i), 0)),
            pl.BlockSpec((n, A_WIDTH), lambda i: (rev(i), 1)),
            pl.BlockSpec((n, A_QK), lambda i: (rev(i), 1)),
            s0_spec,
            pl.BlockSpec(stack.shape, lambda i: (0, 0, 0)),
            pl.BlockSpec(masks.shape, lambda i: (0, 0, 0, 0)),
        ],
        out_specs=[
            pl.BlockSpec((n, A_WIDTH), lambda i: (i, 0)),
            pl.BlockSpec((n, A_WIDTH), lambda i: (rev(i), 0)),
            st_spec,
        ],
        out_shape=[
            jax.ShapeDtypeStruct((cfg.t_all, A_WIDTH), F32),
            jax.ShapeDtypeStruct((cfg.t_all, A_WIDTH), F32),
            jax.ShapeDtypeStruct(st_shape, F32),
        ],
        scratch_shapes=[pltpu.VMEM(st_shape[2:], F32), pltpu.VMEM(st_shape[2:], F32)],
        compiler_params=_cparams(("arbitrary",)),
        name="gla_scan",
    )(proj, proj, gates, proj, proj, gates, s0, stack, masks)


def _rms_heads(o, gain, n_heads, width):
    parts = []
    for h in range(n_heads):
        oh = o[:, width * h:width * (h + 1)]
        parts.append(oh * lax.rsqrt(jnp.mean(oh * oh, axis=-1, keepdims=True) + EPS) * gain)
    return parts


def _out_even_kernel(cfg, of_ref, ob_ref, r_ref, pz_ref, pzp_ref, pzn_ref, x_ref, g1_ref,
                     norm_ref, bproj_ref, bscale_ref, wout_ref, lng_ref, lnb_ref, o_ref, ext_scr, w_scr):
    n = SEQ_TILE
    i = pl.program_id(0)

    @pl.when(i == 0)
    def _():
        w_scr[...] = wout_ref[...].astype(BF16)

    o = of_ref[...] + ob_ref[...]
    r = r_ref[...]
    heads = _rms_heads(o, norm_ref[...], A_HEADS, A_DV)
    parts = [(heads[h] * _silu(r[:, A_DV * h:A_DV * (h + 1)])).astype(BF16) for h in range(A_HEADS)]

    pos, length = _seq_edges(cfg, i, n)
    starts_seq, ends_seq = cfg.chunk_edges(i * n)
    ext_scr[0:HALO, :] = jnp.where(starts_seq, 0.0, pzp_ref[...])
    ext_scr[HALO:HALO + n, :] = pz_ref[...]
    ext_scr[HALO + n:, :] = jnp.where(ends_seq, 0.0, pzn_ref[...])
    for gi, win in enumerate(POOL_WINDOWS):
        lo = win // 2
        hi = win - 1 - lo
        cols = slice(B_GW * gi, B_GW * (gi + 1))
        acc = ext_scr[HALO - lo:HALO - lo + n, cols]
        for d in range(-lo + 1, hi + 1):
            acc = acc + ext_scr[HALO + d:HALO + d + n, cols]
        cnt = (jnp.minimum(pos + hi + 1, length) - jnp.maximum(pos - lo, 0)).astype(F32)
        pooled = (acc / cnt - ext_scr[HALO:HALO + n, cols]).astype(BF16)
        mixed = _dot(pooled, bproj_ref[gi].astype(BF16)) * bscale_ref[:, cols]
        parts.append(mixed.astype(BF16))

    y = _dot(jnp.concatenate(parts, axis=1), w_scr[...])
    o_ref[...] = _layernorm_rows(ALPHA * x_ref[...] + g1_ref[...] * y, lng_ref[...], lnb_ref[...])


def _out_even(cfg, of, ob, proj, x, mod, layer, norm_g, b_proj, b_scale, w_out, ln_g, ln_b):
    n, e = SEQ_TILE, layer // 2
    pz_prev, pz_next = _halo_specs(cfg, n, B_WIDTH, col_block=3)
    return pl.pallas_call(
        functools.partial(_out_even_kernel, cfg),
        grid=(cfg.n_tiles,),
        in_specs=[
            pl.BlockSpec((n, A_WIDTH), lambda i: (i, 0)),
            pl.BlockSpec((n, A_WIDTH), lambda i: (i, 0)),
            pl.BlockSpec((n, A_WIDTH), lambda i: (i, 2)),
            pl.BlockSpec((n, B_WIDTH), lambda i: (i, 3)),
            pz_prev, pz_next,
            pl.BlockSpec((n, D_MODEL), lambda i: (i, 0)),
            _mod_spec(cfg, layer, 2, n),
            pl.BlockSpec((None, 1, A_DV), lambda i: (e, 0, 0)),
            pl.BlockSpec((None, len(POOL_WINDOWS), B_GW, B_GW), lambda i: (e, 0, 0, 0)),
            pl.BlockSpec((None, 1, B_WIDTH), lambda i: (e, 0, 0)),
            pl.BlockSpec((None, D_MODEL, D_MODEL), lambda i: (e, 0, 0)),
            pl.BlockSpec((None, 1, D_MODEL), lambda i: (layer, 0, 0)),
            pl.BlockSpec((None, 1, D_MODEL), lambda i: (layer, 0, 0)),
        ],
        out_specs=pl.BlockSpec((n, D_MODEL), lambda i: (i, 0)),
        out_shape=jax.ShapeDtypeStruct((cfg.t_all, D_MODEL), F32),
        scratch_shapes=[pltpu.VMEM((n + 2 * HALO, B_WIDTH), F32), pltpu.VMEM((D_MODEL, D_MODEL), BF16)],
        compiler_params=_cparams(("arbitrary",)),
        name=f"out_even_l{layer}",
    )(of, ob, proj, proj, proj, proj, x, mod, norm_g, b_proj, b_scale, w_out, ln_g, ln_b)


def _qkv_conv_kernel(cfg, tm, tn, x_ref, xp_ref, xn_ref, sc_ref, sh_ref, w_ref, cw_ref, o_ref, u_scr, y_scr):
    i, j = pl.program_id(0), pl.program_id(1)

    @pl.when(j == 0)
    def _():
        sc, sh = 1.0 + sc_ref[...], sh_ref[...]
        u_scr[0:HALO, :] = (xp_ref[...] * sc + sh).astype(BF16)
        u_scr[HALO:HALO + tm, :] = (x_ref[...] * sc + sh).astype(BF16)
        u_scr[HALO + tm:, :] = (xn_ref[...] * sc + sh).astype(BF16)

    w = w_ref[...].astype(BF16)
    cw = cw_ref[...]
    is_q = j == 0
    is_qk = j < 2
    nchunk = tm // ROW_CHUNK

    def up(r):
        y_scr[r] = _dot(u_scr[ROW_CHUNK * r:ROW_CHUNK * (r + 1) + 2 * HALO, :], w)

    up(0)
    for r in range(nchunk):
        if r + 1 < nchunk:
            up(r + 1)
        rows = slice(ROW_CHUNK * r, ROW_CHUNK * (r + 1))
        h = _silu(_conv3(y_scr.at[r], cw, *cfg.chunk_edges(i * tm + ROW_CHUNK * r)))
        for c in range(tn // C_DK):
            hc = h[:, C_DK * c:C_DK * (c + 1)]
            inv = lax.rsqrt(jnp.sum(hc * hc, axis=-1, keepdims=True) + EPS)
            scale = jnp.where(is_qk, inv * jnp.where(is_q, C_DK ** -0.5, 1.0), 1.0)
            o_ref[rows, C_DK * c:C_DK * (c + 1)] = hc * scale


def _qkv_conv(cfg, x, mod, layer, w_in, conv_w, idx):
    tm, tn = cfg.tm, C_WIDTH
    n = 3 * C_WIDTH
    xp, xn = _halo_specs(cfg, tm, D_MODEL)
    return pl.pallas_call(
        functools.partial(_qkv_conv_kernel, cfg, tm, tn),
        grid=(cfg.t_all // tm, n // tn),
        in_specs=[
            pl.BlockSpec((tm, D_MODEL), lambda i, j: (i, 0)),
            xp, xn,
            _mod_spec(cfg, layer, 1, tm),
            _mod_spec(cfg, layer, 0, tm),
            pl.BlockSpec((None, D_MODEL, tn), lambda i, j: (idx, 0, j)),
            pl.BlockSpec((None, 3, tn), lambda i, j: (idx, 0, j)),
        ],
        out_specs=pl.BlockSpec((tm, tn), lambda i, j: (i, j)),
        out_shape=jax.ShapeDtypeStruct((cfg.t_all, n), F32),
        scratch_shapes=[pltpu.VMEM((tm + 2 * HALO, D_MODEL), BF16),
                        pltpu.VMEM((tm // ROW_CHUNK, ROW_CHUNK + 2 * HALO, tn), F32)],
        compiler_params=_cparams(("arbitrary", "arbitrary")),
        name=f"qkv_conv_l{layer}",
    )(x, x, x, mod, mod, w_in, conv_w)


DN_BLOCK = 64
DN_NBLK = SEQ_TILE // DN_BLOCK


def _tri_ones():
    n = SEQ_TILE
    idx = np.arange(n)
    lower = (idx[None, :] <= idx[:, None]).astype(np.float32)
    return jnp.asarray(np.stack([lower, lower.T]), BF16)


def _unit_tri_inverses(packs):
    n, w = DN_BLOCK, SEQ_TILE
    row = lax.broadcasted_iota(jnp.int32, (n, w), 0)
    col = jnp.bitwise_and(lax.broadcasted_iota(jnp.int32, (n, w), 1), n - 1)
    eye = (row == col).astype(F32)
    band = lax.broadcasted_iota(jnp.int32, (w, w), 0) // n
    keep = (band == lax.broadcasted_iota(jnp.int32, (w, w), 1) // n).astype(BF16)

    def block_diag(x):
        return jnp.concatenate([x.astype(BF16)] * DN_NBLK, axis=0) * keep

    ts = [eye - jnp.where((row >> 1) == (col >> 1), a, 0.0) for a in packs]
    for l in range(1, int(math.log2(n))):
        couple = jnp.logical_and((row >> (l + 1)) == (col >> (l + 1)), (row >> l) != (col >> l))
        ws = [_dot(jnp.where(couple, a, 0.0).astype(BF16), block_diag(t)) for a, t in zip(packs, ts)]
        ts = [t - _dot(t.astype(BF16), block_diag(wv)) for t, wv in zip(ts, ws)]
    return ts


def _delta_tile(dirs):
    n = SEQ_TILE
    row = lax.broadcasted_iota(jnp.int32, (n, n), 0)
    col = lax.broadcasted_iota(jnp.int32, (n, n), 1)
    probs = []
    for d, (q_ref, k_ref, v_ref, g, beta, tri_ref, s_scr, o_ref) in enumerate(dirs):
        incl = (col <= row) if d == 0 else (col >= row)
        strict = (col < row) if d == 0 else (col > row)
        last = n - 1 if d == 0 else 0
        g_hi, g_lo = _split_bf16(g)
        b_col = _dot(tri_ref[...], g_hi) + _dot(tri_ref[...], g_lo)
        b_row = b_col.T
        for h in range(C_HEADS):
            c = d * C_HEADS + h
            cols = slice(C_DK * h, C_DK * (h + 1))
            q, k, v = q_ref[:, cols], k_ref[:, cols], v_ref[:, cols]
            bc = b_col[:, c:c + 1]
            br = b_row[c:c + 1, :]
            b_last = b_col[last:last + 1, c:c + 1]
            ld = jnp.exp(jnp.where(incl, bc - br, -jnp.inf))
            bt = beta[:, 2 * C_HEADS + c:2 * C_HEADS + c + 1]
            kb = k * bt
            kbf = k.astype(BF16)
            e_col = jnp.exp(bc)
            probs.append(dict(
                d=d, h=h, cols=cols, s_scr=s_scr, o_ref=o_ref,
                a_mat=jnp.where(strict, _dot_nt(kb.astype(BF16), kbf) * ld, 0.0),
                aqk=(_dot_nt(q.astype(BF16), kbf) * ld).astype(BF16),
                rhs=jnp.concatenate([v * bt, kb * e_col], axis=1),
                qd=(q * e_col).astype(BF16),
                kd_t=(k * jnp.exp(b_last - bc)).T.astype(BF16),
                dec=jnp.exp(b_last)))

    blk = lambda b: slice(DN_BLOCK * b, DN_BLOCK * (b + 1))
    on_diag = (row // DN_BLOCK) == (col // DN_BLOCK)
    packs = []
    for p in probs:
        dm = jnp.where(on_diag, p["a_mat"], 0.0)
        packs.append(sum(dm[blk(b), :] for b in range(1, DN_NBLK)) + dm[blk(0), :])
    t_inv = _unit_tri_inverses(packs)

    xs = [[None] * DN_NBLK for _ in probs]
    for step in range(DN_NBLK):
        for pi, p in enumerate(probs):
            b = step if p["d"] == 0 else DN_NBLK - 1 - step
            resid = p["rhs"][blk(b), :]
            for bb in range(DN_NBLK):
                if xs[pi][bb] is not None:
                    resid = resid - _dot(p["a_mat"][blk(b), blk(bb)].astype(BF16), xs[pi][bb].astype(BF16))
            p["resid"] = resid
        for pi, p in enumerate(probs):
            b = step if p["d"] == 0 else DN_NBLK - 1 - step
            xs[pi][b] = _dot(t_inv[pi][:, blk(b)].astype(BF16), p["resid"].astype(BF16))

    for pi, p in enumerate(probs):
        x = jnp.concatenate(xs[pi], axis=0)
        p["u"], p["w"] = x[:, :C_DK], x[:, C_DK:].astype(BF16)
        p["s"] = p["s_scr"][p["h"]]
        p["sb"] = p["s"].astype(BF16)
    for p in probs:
        p["vb"] = (p["u"] - _dot(p["w"], p["sb"])).astype(BF16)
    for p in probs:
        p["o_ref"][:, p["cols"]] = _dot(p["qd"], p["sb"]) + _dot(p["aqk"], p["vb"])
    for p in probs:
        p["s_scr"][p["h"]] = p["s"] * p["dec"] + _dot(p["kd_t"], p["vb"])


def _delta_kernel(cfg, qf, kf, vf, abf, qb, kb, vb, abb, s0_ref, alog_ref, dtb_ref, tri_ref,
                  of_ref, ob_ref, st_ref, sf_scr, sb_scr):
    i = pl.program_id(0)

    _load_states(cfg, i, s0_ref, sf_scr, sb_scr)

    lane = lax.broadcasted_iota(jnp.int32, (1, 128), 1)
    neg_a = jnp.where(lane < 2 * C_HEADS, -jnp.exp(alog_ref[...]), 0.0)

    def gate(ab):
        a_in = ab + dtb_ref[...]
        softplus = jnp.maximum(a_in, 0.0) + jnp.log1p(jnp.exp(-jnp.abs(a_in)))
        return neg_a * softplus, jax.nn.sigmoid(ab)

    g_f, beta_f = gate(abf[...])
    g_b, beta_b = gate(abb[...])
    _delta_tile([(qf, kf, vf, g_f, beta_f, tri_ref.at[0], sf_scr, of_ref),
                 (qb, kb, vb, g_b, beta_b, tri_ref.at[1], sb_scr, ob_ref)])

    _store_states(cfg, i, st_ref, sf_scr, sb_scr)


def _delta(cfg, qkv, zab, s0, slot, a_log, dt_bias, tri):
    rev = cfg.rev_tile
    n = SEQ_TILE
    st_shape = (cfg.n_ctx, 2, C_HEADS, C_DK, C_DK)
    s0_spec, st_spec = _state_specs(cfg, slot, st_shape[1:])
    ab_col = C_WIDTH // 128
    in_specs = []
    for tile_of in (lambda i: i, rev):
        for cb in range(3):
            in_specs.append(pl.BlockSpec((n, C_WIDTH), functools.partial(lambda i, f, c: (f(i), c), f=tile_of, c=cb)))
        in_specs.append(pl.BlockSpec((n, 128), functools.partial(lambda i, f: (f(i), ab_col), f=tile_of)))
    in_specs += [
        s0_spec,
        pl.BlockSpec((1, 128), lambda i: (0, 0)),
        pl.BlockSpec((1, 128), lambda i: (0, 0)),
        pl.BlockSpec(tri.shape, lambda i: (0, 0, 0)),
    ]
    return pl.pallas_call(
        functools.partial(_delta_kernel, cfg),
        grid=(cfg.n_tiles,),
        in_specs=in_specs,
        out_specs=[
            pl.BlockSpec((n, C_WIDTH), lambda i: (i, 0)),
            pl.BlockSpec((n, C_WIDTH), lambda i: (rev(i), 0)),
            st_spec,
        ],
        out_shape=[
            jax.ShapeDtypeStruct((cfg.t_all, C_WIDTH), F32),
            jax.ShapeDtypeStruct((cfg.t_all, C_WIDTH), F32),
            jax.ShapeDtypeStruct(st_shape, F32),
        ],
        scratch_shapes=[pltpu.VMEM(st_shape[2:], F32), pltpu.VMEM(st_shape[2:], F32)],
        compiler_params=_cparams(("arbitrary",)),
        name="delta_scan",
    )(qkv, qkv, qkv, zab, qkv, qkv, qkv, zab, s0, a_log, dt_bias, tri)


def _out_odd_kernel(of_ref, ob_ref, z_ref, x_ref, g1_ref, norm_ref, wout_ref, lng_ref, lnb_ref, o_ref, w_scr):
    @pl.when(pl.program_id(0) == 0)
    def _():
        w_scr[...] = wout_ref[...].astype(BF16)

    o = of_ref[...] + ob_ref[...]
    z = z_ref[...]
    heads = _rms_heads(o, norm_ref[...], C_HEADS, C_DK)
    parts = [(heads[h] * _silu(z[:, C_DK * h:C_DK * (h + 1)])).astype(BF16) for h in range(C_HEADS)]
    y = _dot(jnp.concatenate(parts, axis=1), w_scr[...])
    o_ref[...] = _layernorm_rows(ALPHA * x_ref[...] + g1_ref[...] * y, lng_ref[...], lnb_ref[...])


def _out_odd(cfg, of, ob, zab, x, mod, layer, norm_g, w_out, ln_g, ln_b):
    n, e = min(cfg.tm, 2 * SEQ_TILE), layer // 2
    return pl.pallas_call(
        _out_odd_kernel,
        grid=(cfg.t_all // n,),
        in_specs=[
            pl.BlockSpec((n, C_WIDTH), lambda i: (i, 0)),
            pl.BlockSpec((n, C_WIDTH), lambda i: (i, 0)),
            pl.BlockSpec((n, C_WIDTH), lambda i: (i, 0)),
            pl.BlockSpec((n, D_MODEL), lambda i: (i, 0)),
            _mod_spec(cfg, layer, 2, n),
            pl.BlockSpec((None, 1, C_DK), lambda i: (e, 0, 0)),
            pl.BlockSpec((None, D_MODEL, D_MODEL), lambda i: (e, 0, 0)),
            pl.BlockSpec((None, 1, D_MODEL), lambda i: (layer, 0, 0)),
            pl.BlockSpec((None, 1, D_MODEL), lambda i: (layer, 0, 0)),
        ],
        out_specs=pl.BlockSpec((n, D_MODEL), lambda i: (i, 0)),
        out_shape=jax.ShapeDtypeStruct((cfg.t_all, D_MODEL), F32),
        scratch_shapes=[pltpu.VMEM((D_MODEL, D_MODEL), BF16)],
        compiler_params=_cparams(("arbitrary",)),
        name=f"out_odd_l{layer}",
    )(of, ob, zab, x, mod, norm_g, w_out, ln_g, ln_b)


FFN_TF = 256
FFN_NBLK = D_FF // FFN_TF
FFN_TM = 1024


def _ffn_kernel(cfg, tm, split_out, x_ref, xp_ref, xn_ref, sc_ref, sh_ref, g2_ref, wup_ref, cw_ref,
                wd_ref, lng_ref, lnb_ref, *rest):
    out_refs, (u_scr, ya_scr, yg_scr, acc_scr) = rest[:-4], rest[-4:]
    i = pl.program_id(0)
    sc, sh = 1.0 + sc_ref[...], sh_ref[...]
    u_scr[0:HALO, :] = (xp_ref[...] * sc + sh).astype(BF16)
    u_scr[HALO:HALO + tm, :] = (x_ref[...] * sc + sh).astype(BF16)
    u_scr[HALO + tm:, :] = (xn_ref[...] * sc + sh).astype(BF16)
    acc_scr[...] = jnp.zeros_like(acc_scr)
    nchunk = tm // ROW_CHUNK
    edges = [cfg.chunk_edges(i * tm + ROW_CHUNK * r) for r in range(nchunk)]

    def ff_block(jb, carry):
        wa, wg, wd = wup_ref[jb], wup_ref[FFN_NBLK + jb], wd_ref[jb]
        ca, cg = cw_ref[jb], cw_ref[FFN_NBLK + jb]

        def up(r):
            u = u_scr[ROW_CHUNK * r:ROW_CHUNK * (r + 1) + 2 * HALO, :]
            ya_scr[r] = _dot(u, wa)
            yg_scr[r] = _dot(u, wg)

        up(0)
        for r in range(nchunk):
            if r + 1 < nchunk:
                up(r + 1)
            act = (_silu(_conv3(yg_scr.at[r], cg, *edges[r])) * _conv3(ya_scr.at[r], ca, *edges[r])).astype(BF16)
            acc_scr[ROW_CHUNK * r:ROW_CHUNK * (r + 1), :] += _dot(act, wd)
        return carry

    lax.fori_loop(0, FFN_NBLK, ff_block, 0)

    y = _layernorm_rows(ALPHA * x_ref[...] + g2_ref[...] * acc_scr[...], lng_ref[...], lnb_ref[...])
    if split_out:
        is_ctx = i * tm < cfg.t_ctx

        @pl.when(is_ctx)
        def _():
            out_refs[0][...] = y

        @pl.when(jnp.logical_not(is_ctx))
        def _():
            out_refs[1][...] = y
    else:
        out_refs[0][...] = y


def _ffn(cfg, x, mod, layer, w_up, conv_w, w_down, ln_g, ln_b, split_out=False):
    tm = min(cfg.tm, FFN_TM)
    n_ctx_tiles = cfg.t_ctx // tm
    xp, xn = _halo_specs(cfg, tm, D_MODEL)
    nchunk = tm // ROW_CHUNK
    resident = pl.Buffered(1)
    if split_out:
        out_specs = [
            pl.BlockSpec((tm, D_MODEL), lambda i: (jnp.minimum(i, n_ctx_tiles - 1), 0)),
            pl.BlockSpec((tm, D_MODEL), lambda i: (jnp.maximum(i - n_ctx_tiles, 0), 0)),
        ]
        out_shape = [jax.ShapeDtypeStruct((cfg.t_ctx, D_MODEL), F32),
                     jax.ShapeDtypeStruct((cfg.t_all - cfg.t_ctx, D_MODEL), F32)]
    else:
        out_specs = pl.BlockSpec((tm, D_MODEL), lambda i: (i, 0))
        out_shape = jax.ShapeDtypeStruct((cfg.t_all, D_MODEL), F32)
    return pl.pallas_call(
        functools.partial(_ffn_kernel, cfg, tm, split_out),
        grid=(cfg.t_all // tm,),
        in_specs=[
            pl.BlockSpec((tm, D_MODEL), lambda i: (i, 0)),
            xp, xn,
            _mod_spec(cfg, layer, 4, tm),
            _mod_spec(cfg, layer, 3, tm),
            _mod_spec(cfg, layer, 5, tm),
            pl.BlockSpec((None, 2 * FFN_NBLK, D_MODEL, FFN_TF), lambda i: (layer, 0, 0, 0), pipeline_mode=resident),
            pl.BlockSpec((None, 2 * FFN_NBLK, 3, FFN_TF), lambda i: (layer, 0, 0, 0)),
            pl.BlockSpec((None, FFN_NBLK, FFN_TF, D_MODEL), lambda i: (layer, 0, 0, 0), pipeline_mode=resident),
            pl.BlockSpec((None, 1, D_MODEL), lambda i: (layer, 0, 0)),
            pl.BlockSpec((None, 1, D_MODEL), lambda i: (layer, 0, 0)),
        ],
        out_specs=out_specs,
        out_shape=out_shape,
        scratch_shapes=[
            pltpu.VMEM((tm + 2 * HALO, D_MODEL), BF16),
            pltpu.VMEM((nchunk, ROW_CHUNK + 2 * HALO, FFN_TF), F32),
            pltpu.VMEM((nchunk, ROW_CHUNK + 2 * HALO, FFN_TF), F32),
            pltpu.VMEM((tm, D_MODEL), F32),
        ],
        compiler_params=_cparams(("arbitrary",)),
        name=f"ffn_l{layer}",
    )(x, x, x, mod, mod, mod, w_up, conv_w, w_down, ln_g, ln_b)


def _grid_pos_embed(n_tokens):
    rows = n_tokens // GRID_W
    r = jnp.broadcast_to(jnp.arange(rows, dtype=F32)[:, None], (rows, GRID_W)).reshape(-1)
    col = jnp.broadcast_to(jnp.arange(GRID_W, dtype=F32)[None, :], (rows, GRID_W)).reshape(-1)
    quarter = D_MODEL // 4
    freq = jnp.exp(-math.log(10000.0) * jnp.arange(quarter, dtype=F32) / quarter)
    ra, ca = r[:, None] * freq, col[:, None] * freq
    return jnp.concatenate([jnp.sin(ra), jnp.cos(ra), jnp.sin(ca), jnp.cos(ca)], -1)


def kernel(x_prompt, x_sample, state_gla, state_dn, c, c_ctx, w_mod, b_mod, ln1_g, ln1_b, ln2_g, ln2_b,
           a_w_in, a_w_gate, a_b_gate, a_norm, b_proj, b_scale, a_w_out,
           c_w_in, c_conv, c_a_log, c_dt_bias, c_norm, c_w_out, f_w_up, f_conv, f_w_down):
    n_ctx, seq, _ = x_prompt.shape
    n_smp, smp_len, _ = x_sample.shape
    assert seq == SEQ_TILE and n_smp + 1 <= MOD_ROWS
    cfg = _Cfg(n_ctx, n_smp, smp_len)

    x = _embed(cfg, x_prompt.reshape(-1, D_MODEL), x_sample.reshape(-1, D_MODEL), _grid_pos_embed(smp_len))
    cond = jnp.concatenate([c_ctx[None, :], c, jnp.zeros((MOD_ROWS - 1 - n_smp, D_MODEL), F32)], axis=0)
    mod = _modulation(cond, w_mod, b_mod)

    gla_consts = _gla_constants()
    tri = _tri_ones()
    row3 = lambda t: t.reshape(t.shape[0], 1, t.shape[1])
    ln1g, ln1b, ln2g, ln2b = row3(ln1_g), row3(ln1_b), row3(ln2_g), row3(ln2_b)
    s0_gla = state_gla.astype(F32).reshape(n_smp, -1, 2, A_HEADS // 2, 128, A_DV)
    s0_dn = state_dn.astype(F32)
    w_up_blk = _cast_blocks(f_w_up, FFN_TF, 2 * D_FF)
    w_down_blk = _cast_blocks(f_w_down, D_MODEL, D_MODEL).reshape(DEPTH, FFN_NBLK, FFN_TF, D_MODEL)
    ffn_conv = f_conv.reshape(DEPTH, 3, 2 * FFN_NBLK, FFN_TF).transpose(0, 2, 1, 3)
    gla_states, dn_states = [], []
    for layer in range(DEPTH):
        if layer % 2 == 0:
            e = layer // 2
            w_in = a_w_in[e]
            w_main = jnp.concatenate([w_in[:, :2 * A_QK + 2 * A_WIDTH], w_in[:, -B_WIDTH:]], axis=1)
            lr0 = 2 * A_QK + 2 * A_WIDTH
            w_lr = jnp.pad(w_in[:, lr0:lr0 + 2 * A_RANK], ((0, 0), (0, 128 - 2 * A_RANK)))
            w_gate_bd = jnp.zeros((128, 2 * A_QK), F32)
            w_gate_bd = w_gate_bd.at[:A_RANK, :A_QK].set(a_w_gate[e, 0]).at[A_RANK:2 * A_RANK, A_QK:].set(a_w_gate[e, 1])
            proj, gates = _proj_gates(cfg, x, mod, layer, w_main, 1024, w_lr, w_gate_bd,
                                      a_b_gate[e].reshape(1, 2 * A_QK))
            of, ob, st = _gla(cfg, proj, gates, s0_gla, e, gla_consts)
            gla_states.append(st.reshape(n_ctx, 2, A_HEADS, A_DK, A_DV))
            x = _out_even(cfg, of, ob, proj, x, mod, layer, row3(a_norm), b_proj, row3(b_scale), a_w_out, ln1g, ln1b)
        else:
            o_ = layer // 2
            w_zab = jnp.pad(c_w_in[o_][:, 3 * C_WIDTH:], ((0, 0), (0, 128 - 4 * C_HEADS)))
            qkv = _qkv_conv(cfg, x, mod, layer, c_w_in, c_conv, o_)
            zab = _proj(cfg, x, mod, layer, w_zab, w_zab.shape[1])
            pad16 = lambda t: jnp.pad(t.reshape(1, -1), ((0, 0), (0, 128 - 2 * C_HEADS)))
            of, ob, st = _delta(cfg, qkv, zab, s0_dn, o_, pad16(c_a_log[o_]), pad16(c_dt_bias[o_]), tri)
            dn_states.append(st)
            x = _out_odd(cfg, of, ob, zab, x, mod, layer, row3(c_norm), c_w_out, ln1g, ln1b)
        x = _ffn(cfg, x, mod, layer, w_up_blk, ffn_conv, w_down_blk, ln2g, ln2b, split_out=layer == DEPTH - 1)

    y_prompt = x[0].reshape(n_ctx, seq, D_MODEL)
    y_sample = x[1].reshape(n_smp, smp_len, D_MODEL)
    return (y_prompt, y_sample, jnp.stack(gla_states, axis=1).astype(x_prompt.dtype),
            jnp.stack(dn_states, axis=1).astype(x_prompt.dtype))
```

```python
import functools
import math

import numpy as np
import jax
import jax.numpy as jnp
from jax import lax
from jax.experimental import pallas as pl
from jax.experimental.pallas import tpu as pltpu

F32 = jnp.float32
BF16 = jnp.bfloat16

D_MODEL = 1024
DEPTH = 4
GRID_W = 64
A_HEADS = 4
A_DK = 64
A_DV = 128
A_QK = A_HEADS * A_DK
A_WIDTH = A_HEADS * A_DV
A_RANK = 16
A_GATE_NORM = 16.0
B_WIDTH = 512
POOL_WINDOWS = (2, 4, 8, 16)
B_GW = 128
C_HEADS = 8
C_DK = 128
C_WIDTH = 1024
D_FF = 2816
N_MOD = 6
ALPHA = (2 * DEPTH) ** 0.25
EPS = 1e-6

SEQ_TILE = 256
HALO = 8
ROW_CHUNK = 256
assert ROW_CHUNK == SEQ_TILE
MOD_ROWS = 8
V7X_VMEM_LIMIT = 56 * 1024 * 1024


def _cparams(sem):
    return pltpu.CompilerParams(dimension_semantics=sem, vmem_limit_bytes=V7X_VMEM_LIMIT)


class _Cfg:
    def __init__(self, n_ctx, n_smp, smp_len):
        assert smp_len % SEQ_TILE == 0 and smp_len & (smp_len - 1) == 0
        self.n_ctx, self.n_smp, self.smp_len = n_ctx, n_smp, smp_len
        self.t_ctx = n_ctx * SEQ_TILE
        self.t_all = self.t_ctx + n_smp * smp_len
        self.tps = smp_len // SEQ_TILE
        self.n_tiles = self.t_all // SEQ_TILE
        self.n_seq = n_ctx + n_smp
        tm = 1024
        while self.t_ctx % tm or smp_len % tm:
            tm //= 2
        self.tm = tm

    def cond_of_row(self, g):
        return jnp.where(g < self.t_ctx, 0, 1 + (g - self.t_ctx) // self.smp_len)

    def seq_of_tile(self, i):
        return jnp.where(i < self.n_ctx, i, self.n_ctx + (i - self.n_ctx) // self.tps)

    def rev_tile(self, i):
        m = (i - self.n_ctx) % self.tps
        return jnp.where(i < self.n_ctx, i, i - m + (self.tps - 1 - m))

    def chunk_edges(self, row0):
        is_ctx = row0 < self.t_ctx
        length = jnp.where(is_ctx, SEQ_TILE, self.smp_len)
        pos0 = jnp.bitwise_and(jnp.where(is_ctx, row0, row0 - self.t_ctx), length - 1)
        return pos0 == 0, pos0 + ROW_CHUNK == length

    def first_of_seq(self, i):
        return jnp.logical_or(i < self.n_ctx, (i - self.n_ctx) % self.tps == 0)

    def last_of_seq(self, i):
        return jnp.logical_or(i < self.n_ctx, (i - self.n_ctx) % self.tps == self.tps - 1)


def _mod_map(cfg, layer, which, rows):
    def index_map(i, *_):
        return ((layer * MOD_ROWS + cfg.cond_of_row(i * rows)) * N_MOD + which, 0, 0)
    return index_map


def _mod_spec(cfg, layer, which, rows):
    return pl.BlockSpec((None, 1, D_MODEL), _mod_map(cfg, layer, which, rows))


def _halo_specs(cfg, rows, width, col_block=0):
    per = rows // HALO
    last = cfg.t_all // HALO - 1
    prev = pl.BlockSpec((HALO, width), lambda i, *_: (jnp.maximum(i * per - 1, 0), col_block))
    nxt = pl.BlockSpec((HALO, width), lambda i, *_: (jnp.minimum((i + 1) * per, last), col_block))
    return prev, nxt


def _seq_edges(cfg, tile, rows):
    g = tile * rows + lax.broadcasted_iota(jnp.int32, (rows, 1), 0)
    is_ctx = g < cfg.t_ctx
    length = jnp.where(is_ctx, SEQ_TILE, cfg.smp_len)
    pos = jnp.bitwise_and(jnp.where(is_ctx, g, g - cfg.t_ctx), length - 1)
    return pos, length


def _silu(x):
    return x * jax.nn.sigmoid(x)


def _dot(a, b):
    return jnp.dot(a, b, preferred_element_type=F32)


def _dot_nt(a, b):
    return lax.dot_general(a, b, (((1,), (1,)), ((), ())), preferred_element_type=F32)


def _split_bf16(x):
    hi = x.astype(BF16)
    lo = (x - hi.astype(F32)).astype(BF16)
    return hi, lo


def _conv3(y_ref, cw, starts_seq, ends_seq):
    n = ROW_CHUNK
    y_ref[HALO - 1:HALO, :] = jnp.where(starts_seq, 0.0, y_ref[HALO - 1:HALO, :])
    y_ref[HALO + n:HALO + n + 1, :] = jnp.where(ends_seq, 0.0, y_ref[HALO + n:HALO + n + 1, :])
    return (y_ref[HALO - 1:HALO - 1 + n, :] * cw[0:1, :]
            + y_ref[HALO:HALO + n, :] * cw[1:2, :]
            + y_ref[HALO + 1:HALO + 1 + n, :] * cw[2:3, :])


def _layernorm_rows(x, g, b):
    mu = jnp.mean(x, axis=-1, keepdims=True)
    xc = x - mu
    var = jnp.mean(xc * xc, axis=-1, keepdims=True)
    return xc * lax.rsqrt(var + EPS) * g + b


def _state_specs(cfg, slot, per_seq):
    zeros = (0,) * len(per_seq)
    n_ctx = cfg.n_ctx
    ctx_seq = lambda i: jnp.minimum(cfg.seq_of_tile(i), n_ctx - 1)
    in_spec = pl.BlockSpec((None, None) + per_seq,
                           lambda i: (jnp.maximum(cfg.seq_of_tile(i) - n_ctx, 0), slot) + zeros)
    prev_spec = pl.BlockSpec((None, max(slot, 1)) + per_seq, lambda i: (ctx_seq(i), 0) + zeros)
    out_spec = pl.BlockSpec((None, slot + 1) + per_seq, lambda i: (ctx_seq(i), 0) + zeros)
    return in_spec, prev_spec, out_spec


def _load_states(cfg, i, s0_ref, sf_scr, sb_scr):
    @pl.when(cfg.first_of_seq(i))
    def _():
        is_ctx = i < cfg.n_ctx
        sf_scr[...] = jnp.where(is_ctx, 0.0, s0_ref[0])
        sb_scr[...] = jnp.where(is_ctx, 0.0, s0_ref[1])


def _store_states(cfg, i, slot, prev_ref, st_ref, sf_scr, sb_scr):
    @pl.when(i < cfg.n_ctx)
    def _():
        if slot:
            st_ref[0:slot] = prev_ref[...]
        st_ref[slot, 0] = sf_scr[...]
        st_ref[slot, 1] = sb_scr[...]


def _embed_kernel(n_ctx_tiles, xp_ref, xs_ref, pe_ref, o_ref):
    i = pl.program_id(0)

    @pl.when(i < n_ctx_tiles)
    def _():
        o_ref[...] = xp_ref[...]

    @pl.when(i >= n_ctx_tiles)
    def _():
        o_ref[...] = xs_ref[...] + pe_ref[...]


def _embed(cfg, xp, xs, pe):
    tm = cfg.tm
    n_ctx_tiles, tps = cfg.t_ctx // tm, cfg.smp_len // tm
    return pl.pallas_call(
        functools.partial(_embed_kernel, n_ctx_tiles),
        grid=(cfg.t_all // tm,),
        in_specs=[
            pl.BlockSpec((tm, D_MODEL), lambda i: (jnp.minimum(i, n_ctx_tiles - 1), 0)),
            pl.BlockSpec((tm, D_MODEL), lambda i: (jnp.maximum(i - n_ctx_tiles, 0), 0)),
            pl.BlockSpec((tm, D_MODEL), lambda i: (jnp.maximum(i - n_ctx_tiles, 0) % tps, 0)),
        ],
        out_specs=pl.BlockSpec((tm, D_MODEL), lambda i: (i, 0)),
        out_shape=jax.ShapeDtypeStruct((cfg.t_all, D_MODEL), F32),
        compiler_params=_cparams(("arbitrary",)),
        name="embed",
    )(xp, xs, pe)


CAST_ROWS = 256


def _cast_blocks_kernel(tn, nblk, col0, w_ref, o_ref):
    for c in range(nblk):
        o_ref[c] = w_ref[:, col0 + tn * c:col0 + tn * (c + 1)].astype(BF16)


def _cast_blocks(w, tn, ncols, col0=0):
    n_l, k, n = w.shape
    nblk = ncols // tn
    return pl.pallas_call(
        functools.partial(_cast_blocks_kernel, tn, nblk, col0),
        grid=(n_l, k // CAST_ROWS),
        in_specs=[pl.BlockSpec((None, CAST_ROWS, n), lambda l, r: (l, r, 0))],
        out_specs=pl.BlockSpec((None, nblk, CAST_ROWS, tn), lambda l, r: (l, 0, r, 0)),
        out_shape=jax.ShapeDtypeStruct((n_l, nblk, k, tn), BF16),
        compiler_params=_cparams(("arbitrary", "arbitrary")),
        name=f"cast_blocks_{n}",
    )(w)


def _mod_kernel(c_ref, w_ref, b_ref, o_ref):
    sc = _silu(c_ref[...]).astype(BF16)
    o_ref[...] = _dot(sc, w_ref[...].astype(BF16)) + b_ref[...]


def _modulation(cond, w_mod, b_mod):
    out = pl.pallas_call(
        _mod_kernel,
        grid=(DEPTH, N_MOD),
        in_specs=[
            pl.BlockSpec((MOD_ROWS, D_MODEL), lambda l, j: (0, 0)),
            pl.BlockSpec((None, D_MODEL, D_MODEL), lambda l, j: (l, 0, j)),
            pl.BlockSpec((None, 1, D_MODEL), lambda l, j: (l, 0, j)),
        ],
        out_specs=pl.BlockSpec((None, MOD_ROWS, D_MODEL), lambda l, j: (l, 0, j)),
        out_shape=jax.ShapeDtypeStruct((DEPTH, MOD_ROWS, N_MOD * D_MODEL), F32),
        compiler_params=_cparams(("arbitrary", "arbitrary")),
        name="modulation",
    )(cond, w_mod, b_mod.reshape(DEPTH, 1, N_MOD * D_MODEL))
    return out.reshape(DEPTH * MOD_ROWS * N_MOD, 1, D_MODEL)


def _proj_kernel(x_ref, sc_ref, sh_ref, w_ref, o_ref, u_scr):
    @pl.when(pl.program_id(1) == 0)
    def _():
        u_scr[...] = (x_ref[...] * (1.0 + sc_ref[...]) + sh_ref[...]).astype(BF16)

    o_ref[...] = _dot(u_scr[...], w_ref[...].astype(BF16))


def _proj(cfg, x, mod, layer, w, tn):
    n = w.shape[1]
    tm = cfg.tm
    return pl.pallas_call(
        _proj_kernel,
        grid=(cfg.t_all // tm, n // tn),
        in_specs=[
            pl.BlockSpec((tm, D_MODEL), lambda i, j: (i, 0)),
            _mod_spec(cfg, layer, 1, tm),
            _mod_spec(cfg, layer, 0, tm),
            pl.BlockSpec((D_MODEL, tn), lambda i, j: (0, j)),
        ],
        out_specs=pl.BlockSpec((tm, tn), lambda i, j: (i, j)),
        out_shape=jax.ShapeDtypeStruct((cfg.t_all, n), F32),
        scratch_shapes=[pltpu.VMEM((tm, D_MODEL), BF16)],
        compiler_params=_cparams(("arbitrary", "arbitrary")),
        name=f"proj_l{layer}",
    )(x, mod, mod, w)


def _proj_gates_kernel(x_ref, sc_ref, sh_ref, w_ref, wlr_ref, wg_ref, bg_ref, o_ref, g_ref, u_scr):
    @pl.when(pl.program_id(1) == 0)
    def _():
        u = (x_ref[...] * (1.0 + sc_ref[...]) + sh_ref[...]).astype(BF16)
        u_scr[...] = u
        lr = _dot(u, wlr_ref[...].astype(BF16))
        z = _dot(lr.astype(BF16), wg_ref[...].astype(BF16)) + bg_ref[...]
        g_ref[...] = (jnp.minimum(z, 0.0) - jnp.log1p(jnp.exp(-jnp.abs(z)))) / A_GATE_NORM

    o_ref[...] = _dot(u_scr[...], w_ref[...].astype(BF16))


def _proj_gates(cfg, x, mod, layer, w, tn, w_lr, w_gate_bd, b_gate):
    n = w.shape[1]
    tm = cfg.tm
    return pl.pallas_call(
        _proj_gates_kernel,
        grid=(cfg.t_all // tm, n // tn),
        in_specs=[
            pl.BlockSpec((tm, D_MODEL), lambda i, j: (i, 0)),
            _mod_spec(cfg, layer, 1, tm),
            _mod_spec(cfg, layer, 0, tm),
            pl.BlockSpec((D_MODEL, tn), lambda i, j: (0, j)),
            pl.BlockSpec((D_MODEL, 128), lambda i, j: (0, 0)),
            pl.BlockSpec((128, 2 * A_QK), lambda i, j: (0, 0)),
            pl.BlockSpec((1, 2 * A_QK), lambda i, j: (0, 0)),
        ],
        out_specs=[
            pl.BlockSpec((tm, tn), lambda i, j: (i, j)),
            pl.BlockSpec((tm, 2 * A_QK), lambda i, j: (i, 0)),
        ],
        out_shape=[
            jax.ShapeDtypeStruct((cfg.t_all, n), F32),
            jax.ShapeDtypeStruct((cfg.t_all, 2 * A_QK), F32),
        ],
        scratch_shapes=[pltpu.VMEM((tm, D_MODEL), BF16)],
        compiler_params=_cparams(("arbitrary", "arbitrary")),
        name=f"proj_gates_l{layer}",
    )(x, mod, mod, w, w_lr, w_gate_bd, b_gate)


GLA_LEVELS = 8
GLA_STACK = GLA_LEVELS + 2


def _gla_constants():
    n = SEQ_TILE
    idx = np.arange(n)
    i, m = idx[:, None], idx[None, :]
    stack = np.zeros((2, GLA_STACK, n, n), np.float32)
    masks = np.zeros((2, GLA_LEVELS + 1, n, n), np.float32)
    stack[0, 0] = m <= i
    stack[0, 1] = m > i
    stack[1, 0] = m >= i
    stack[1, 1] = m < i
    for l in range(GLA_LEVELS):
        s = 1 << l
        blk = idx // (2 * s)
        upper = (idx // s) % 2 == 1
        piv_f = blk * 2 * s + s - 1
        piv_b = blk * 2 * s + s
        up, pf, pb = upper[:, None], piv_f[:, None], piv_b[:, None]
        stack[0, 2 + l] = np.where(up, (m > pf) & (m <= i), (m > i) & (m <= pf))
        stack[1, 2 + l] = np.where(up, (m >= pb) & (m < i), (m >= i) & (m < pb))
        same = blk[:, None] == blk[None, :]
        masks[0, l] = same & upper[:, None] & ~upper[None, :]
        masks[1, l] = same & ~upper[:, None] & upper[None, :]
    masks[:, GLA_LEVELS] = np.eye(n)
    return (jnp.asarray(stack.reshape(2, GLA_STACK * n, n), BF16), jnp.asarray(masks, F32))


def _gla_direction(qk, v, g, stack_ref, mask_ref, s_scr, o_ref):
    n = SEQ_TILE
    q = qk[:, :A_QK] * (A_DK ** -0.5)
    k = qk[:, A_QK:]
    g_hi, g_lo = _split_bf16(g)
    sums = _dot(stack_ref[...], g_hi) + _dot(stack_ref[...], g_lo)
    e_in = jnp.exp(sums[0:n])
    e_out = jnp.exp(sums[n:2 * n])
    lane = lax.broadcasted_iota(jnp.int32, (n, 128), 1)
    low_half = lane < A_DK

    att = [jnp.zeros((n, n), F32) for _ in range(A_HEADS)]
    for l in range(GLA_LEVELS + 1):
        if l < GLA_LEVELS:
            e = jnp.exp(sums[(2 + l) * n:(3 + l) * n])
            qs, ks = q * e, k * e
        else:
            qs, ks = q, k
        m = mask_ref[l]
        for p in range(A_HEADS // 2):
            qp = qs[:, 128 * p:128 * (p + 1)]
            kp = ks[:, 128 * p:128 * (p + 1)].astype(BF16)
            for hh in range(2):
                qh = jnp.where(low_half if hh == 0 else ~low_half, qp, 0.0).astype(BF16)
                att[2 * p + hh] = att[2 * p + hh] + m * _dot_nt(qh, kp)

    qd = q * e_in
    kd = k * e_out
    kd_t = kd.T.astype(BF16)
    g_t = g.T
    gt_hi, gt_lo = _split_bf16(g_t)
    ones = jnp.ones((n, 128), BF16)
    dec = jnp.exp(_dot(gt_hi, ones) + _dot(gt_lo, ones))
    for h in range(A_HEADS):
        p, hh = h // 2, h % 2
        v_h = v[:, A_DV * h:A_DV * (h + 1)].astype(BF16)
        s_pair = s_scr[p]
        qp = qd[:, 128 * p:128 * (p + 1)]
        qh = jnp.where(low_half if hh == 0 else ~low_half, qp, 0.0).astype(BF16)
        o_h = _dot(att[h].astype(BF16), v_h) + _dot(qh, s_pair.astype(BF16))
        o_ref[:, A_DV * h:A_DV * (h + 1)] = o_h
    for h in range(A_HEADS):
        p, hh = h // 2, h % 2
        v_h = v[:, A_DV * h:A_DV * (h + 1)].astype(BF16)
        rows = slice(A_DK * hh, A_DK * (hh + 1))
        s_old = s_scr[p, rows, :]
        s_scr[p, rows, :] = s_old * dec[A_DK * h:A_DK * (h + 1), :] + _dot(kd_t[A_DK * h:A_DK * (h + 1), :], v_h)


def _gla_kernel(cfg, slot, qk_f, v_f, g_f, qk_b, v_b, g_b, s0_ref, stack_ref, mask_ref, *rest):
    prev_ref = rest[0] if slot else None
    of_ref, ob_ref, st_ref, sf_scr, sb_scr = rest[-5:]
    i = pl.program_id(0)

    _load_states(cfg, i, s0_ref, sf_scr, sb_scr)

    _gla_direction(qk_f[...], v_f[...], g_f[...], stack_ref.at[0], mask_ref.at[0], sf_scr, of_ref)
    _gla_direction(qk_b[...], v_b[...], g_b[...], stack_ref.at[1], mask_ref.at[1], sb_scr, ob_ref)

    _store_states(cfg, i, slot, prev_ref, st_ref, sf_scr, sb_scr)


def _gla(cfg, proj, gates, s0, slot, prev, consts):
    stack, masks = consts
    rev = cfg.rev_tile
    n = SEQ_TILE
    per_seq = (2, A_HEADS // 2, 128, A_DV)
    s0_spec, prev_spec, st_spec = _state_specs(cfg, slot, per_seq)
    return pl.pallas_call(
        functools.partial(_gla_kernel, cfg, slot),
        grid=(cfg.n_tiles,),
        in_specs=[
            pl.BlockSpec((n, 2 * A_QK), lambda i: (i, 0)),
            pl.BlockSpec((n, A_WIDTH), lambda i: (i, 1)),
            pl.BlockSpec((n, A_QK), lambda i: (i, 0)),
            pl.BlockSpec((n, 2 * A_QK), lambda i: (rev(i), 0)),
            pl.BlockSpec((n, A_WIDTH), lambda i: (rev(i), 1)),
            pl.BlockSpec((n, A_QK), lambda i: (rev(i), 1)),
            s0_spec,
            pl.BlockSpec(stack.shape, lambda i: (0, 0, 0)),
            pl.BlockSpec(masks.shape, lambda i: (0, 0, 0, 0)),
        ] + ([prev_spec] if slot else []),
        out_specs=[
            pl.BlockSpec((n, A_WIDTH), lambda i: (i, 0)),
            pl.BlockSpec((n, A_WIDTH), lambda i: (rev(i), 0)),
            st_spec,
        ],
        out_shape=[
            jax.ShapeDtypeStruct((cfg.t_all, A_WIDTH), F32),
            jax.ShapeDtypeStruct((cfg.t_all, A_WIDTH), F32),
            jax.ShapeDtypeStruct((cfg.n_ctx, slot + 1) + per_seq, F32),
        ],
        scratch_shapes=[pltpu.VMEM(per_seq[1:], F32), pltpu.VMEM(per_seq[1:], F32)],
        compiler_params=_cparams(("arbitrary",)),
        name="gla_scan",
    )(proj, proj, gates, proj, proj, gates, s0, stack, masks, *([prev] if slot else []))


def _rms_heads(o, gain, n_heads, width):
    parts = []
    for h in range(n_heads):
        oh = o[:, width * h:width * (h + 1)]
        parts.append(oh * lax.rsqrt(jnp.mean(oh * oh, axis=-1, keepdims=True) + EPS) * gain)
    return parts


def _out_even_kernel(cfg, of_ref, ob_ref, r_ref, pz_ref, pzp_ref, pzn_ref, x_ref, g1_ref,
                     norm_ref, bproj_ref, bscale_ref, wout_ref, lng_ref, lnb_ref, o_ref, ext_scr, w_scr):
    n = SEQ_TILE
    i = pl.program_id(0)

    @pl.when(i == 0)
    def _():
        w_scr[...] = wout_ref[...].astype(BF16)

    o = of_ref[...] + ob_ref[...]
    r = r_ref[...]
    heads = _rms_heads(o, norm_ref[...], A_HEADS, A_DV)
    parts = [(heads[h] * _silu(r[:, A_DV * h:A_DV * (h + 1)])).astype(BF16) for h in range(A_HEADS)]

    pos, length = _seq_edges(cfg, i, n)
    starts_seq, ends_seq = cfg.chunk_edges(i * n)
    ext_scr[0:HALO, :] = jnp.where(starts_seq, 0.0, pzp_ref[...])
    ext_scr[HALO:HALO + n, :] = pz_ref[...]
    ext_scr[HALO + n:, :] = jnp.where(ends_seq, 0.0, pzn_ref[...])
    for gi, win in enumerate(POOL_WINDOWS):
        lo = win // 2
        hi = win - 1 - lo
        cols = slice(B_GW * gi, B_GW * (gi + 1))
        acc = ext_scr[HALO - lo:HALO - lo + n, cols]
        for d in range(-lo + 1, hi + 1):
            acc = acc + ext_scr[HALO + d:HALO + d + n, cols]
        cnt = (jnp.minimum(pos + hi + 1, length) - jnp.maximum(pos - lo, 0)).astype(F32)
        pooled = (acc / cnt - ext_scr[HALO:HALO + n, cols]).astype(BF16)
        mixed = _dot(pooled, bproj_ref[gi].astype(BF16)) * bscale_ref[:, cols]
        parts.append(mixed.astype(BF16))

    y = _dot(jnp.concatenate(parts, axis=1), w_scr[...])
    o_ref[...] = _layernorm_rows(ALPHA * x_ref[...] + g1_ref[...] * y, lng_ref[...], lnb_ref[...])


def _out_even(cfg, of, ob, proj, x, mod, layer, norm_g, b_proj, b_scale, w_out, ln_g, ln_b):
    n, e = SEQ_TILE, layer // 2
    pz_prev, pz_next = _halo_specs(cfg, n, B_WIDTH, col_block=3)
    return pl.pallas_call(
        functools.partial(_out_even_kernel, cfg),
        grid=(cfg.n_tiles,),
        in_specs=[
            pl.BlockSpec((n, A_WIDTH), lambda i: (i, 0)),
            pl.BlockSpec((n, A_WIDTH), lambda i: (i, 0)),
            pl.BlockSpec((n, A_WIDTH), lambda i: (i, 2)),
            pl.BlockSpec((n, B_WIDTH), lambda i: (i, 3)),
            pz_prev, pz_next,
            pl.BlockSpec((n, D_MODEL), lambda i: (i, 0)),
            _mod_spec(cfg, layer, 2, n),
            pl.BlockSpec((None, 1, A_DV), lambda i: (e, 0, 0)),
            pl.BlockSpec((None, len(POOL_WINDOWS), B_GW, B_GW), lambda i: (e, 0, 0, 0)),
            pl.BlockSpec((None, 1, B_WIDTH), lambda i: (e, 0, 0)),
            pl.BlockSpec((None, D_MODEL, D_MODEL), lambda i: (e, 0, 0)),
            pl.BlockSpec((None, 1, D_MODEL), lambda i: (layer, 0, 0)),
            pl.BlockSpec((None, 1, D_MODEL), lambda i: (layer, 0, 0)),
        ],
        out_specs=pl.BlockSpec((n, D_MODEL), lambda i: (i, 0)),
        out_shape=jax.ShapeDtypeStruct((cfg.t_all, D_MODEL), F32),
        scratch_shapes=[pltpu.VMEM((n + 2 * HALO, B_WIDTH), F32), pltpu.VMEM((D_MODEL, D_MODEL), BF16)],
        compiler_params=_cparams(("arbitrary",)),
        name=f"out_even_l{layer}",
    )(of, ob, proj, proj, proj, proj, x, mod, norm_g, b_proj, b_scale, w_out, ln_g, ln_b)


def _qkv_conv_kernel(cfg, tm, tn, x_ref, xp_ref, xn_ref, sc_ref, sh_ref, w_ref, cw_ref, o_ref, u_scr, y_scr):
    i, j = pl.program_id(0), pl.program_id(1)

    @pl.when(j == 0)
    def _():
        sc, sh = 1.0 + sc_ref[...], sh_ref[...]
        u_scr[0:HALO, :] = (xp_ref[...] * sc + sh).astype(BF16)
        u_scr[HALO:HALO + tm, :] = (x_ref[...] * sc + sh).astype(BF16)
        u_scr[HALO + tm:, :] = (xn_ref[...] * sc + sh).astype(BF16)

    w = w_ref[...].astype(BF16)
    cw = cw_ref[...]
    is_q = j == 0
    is_qk = j < 2
    nchunk = tm // ROW_CHUNK

    def up(r):
        y_scr[r] = _dot(u_scr[ROW_CHUNK * r:ROW_CHUNK * (r + 1) + 2 * HALO, :], w)

    up(0)
    for r in range(nchunk):
        if r + 1 < nchunk:
            up(r + 1)
        rows = slice(ROW_CHUNK * r, ROW_CHUNK * (r + 1))
        h = _silu(_conv3(y_scr.at[r], cw, *cfg.chunk_edges(i * tm + ROW_CHUNK * r)))
        for c in range(tn // C_DK):
            hc = h[:, C_DK * c:C_DK * (c + 1)]
            inv = lax.rsqrt(jnp.sum(hc * hc, axis=-1, keepdims=True) + EPS)
            scale = jnp.where(is_qk, inv * jnp.where(is_q, C_DK ** -0.5, 1.0), 1.0)
            o_ref[rows, C_DK * c:C_DK * (c + 1)] = hc * scale


def _qkv_conv(cfg, x, mod, layer, w_in, conv_w, idx):
    tm, tn = cfg.tm, C_WIDTH
    n = 3 * C_WIDTH
    xp, xn = _halo_specs(cfg, tm, D_MODEL)
    return pl.pallas_call(
        functools.partial(_qkv_conv_kernel, cfg, tm, tn),
        grid=(cfg.t_all // tm, n // tn),
        in_specs=[
            pl.BlockSpec((tm, D_MODEL), lambda i, j: (i, 0)),
            xp, xn,
            _mod_spec(cfg, layer, 1, tm),
            _mod_spec(cfg, layer, 0, tm),
            pl.BlockSpec((None, D_MODEL, tn), lambda i, j: (idx, 0, j)),
            pl.BlockSpec((None, 3, tn), lambda i, j: (idx, 0, j)),
        ],
        out_specs=pl.BlockSpec((tm, tn), lambda i, j: (i, j)),
        out_shape=jax.ShapeDtypeStruct((cfg.t_all, n), F32),
        scratch_shapes=[pltpu.VMEM((tm + 2 * HALO, D_MODEL), BF16),
                        pltpu.VMEM((tm // ROW_CHUNK, ROW_CHUNK + 2 * HALO, tn), F32)],
        compiler_params=_cparams(("arbitrary", "arbitrary")),
        name=f"qkv_conv_l{layer}",
    )(x, x, x, mod, mod, w_in, conv_w)


DN_BLOCK = 64
DN_NBLK = SEQ_TILE // DN_BLOCK


def _tri_ones():
    n = SEQ_TILE
    idx = np.arange(n)
    lower = (idx[None, :] <= idx[:, None]).astype(np.float32)
    return jnp.asarray(np.stack([lower, lower.T]), BF16)


def _unit_tri_inverses(packs):
    n, w = DN_BLOCK, SEQ_TILE
    row = lax.broadcasted_iota(jnp.int32, (n, w), 0)
    col = jnp.bitwise_and(lax.broadcasted_iota(jnp.int32, (n, w), 1), n - 1)
    eye = (row == col).astype(F32)
    band = lax.broadcasted_iota(jnp.int32, (w, w), 0) // n
    keep = (band == lax.broadcasted_iota(jnp.int32, (w, w), 1) // n).astype(BF16)

    def block_diag(x):
        return jnp.concatenate([x.astype(BF16)] * DN_NBLK, axis=0) * keep

    ts = [eye - jnp.where((row >> 1) == (col >> 1), a, 0.0) for a in packs]
    for l in range(1, int(math.log2(n))):
        couple = jnp.logical_and((row >> (l + 1)) == (col >> (l + 1)), (row >> l) != (col >> l))
        ws = [_dot(jnp.where(couple, a, 0.0).astype(BF16), block_diag(t)) for a, t in zip(packs, ts)]
        ts = [t - _dot(t.astype(BF16), block_diag(wv)) for t, wv in zip(ts, ws)]
    return ts


def _delta_tile(dirs):
    n = SEQ_TILE
    row = lax.broadcasted_iota(jnp.int32, (n, n), 0)
    col = lax.broadcasted_iota(jnp.int32, (n, n), 1)
    probs = []
    for d, (q_ref, k_ref, v_ref, g, beta, tri_ref, s_scr, o_ref) in enumerate(dirs):
        incl = (col <= row) if d == 0 else (col >= row)
        strict = (col < row) if d == 0 else (col > row)
        last = n - 1 if d == 0 else 0
        g_hi, g_lo = _split_bf16(g)
        b_col = _dot(tri_ref[...], g_hi) + _dot(tri_ref[...], g_lo)
        b_row = b_col.T
        for h in range(C_HEADS):
            c = d * C_HEADS + h
            cols = slice(C_DK * h, C_DK * (h + 1))
            q, k, v = q_ref[:, cols], k_ref[:, cols], v_ref[:, cols]
            bc = b_col[:, c:c + 1]
            br = b_row[c:c + 1, :]
            b_last = b_col[last:last + 1, c:c + 1]
            ld = jnp.exp(jnp.where(incl, bc - br, -jnp.inf))
            bt = beta[:, 2 * C_HEADS + c:2 * C_HEADS + c + 1]
            kb = k * bt
            kbf = k.astype(BF16)
            e_col = jnp.exp(bc)
            probs.append(dict(
                d=d, h=h, cols=cols, s_scr=s_scr, o_ref=o_ref,
                a_mat=jnp.where(strict, _dot_nt(kb.astype(BF16), kbf) * ld, 0.0),
                aqk=(_dot_nt(q.astype(BF16), kbf) * ld).astype(BF16),
                rhs=jnp.concatenate([v * bt, kb * e_col], axis=1),
                qd=(q * e_col).astype(BF16),
                kd_t=(k * jnp.exp(b_last - bc)).T.astype(BF16),
                dec=jnp.exp(b_last)))

    blk = lambda b: slice(DN_BLOCK * b, DN_BLOCK * (b + 1))
    on_diag = (row // DN_BLOCK) == (col // DN_BLOCK)
    packs = []
    for p in probs:
        dm = jnp.where(on_diag, p["a_mat"], 0.0)
        packs.append(sum(dm[blk(b), :] for b in range(1, DN_NBLK)) + dm[blk(0), :])
    t_inv = _unit_tri_inverses(packs)

    xs = [[None] * DN_NBLK for _ in probs]
    for step in range(DN_NBLK):
        for pi, p in enumerate(probs):
            b = step if p["d"] == 0 else DN_NBLK - 1 - step
            resid = p["rhs"][blk(b), :]
            for bb in range(DN_NBLK):
                if xs[pi][bb] is not None:
                    resid = resid - _dot(p["a_mat"][blk(b), blk(bb)].astype(BF16), xs[pi][bb].astype(BF16))
            p["resid"] = resid
        for pi, p in enumerate(probs):
            b = step if p["d"] == 0 else DN_NBLK - 1 - step
            xs[pi][b] = _dot(t_inv[pi][:, blk(b)].astype(BF16), p["resid"].astype(BF16))

    for pi, p in enumerate(probs):
        x = jnp.concatenate(xs[pi], axis=0)
        p["u"], p["w"] = x[:, :C_DK], x[:, C_DK:].astype(BF16)
        p["s"] = p["s_scr"][p["h"]]
        p["sb"] = p["s"].astype(BF16)
    for p in probs:
        p["vb"] = (p["u"] - _dot(p["w"], p["sb"])).astype(BF16)
    for p in probs:
        p["o_ref"][:, p["cols"]] = _dot(p["qd"], p["sb"]) + _dot(p["aqk"], p["vb"])
    for p in probs:
        p["s_scr"][p["h"]] = p["s"] * p["dec"] + _dot(p["kd_t"], p["vb"])


def _delta_kernel(cfg, slot, qf, kf, vf, abf, qb, kb, vb, abb, s0_ref, alog_ref, dtb_ref, tri_ref, *rest):
    prev_ref = rest[0] if slot else None
    of_ref, ob_ref, st_ref, sf_scr, sb_scr = rest[-5:]
    i = pl.program_id(0)

    _load_states(cfg, i, s0_ref, sf_scr, sb_scr)

    lane = lax.broadcasted_iota(jnp.int32, (1, 128), 1)
    neg_a = jnp.where(lane < 2 * C_HEADS, -jnp.exp(alog_ref[...]), 0.0)

    def gate(ab):
        a_in = ab + dtb_ref[...]
        softplus = jnp.maximum(a_in, 0.0) + jnp.log1p(jnp.exp(-jnp.abs(a_in)))
        return neg_a * softplus, jax.nn.sigmoid(ab)

    g_f, beta_f = gate(abf[...])
    g_b, beta_b = gate(abb[...])
    _delta_tile([(qf, kf, vf, g_f, beta_f, tri_ref.at[0], sf_scr, of_ref),
                 (qb, kb, vb, g_b, beta_b, tri_ref.at[1], sb_scr, ob_ref)])

    _store_states(cfg, i, slot, prev_ref, st_ref, sf_scr, sb_scr)


def _delta(cfg, qkv, zab, s0, slot, prev, a_log, dt_bias, tri):
    rev = cfg.rev_tile
    n = SEQ_TILE
    per_seq = (2, C_HEADS, C_DK, C_DK)
    s0_spec, prev_spec, st_spec = _state_specs(cfg, slot, per_seq)
    ab_col = C_WIDTH // 128
    in_specs = []
    for tile_of in (lambda i: i, rev):
        for cb in range(3):
            in_specs.append(pl.BlockSpec((n, C_WIDTH), functools.partial(lambda i, f, c: (f(i), c), f=tile_of, c=cb)))
        in_specs.append(pl.BlockSpec((n, 128), functools.partial(lambda i, f: (f(i), ab_col), f=tile_of)))
    in_specs += [
        s0_spec,
        pl.BlockSpec((1, 128), lambda i: (0, 0)),
        pl.BlockSpec((1, 128), lambda i: (0, 0)),
        pl.BlockSpec(tri.shape, lambda i: (0, 0, 0)),
    ] + ([prev_spec] if slot else [])
    return pl.pallas_call(
        functools.partial(_delta_kernel, cfg, slot),
        grid=(cfg.n_tiles,),
        in_specs=in_specs,
        out_specs=[
            pl.BlockSpec((n, C_WIDTH), lambda i: (i, 0)),
            pl.BlockSpec((n, C_WIDTH), lambda i: (rev(i), 0)),
            st_spec,
        ],
        out_shape=[
            jax.ShapeDtypeStruct((cfg.t_all, C_WIDTH), F32),
            jax.ShapeDtypeStruct((cfg.t_all, C_WIDTH), F32),
            jax.ShapeDtypeStruct((cfg.n_ctx, slot + 1) + per_seq, F32),
        ],
        scratch_shapes=[pltpu.VMEM(per_seq[1:], F32), pltpu.VMEM(per_seq[1:], F32)],
        compiler_params=_cparams(("arbitrary",)),
        name="delta_scan",
    )(qkv, qkv, qkv, zab, qkv, qkv, qkv, zab, s0, a_log, dt_bias, tri, *([prev] if slot else []))


def _out_odd_kernel(of_ref, ob_ref, z_ref, x_ref, g1_ref, norm_ref, wout_ref, lng_ref, lnb_ref, o_ref, w_scr):
    @pl.when(pl.program_id(0) == 0)
    def _():
        w_scr[...] = wout_ref[...].astype(BF16)

    o = of_ref[...] + ob_ref[...]
    z = z_ref[...]
    heads = _rms_heads(o, norm_ref[...], C_HEADS, C_DK)
    parts = [(heads[h] * _silu(z[:, C_DK * h:C_DK * (h + 1)])).astype(BF16) for h in range(C_HEADS)]
    y = _dot(jnp.concatenate(parts, axis=1), w_scr[...])
    o_ref[...] = _layernorm_rows(ALPHA * x_ref[...] + g1_ref[...] * y, lng_ref[...], lnb_ref[...])


def _out_odd(cfg, of, ob, zab, x, mod, layer, norm_g, w_out, ln_g, ln_b):
    n, e = min(cfg.tm, 2 * SEQ_TILE), layer // 2
    return pl.pallas_call(
        _out_odd_kernel,
        grid=(cfg.t_all // n,),
        in_specs=[
            pl.BlockSpec((n, C_WIDTH), lambda i: (i, 0)),
            pl.BlockSpec((n, C_WIDTH), lambda i: (i, 0)),
            pl.BlockSpec((n, C_WIDTH), lambda i: (i, 0)),
            pl.BlockSpec((n, D_MODEL), lambda i: (i, 0)),
            _mod_spec(cfg, layer, 2, n),
            pl.BlockSpec((None, 1, C_DK), lambda i: (e, 0, 0)),
            pl.BlockSpec((None, D_MODEL, D_MODEL), lambda i: (e, 0, 0)),
            pl.BlockSpec((None, 1, D_MODEL), lambda i: (layer, 0, 0)),
            pl.BlockSpec((None, 1, D_MODEL), lambda i: (layer, 0, 0)),
        ],
        out_specs=pl.BlockSpec((n, D_MODEL), lambda i: (i, 0)),
        out_shape=jax.ShapeDtypeStruct((cfg.t_all, D_MODEL), F32),
        scratch_shapes=[pltpu.VMEM((D_MODEL, D_MODEL), BF16)],
        compiler_params=_cparams(("arbitrary",)),
        name=f"out_odd_l{layer}",
    )(of, ob, zab, x, mod, norm_g, w_out, ln_g, ln_b)


FFN_TF = 256
FFN_NBLK = D_FF // FFN_TF
FFN_TM = 1024


def _ffn_kernel(cfg, tm, split_out, x_ref, xp_ref, xn_ref, sc_ref, sh_ref, g2_ref, wup_ref, cw_ref,
                wd_ref, lng_ref, lnb_ref, *rest):
    out_refs, (u_scr, ya_scr, yg_scr, acc_scr) = rest[:-4], rest[-4:]
    i = pl.program_id(0)
    sc, sh = 1.0 + sc_ref[...], sh_ref[...]
    u_scr[0:HALO, :] = (xp_ref[...] * sc + sh).astype(BF16)
    u_scr[HALO:HALO + tm, :] = (x_ref[...] * sc + sh).astype(BF16)
    u_scr[HALO + tm:, :] = (xn_ref[...] * sc + sh).astype(BF16)
    acc_scr[...] = jnp.zeros_like(acc_scr)
    nchunk = tm // ROW_CHUNK
    edges = [cfg.chunk_edges(i * tm + ROW_CHUNK * r) for r in range(nchunk)]

    def ff_block(jb, carry):
        wa, wg, wd = wup_ref[jb], wup_ref[FFN_NBLK + jb], wd_ref[jb]
        ca, cg = cw_ref[jb], cw_ref[FFN_NBLK + jb]

        def up(r):
            u = u_scr[ROW_CHUNK * r:ROW_CHUNK * (r + 1) + 2 * HALO, :]
            ya_scr[r] = _dot(u, wa)
            yg_scr[r] = _dot(u, wg)

        up(0)
        for r in range(nchunk):
            if r + 1 < nchunk:
                up(r + 1)
            act = (_silu(_conv3(yg_scr.at[r], cg, *edges[r])) * _conv3(ya_scr.at[r], ca, *edges[r])).astype(BF16)
            acc_scr[ROW_CHUNK * r:ROW_CHUNK * (r + 1), :] += _dot(act, wd)
        return carry

    lax.fori_loop(0, FFN_NBLK, ff_block, 0)

    y = _layernorm_rows(ALPHA * x_ref[...] + g2_ref[...] * acc_scr[...], lng_ref[...], lnb_ref[...])
    if split_out:
        is_ctx = i * tm < cfg.t_ctx

        @pl.when(is_ctx)
        def _():
            out_refs[0][...] = y

        @pl.when(jnp.logical_not(is_ctx))
        def _():
            out_refs[1][...] = y
    else:
        out_refs[0][...] = y


def _ffn(cfg, x, mod, layer, w_up, conv_w, w_down, ln_g, ln_b, split_out=False):
    tm = min(cfg.tm, FFN_TM)
    n_ctx_tiles = cfg.t_ctx // tm
    xp, xn = _halo_specs(cfg, tm, D_MODEL)
    nchunk = tm // ROW_CHUNK
    resident = pl.Buffered(1)
    if split_out:
        out_specs = [
            pl.BlockSpec((tm, D_MODEL), lambda i: (jnp.minimum(i, n_ctx_tiles - 1), 0)),
            pl.BlockSpec((tm, D_MODEL), lambda i: (jnp.maximum(i - n_ctx_tiles, 0), 0)),
        ]
        out_shape = [jax.ShapeDtypeStruct((cfg.t_ctx, D_MODEL), F32),
                     jax.ShapeDtypeStruct((cfg.t_all - cfg.t_ctx, D_MODEL), F32)]
    else:
        out_specs = pl.BlockSpec((tm, D_MODEL), lambda i: (i, 0))
        out_shape = jax.ShapeDtypeStruct((cfg.t_all, D_MODEL), F32)
    return pl.pallas_call(
        functools.partial(_ffn_kernel, cfg, tm, split_out),
        grid=(cfg.t_all // tm,),
        in_specs=[
            pl.BlockSpec((tm, D_MODEL), lambda i: (i, 0)),
            xp, xn,
            _mod_spec(cfg, layer, 4, tm),
            _mod_spec(cfg, layer, 3, tm),
            _mod_spec(cfg, layer, 5, tm),
            pl.BlockSpec((None, 2 * FFN_NBLK, D_MODEL, FFN_TF), lambda i: (layer, 0, 0, 0), pipeline_mode=resident),
            pl.BlockSpec((None, 2 * FFN_NBLK, 3, FFN_TF), lambda i: (layer, 0, 0, 0)),
            pl.BlockSpec((None, FFN_NBLK, FFN_TF, D_MODEL), lambda i: (layer, 0, 0, 0), pipeline_mode=resident),
            pl.BlockSpec((None, 1, D_MODEL), lambda i: (layer, 0, 0)),
            pl.BlockSpec((None, 1, D_MODEL), lambda i: (layer, 0, 0)),
        ],
        out_specs=out_specs,
        out_shape=out_shape,
        scratch_shapes=[
            pltpu.VMEM((tm + 2 * HALO, D_MODEL), BF16),
            pltpu.VMEM((nchunk, ROW_CHUNK + 2 * HALO, FFN_TF), F32),
            pltpu.VMEM((nchunk, ROW_CHUNK + 2 * HALO, FFN_TF), F32),
            pltpu.VMEM((tm, D_MODEL), F32),
        ],
        compiler_params=_cparams(("arbitrary",)),
        name=f"ffn_l{layer}",
    )(x, x, x, mod, mod, mod, w_up, conv_w, w_down, ln_g, ln_b)


def _grid_pos_embed(n_tokens):
    rows = n_tokens // GRID_W
    r = jnp.broadcast_to(jnp.arange(rows, dtype=F32)[:, None], (rows, GRID_W)).reshape(-1)
    col = jnp.broadcast_to(jnp.arange(GRID_W, dtype=F32)[None, :], (rows, GRID_W)).reshape(-1)
    quarter = D_MODEL // 4
    freq = jnp.exp(-math.log(10000.0) * jnp.arange(quarter, dtype=F32) / quarter)
    ra, ca = r[:, None] * freq, col[:, None] * freq
    return jnp.concatenate([jnp.sin(ra), jnp.cos(ra), jnp.sin(ca), jnp.cos(ca)], -1)


def kernel(x_prompt, x_sample, state_gla, state_dn, c, c_ctx, w_mod, b_mod, ln1_g, ln1_b, ln2_g, ln2_b,
           a_w_in, a_w_gate, a_b_gate, a_norm, b_proj, b_scale, a_w_out,
           c_w_in, c_conv, c_a_log, c_dt_bias, c_norm, c_w_out, f_w_up, f_conv, f_w_down):
    n_ctx, seq, _ = x_prompt.shape
    n_smp, smp_len, _ = x_sample.shape
    assert seq == SEQ_TILE and n_smp + 1 <= MOD_ROWS
    cfg = _Cfg(n_ctx, n_smp, smp_len)

    x = _embed(cfg, x_prompt.reshape(-1, D_MODEL), x_sample.reshape(-1, D_MODEL), _grid_pos_embed(smp_len))
    cond = jnp.concatenate([c_ctx[None, :], c, jnp.zeros((MOD_ROWS - 1 - n_smp, D_MODEL), F32)], axis=0)
    mod = _modulation(cond, w_mod, b_mod)

    gla_consts = _gla_constants()
    tri = _tri_ones()
    row3 = lambda t: t.reshape(t.shape[0], 1, t.shape[1])
    ln1g, ln1b, ln2g, ln2b = row3(ln1_g), row3(ln1_b), row3(ln2_g), row3(ln2_b)
    s0_gla = state_gla.astype(F32).reshape(n_smp, -1, 2, A_HEADS // 2, 128, A_DV)
    s0_dn = state_dn.astype(F32)
    w_up_blk = _cast_blocks(f_w_up, FFN_TF, 2 * D_FF)
    w_down_blk = _cast_blocks(f_w_down, D_MODEL, D_MODEL).reshape(DEPTH, FFN_NBLK, FFN_TF, D_MODEL)
    ffn_conv = f_conv.reshape(DEPTH, 3, 2 * FFN_NBLK, FFN_TF).transpose(0, 2, 1, 3)
    gla_states, dn_states = None, None
    for layer in range(DEPTH):
        if layer % 2 == 0:
            e = layer // 2
            w_in = a_w_in[e]
            w_main = jnp.concatenate([w_in[:, :2 * A_QK + 2 * A_WIDTH], w_in[:, -B_WIDTH:]], axis=1)
            lr0 = 2 * A_QK + 2 * A_WIDTH
            w_lr = jnp.pad(w_in[:, lr0:lr0 + 2 * A_RANK], ((0, 0), (0, 128 - 2 * A_RANK)))
            w_gate_bd = jnp.zeros((128, 2 * A_QK), F32)
            w_gate_bd = w_gate_bd.at[:A_RANK, :A_QK].set(a_w_gate[e, 0]).at[A_RANK:2 * A_RANK, A_QK:].set(a_w_gate[e, 1])
            proj, gates = _proj_gates(cfg, x, mod, layer, w_main, 1024, w_lr, w_gate_bd,
                                      a_b_gate[e].reshape(1, 2 * A_QK))
            of, ob, gla_states = _gla(cfg, proj, gates, s0_gla, e, gla_states, gla_consts)
            x = _out_even(cfg, of, ob, proj, x, mod, layer, row3(a_norm), b_proj, row3(b_scale), a_w_out, ln1g, ln1b)
        else:
            o_ = layer // 2
            w_zab = jnp.pad(c_w_in[o_][:, 3 * C_WIDTH:], ((0, 0), (0, 128 - 4 * C_HEADS)))
            qkv = _qkv_conv(cfg, x, mod, layer, c_w_in, c_conv, o_)
            zab = _proj(cfg, x, mod, layer, w_zab, w_zab.shape[1])
            pad16 = lambda t: jnp.pad(t.reshape(1, -1), ((0, 0), (0, 128 - 2 * C_HEADS)))
            of, ob, dn_states = _delta(cfg, qkv, zab, s0_dn, o_, dn_states, pad16(c_a_log[o_]), pad16(c_dt_bias[o_]), tri)
            x = _out_odd(cfg, of, ob, zab, x, mod, layer, row3(c_norm), c_w_out, ln1g, ln1b)
        x = _ffn(cfg, x, mod, layer, w_up_blk, ffn_conv, w_down_blk, ln2g, ln2b, split_out=layer == DEPTH - 1)

    y_prompt = x[0].reshape(n_ctx, seq, D_MODEL)
    y_sample = x[1].reshape(n_smp, smp_len, D_MODEL)
    new_gla = gla_states.reshape(n_ctx, DEPTH - DEPTH // 2, 2, A_HEADS, A_DK, A_DV)
    return y_prompt, y_sample, new_gla.astype(x_prompt.dtype), dn_states.astype(x_prompt.dtype)
```
